```python
import math
import jax
import jax.numpy as jnp
from jax import lax
import numpy as np

D_MODEL = 1024
BATCH = 16
SEQ = 2048
DEPTH = 2

MEM_LEN = 256
HEAD_DIM = 64
D_MIX = D_MODEL
RET_WIDTH = 3 * D_MIX // 8
SSD_WIDTH = 3 * D_MIX // 8
LRU_WIDTH = D_MIX - RET_WIDTH - SSD_WIDTH
RET_HEADS = RET_WIDTH // HEAD_DIM
SSD_HEADS = SSD_WIDTH // HEAD_DIM
SSD_GROUPS = 2
SSD_HEADS_PER_GROUP = SSD_HEADS // SSD_GROUPS
SSD_STATE = 128
SSD_CONV_DIM = SSD_WIDTH + 2 * SSD_GROUPS * SSD_STATE
LRU_BLOCKS = LRU_WIDTH // HEAD_DIM
LRU_C = 8.0
CONV_WIDTH = 4
CHUNK = 128
ROPE_BASE = 10000.0
CA_HEADS = 4
CA_HEAD_DIM = D_MODEL // CA_HEADS
N_GROUPS = 4
EXPERTS_PER_GROUP = 8
N_EXPERTS = N_GROUPS * EXPERTS_PER_GROUP
TOP_K = 2
D_EXPERT = D_MODEL // 2
MOE_BLOCK = 128
EPS = 1e-6
IN_SPLITS = (RET_WIDTH, RET_WIDTH, RET_WIDTH, RET_WIDTH, SSD_WIDTH, SSD_CONV_DIM, SSD_HEADS, LRU_WIDTH, LRU_WIDTH)
D_IN = sum(IN_SPLITS)
SPLIT_POINTS = tuple(int(p) for p in np.cumsum(IN_SPLITS)[:-1])

kernel_name = 'hybrid_ret_ssd_rglru_xattn_hmoe_encoder'


def rms_norm(x, g):
    xf = x.astype(jnp.float32)
    y = xf * lax.rsqrt(jnp.mean(xf * xf, axis=-1, keepdims=True) + EPS)
    return (y * g.astype(jnp.float32)).astype(x.dtype)


def _rev(t):
    return jnp.flip(t, axis=1)


def centred_dwconv(x, w, b):
    k = w.shape[0]
    left = k // 2
    y = lax.conv_general_dilated(x, w[:, None, :].astype(x.dtype), window_strides=(1,),
                                 padding=[(left, k - 1 - left)],
                                 dimension_numbers=('NWC', 'WIO', 'NWC'),
                                 feature_group_count=x.shape[-1])
    return y + b.astype(y.dtype)


def rotary(x):
    s, dh = x.shape[1], x.shape[-1]
    half = dh // 2
    inv = ROPE_BASE ** (-jnp.arange(half, dtype=jnp.float32) / half)
    ang = jnp.arange(s, dtype=jnp.float32)[:, None] * inv[None, :]
    cos = jnp.cos(ang)[:, None, :]
    sin = jnp.sin(ang)[:, None, :]
    x1 = x[..., :half].astype(jnp.float32)
    x2 = x[..., half:].astype(jnp.float32)
    return jnp.concatenate([x1 * cos - x2 * sin, x1 * sin + x2 * cos], axis=-1)


def _retention_causal(q, k, v, log_gamma, strict):
    b, s, h, dh = q.shape
    n = s // CHUNK
    q = q.reshape(b, n, CHUNK, h, dh)
    k = k.reshape(b, n, CHUNK, h, dh)
    v = v.reshape(b, n, CHUNK, h, v.shape[-1])
    idx = jnp.arange(CHUNK)
    diff = (idx[:, None] - idx[None, :]).astype(jnp.float32)
    mask = diff > 0 if strict else diff >= 0
    decay = jnp.where(mask, jnp.exp(jnp.where(mask, diff, 0.0)[None] * log_gamma[:, None, None]), 0.0)
    scores = jnp.einsum('bnihd,bnjhd->bnhij', q, k) * decay
    inner = jnp.einsum('bnhij,bnjhe->bnihe', scores, v)
    pos = idx.astype(jnp.float32)[:, None]
    zeta = jnp.exp((CHUNK - 1 - pos) * log_gamma)
    xi = jnp.exp((pos + 1) * log_gamma)
    states = jnp.einsum('bnjhd,jh,bnjhe->bnhde', k, zeta, v)
    chunk_decay = jnp.exp(CHUNK * log_gamma)[:, None, None]

    def step(r, st):
        return r * chunk_decay + st, r

    _, r_prev = lax.scan(step, jnp.zeros_like(states[:, 0]), jnp.moveaxis(states, 1, 0))
    r_prev = jnp.moveaxis(r_prev, 0, 1)
    cross = jnp.einsum('bnihd,bnhde->bnihe', q, r_prev) * xi[None, None, :, :, None]
    return (inner + cross).reshape(b, s, h, -1)


def retention_mixer(q, k, v, g):
    b, s, _ = q.shape
    shp = (b, s, RET_HEADS, HEAD_DIM)
    q = rotary(q.reshape(shp))
    k = rotary(k.reshape(shp)) * HEAD_DIM ** -0.5
    v = v.reshape(shp).astype(jnp.float32)
    log_gamma = jnp.log1p(-jnp.exp2(-5.0 - jnp.arange(RET_HEADS, dtype=jnp.float32)))
    o = _retention_causal(q, k, v, log_gamma, strict=False)
    o = o + _rev(_retention_causal(_rev(q), _rev(k), _rev(v), log_gamma, strict=True))
    o = o * lax.rsqrt(jnp.mean(o * o, axis=-1, keepdims=True) + EPS)
    return jax.nn.silu(g.astype(jnp.float32)) * o.reshape(b, s, RET_WIDTH)


def _ssd_causal(x, dt, log_a, bm, cm, strict):
    b, s, g, r, p = x.shape
    n = s // CHUNK
    x = x.reshape(b, n, CHUNK, g, r, p)
    dt = dt.reshape(b, n, CHUNK, g, r)
    log_a = log_a.reshape(b, n, CHUNK, g, r)
    bm = bm.reshape(b, n, CHUNK, g, -1)
    cm = cm.reshape(b, n, CHUNK, g, -1)
    a_cum = jnp.cumsum(log_a, axis=2)
    idx = jnp.arange(CHUNK)
    mask = (idx[:, None] > idx[None, :]) if strict else (idx[:, None] >= idx[None, :])
    mask = mask[None, None, :, :, None, None]
    seg = a_cum[:, :, :, None] - a_cum[:, :, None, :]
    lmat = jnp.where(mask, jnp.exp(jnp.where(mask, seg, 0.0)), 0.0)
    xdt = x * dt[..., None]
    cb = jnp.einsum('bnigs,bnjgs->bnijg', cm, bm)
    y_diag = jnp.einsum('bnijgr,bnjgrp->bnigrp', cb[..., None] * lmat, xdt)
    decay_states = jnp.exp(a_cum[:, :, -1:] - a_cum)
    states = jnp.einsum('bnlgs,bnlgr,bnlgrp->bngrps', bm, decay_states, xdt)
    chunk_decay = jnp.exp(a_cum[:, :, -1])

    def step(hst, inp):
        st, d = inp
        return hst * d[..., None, None] + st, hst

    _, h_prev = lax.scan(step, jnp.zeros_like(states[:, 0]),
                         (jnp.moveaxis(states, 1, 0), jnp.moveaxis(chunk_decay, 1, 0)))
    h_prev = jnp.moveaxis(h_prev, 0, 1)
    y_off = jnp.einsum('bnigs,bngrps,bnigr->bnigrp', cm, h_prev, jnp.exp(a_cum))
    return (y_diag + y_off).reshape(b, s, g, r, p)


def ssd_mixer(z, xbc, dt_raw, conv_w, conv_b, dt_bias, a_log, d_skip, norm_w):
    b, s, _ = z.shape
    f32 = jnp.float32
    xbc = jax.nn.silu(centred_dwconv(xbc, conv_w, conv_b).astype(f32))
    xs, bm, cm = jnp.split(xbc, [SSD_WIDTH, SSD_WIDTH + SSD_GROUPS * SSD_STATE], axis=-1)
    xs = xs.reshape(b, s, SSD_GROUPS, SSD_HEADS_PER_GROUP, HEAD_DIM)
    bm = bm.reshape(b, s, SSD_GROUPS, SSD_STATE)
    cm = cm.reshape(b, s, SSD_GROUPS, SSD_STATE)
    hs = (SSD_GROUPS, SSD_HEADS_PER_GROUP)
    dt_raw = dt_raw.astype(f32)
    dt_f = jax.nn.softplus(dt_raw + dt_bias[0].astype(f32)).reshape(b, s, *hs)
    dt_b = jax.nn.softplus(dt_raw + dt_bias[1].astype(f32)).reshape(b, s, *hs)
    la_f = -dt_f * jnp.exp(a_log[0].astype(f32)).reshape(hs)
    la_b = -dt_b * jnp.exp(a_log[1].astype(f32)).reshape(hs)
    y_f = _ssd_causal(xs, dt_f, la_f, bm, cm, strict=False)
    y_b = _rev(_ssd_causal(_rev(xs), _rev(dt_b), _rev(la_b), _rev(bm), _rev(cm), strict=True))
    y = y_f + y_b + d_skip.astype(f32).reshape(hs)[..., None] * xs
    y = y.reshape(b, s, SSD_WIDTH) * jax.nn.silu(z.astype(f32))
    y = y.reshape(b, s, SSD_GROUPS, SSD_WIDTH // SSD_GROUPS)
    y = y * lax.rsqrt(jnp.mean(y * y, axis=-1, keepdims=True) + EPS)
    return y.reshape(b, s, SSD_WIDTH) * norm_w.astype(f32)


def _linear_scan(a, u, reverse):
    def combine(lhs, rhs):
        a_l, u_l = lhs
        a_r, u_r = rhs
        return a_l * a_r, a_r * u_l + u_r

    _, h = lax.associative_scan(combine, (a, u), axis=1, reverse=reverse)
    return h


def _rglru_direction(xc, lam, wa, ba, wx, bx, reverse):
    b, s, _ = xc.shape
    f32 = jnp.float32
    xh = xc.reshape(b, s, LRU_BLOCKS, HEAD_DIM)
    r_gate = jax.nn.sigmoid(jnp.einsum('bshi,hij->bshj', xh, wa.astype(f32)).reshape(b, s, LRU_WIDTH) + ba.astype(f32))
    i_gate = jax.nn.sigmoid(jnp.einsum('bshi,hij->bshj', xh, wx.astype(f32)).reshape(b, s, LRU_WIDTH) + bx.astype(f32))
    log_a = -LRU_C * r_gate * jax.nn.softplus(-lam.astype(f32))
    a = jnp.exp(log_a)
    u = jnp.sqrt(-jnp.expm1(2.0 * log_a)) * (i_gate * xc)
    return _linear_scan(a, u, reverse)


def lru_mixer(xr, gate, conv_w, conv_b, lam, wa, ba, wx, bx):
    xc = centred_dwconv(xr, conv_w, conv_b).astype(jnp.float32)
    h = _rglru_direction(xc, lam[0], wa[0], ba[0], wx[0], bx[0], reverse=False)
    h = h + _rglru_direction(xc, lam[1], wa[1], ba[1], wx[1], bx[1], reverse=True)
    return h * jax.nn.gelu(gate.astype(jnp.float32))


def hybrid_mixer(xn, w_in, w_out, ssd_conv_w, ssd_conv_b, ssd_dt_bias, ssd_a_log, ssd_d, ssd_norm,
                 lru_conv_w, lru_conv_b, lru_lambda, lru_wa, lru_ba, lru_wx, lru_bx):
    proj = xn @ w_in
    q, k, v, g, z, xbc, dt_raw, xr, gate = jnp.split(proj, SPLIT_POINTS, axis=-1)
    o_ret = retention_mixer(q, k, v, g)
    o_ssd = ssd_mixer(z, xbc, dt_raw, ssd_conv_w, ssd_conv_b, ssd_dt_bias, ssd_a_log, ssd_d, ssd_norm)
    o_lru = lru_mixer(xr, gate, lru_conv_w, lru_conv_b, lru_lambda, lru_wa, lru_ba, lru_wx, lru_bx)
    o = jnp.concatenate([o_ret, o_ssd, o_lru], axis=-1).astype(xn.dtype)
    return o @ w_out


def memory_cross_attention(xn, mn, wq, wkv, wo):
    b, s, d = xn.shape
    m = mn.shape[1]
    q = (xn @ wq).reshape(b, s, CA_HEADS, CA_HEAD_DIM)
    k, v = jnp.split(mn @ wkv, 2, axis=-1)
    k = k.reshape(b, m, CA_HEADS, CA_HEAD_DIM)
    v = v.reshape(b, m, CA_HEADS, CA_HEAD_DIM)
    scores = jnp.einsum('bshd,bmhd->bhsm', q, k).astype(jnp.float32) * CA_HEAD_DIM ** -0.5
    p = jax.nn.softmax(scores, axis=-1).astype(v.dtype)
    o = jnp.einsum('bhsm,bmhd->bshd', p, v).reshape(b, s, d)
    return o @ wo


def _expert_dispatch(xt, expert_id, gates, w1, w3, w2):
    t, d = xt.shape
    n_assign = t * TOP_K
    e = expert_id.reshape(n_assign).astype(jnp.int32)
    tok = jnp.repeat(jnp.arange(t, dtype=jnp.int32), TOP_K)
    wts = gates.reshape(n_assign)
    order = jnp.argsort(e)
    e_s, tok_s, w_s = e[order], tok[order], wts[order]
    counts = jnp.zeros((N_EXPERTS,), jnp.int32).at[e].add(1)
    padded = (counts + MOE_BLOCK - 1) // MOE_BLOCK * MOE_BLOCK
    start = jnp.cumsum(counts) - counts
    pad_end = jnp.cumsum(padded)
    pad_start = pad_end - padded
    dest = pad_start[e_s] + jnp.arange(n_assign, dtype=jnp.int32) - start[e_s]
    n_blocks = -(-n_assign // MOE_BLOCK) + N_EXPERTS
    slots = n_blocks * MOE_BLOCK
    buf_tok = jnp.zeros((slots,), jnp.int32).at[dest].set(tok_s)
    buf_w = jnp.zeros((slots,), wts.dtype).at[dest].set(w_s)
    block_start = jnp.arange(n_blocks, dtype=jnp.int32) * MOE_BLOCK
    block_e = jnp.minimum(jnp.searchsorted(pad_end, block_start, side='right'), N_EXPERTS - 1)
    xb = xt[buf_tok].reshape(n_blocks, MOE_BLOCK, d)

    def expert_block(args):
        xblk, eid = args
        hid = jax.nn.silu(xblk @ w1[eid]) * (xblk @ w3[eid])
        return hid @ w2[eid]

    yb = lax.map(expert_block, (xb, block_e)).reshape(slots, d)
    return jnp.zeros_like(xt).at[buf_tok].add(yb * buf_w[:, None])


def hierarchical_moe(xn, wg, bg, we, be, w1, w3, w2):
    b, s, d = xn.shape
    xt = xn.reshape(b * s, d)
    t = xt.shape[0]
    group_p = jax.nn.softmax((xt @ wg).astype(jnp.float32) + bg.astype(jnp.float32), axis=-1)
    gp, gi = lax.top_k(group_p, 1)
    e_logits = ((xt @ we).astype(jnp.float32) + be.astype(jnp.float32)).reshape(t, N_GROUPS, EXPERTS_PER_GROUP)
    e_logits = jnp.take_along_axis(e_logits, gi[:, :, None], axis=1)[:, 0]
    ep, ei = lax.top_k(jax.nn.softmax(e_logits, axis=-1), TOP_K)
    gates = gp * ep / jnp.sum(ep, axis=-1, keepdims=True)
    expert_id = gi * EXPERTS_PER_GROUP + ei
    y = _expert_dispatch(xt, expert_id, gates.astype(xt.dtype), w1, w3, w2)
    return y.reshape(b, s, d)


def setup_inputs(seed: int = 0) -> dict:
    key = jax.random.key(seed)
    keys = iter(jax.random.split(key, 48))
    f32 = jnp.float32
    L = DEPTH

    def normal(shape, scale):
        return jax.random.normal(next(keys), shape, f32) * scale

    def uniform(shape, lo, hi):
        return jax.random.uniform(next(keys), shape, f32, lo, hi)

    def gain(shape):
        return 1.0 + normal(shape, 0.02)

    x = normal((BATCH, SEQ, D_MODEL), 1.0)
    mem = normal((BATCH, MEM_LEN, D_MODEL), 1.0)
    g_mix = gain((L, D_MODEL))
    w_in = normal((L, D_MODEL, D_IN), D_MODEL ** -0.5)
    w_out = normal((L, D_MIX, D_MODEL), D_MIX ** -0.5)
    ssd_conv_w = normal((L, CONV_WIDTH, SSD_CONV_DIM), CONV_WIDTH ** -0.5)
    ssd_conv_b = normal((L, SSD_CONV_DIM), 0.02)
    dt0 = jnp.exp(uniform((L, 2, SSD_HEADS), math.log(1e-3), math.log(1e-1)))
    ssd_dt_bias = dt0 + jnp.log(-jnp.expm1(-dt0))
    ssd_a_log = jnp.log(uniform((L, 2, SSD_HEADS), 1.0, 16.0))
    ssd_d = 1.0 + normal((L, SSD_HEADS), 0.1)
    ssd_norm = gain((L, SSD_WIDTH))
    lru_conv_w = normal((L, CONV_WIDTH, LRU_WIDTH), CONV_WIDTH ** -0.5)
    lru_conv_b = normal((L, LRU_WIDTH), 0.02)
    a_root = uniform((L, 2, LRU_WIDTH), 0.9, 0.999) ** (1.0 / LRU_C)
    lru_lambda = jnp.log(a_root) - jnp.log1p(-a_root)
    lru_wa = normal((L, 2, LRU_BLOCKS, HEAD_DIM, HEAD_DIM), HEAD_DIM ** -0.5)
    lru_ba = normal((L, 2, LRU_WIDTH), 0.02)
    lru_wx = normal((L, 2, LRU_BLOCKS, HEAD_DIM, HEAD_DIM), HEAD_DIM ** -0.5)
    lru_bx = normal((L, 2, LRU_WIDTH), 0.02)
    g_ca = gain((L, D_MODEL))
    g_mem = gain((L, D_MODEL))
    ca_wq = normal((L, D_MODEL, D_MODEL), D_MODEL ** -0.5)
    ca_wkv = normal((L, D_MODEL, 2 * D_MODEL), D_MODEL ** -0.5)
    ca_wo = normal((L, D_MODEL, D_MODEL), D_MODEL ** -0.5)
    g_moe = gain((L, D_MODEL))
    moe_wg = normal((L, D_MODEL, N_GROUPS), D_MODEL ** -0.5)
    moe_bg = normal((L, N_GROUPS), 0.01)
    moe_we = normal((L, D_MODEL, N_EXPERTS), D_MODEL ** -0.5)
    moe_be = normal((L, N_EXPERTS), 0.01)
    moe_w1 = normal((L, N_EXPERTS, D_MODEL, D_EXPERT), D_MODEL ** -0.5)
    moe_w3 = normal((L, N_EXPERTS, D_MODEL, D_EXPERT), D_MODEL ** -0.5)
    moe_w2 = normal((L, N_EXPERTS, D_EXPERT, D_MODEL), D_EXPERT ** -0.5)
    g_final = gain((D_MODEL,))
    return {'x': x, 'mem': mem, 'g_mix': g_mix, 'w_in': w_in, 'w_out': w_out,
            'ssd_conv_w': ssd_conv_w, 'ssd_conv_b': ssd_conv_b, 'ssd_dt_bias': ssd_dt_bias,
            'ssd_a_log': ssd_a_log, 'ssd_d': ssd_d, 'ssd_norm': ssd_norm,
            'lru_conv_w': lru_conv_w, 'lru_conv_b': lru_conv_b, 'lru_lambda': lru_lambda,
            'lru_wa': lru_wa, 'lru_ba': lru_ba, 'lru_wx': lru_wx, 'lru_bx': lru_bx,
            'g_ca': g_ca, 'g_mem': g_mem, 'ca_wq': ca_wq, 'ca_wkv': ca_wkv, 'ca_wo': ca_wo,
            'g_moe': g_moe, 'moe_wg': moe_wg, 'moe_bg': moe_bg, 'moe_we': moe_we, 'moe_be': moe_be,
            'moe_w1': moe_w1, 'moe_w3': moe_w3, 'moe_w2': moe_w2, 'g_final': g_final}


def reference(x, mem, g_mix, w_in, w_out, ssd_conv_w, ssd_conv_b, ssd_dt_bias, ssd_a_log, ssd_d, ssd_norm,
              lru_conv_w, lru_conv_b, lru_lambda, lru_wa, lru_ba, lru_wx, lru_bx,
              g_ca, g_mem, ca_wq, ca_wkv, ca_wo,
              g_moe, moe_wg, moe_bg, moe_we, moe_be, moe_w1, moe_w3, moe_w2, g_final):
    h = x
    for l in range(DEPTH):
        h = h + hybrid_mixer(rms_norm(h, g_mix[l]), w_in[l], w_out[l],
                             ssd_conv_w[l], ssd_conv_b[l], ssd_dt_bias[l], ssd_a_log[l], ssd_d[l], ssd_norm[l],
                             lru_conv_w[l], lru_conv_b[l], lru_lambda[l], lru_wa[l], lru_ba[l], lru_wx[l], lru_bx[l])
        h = h + memory_cross_attention(rms_norm(h, g_ca[l]), rms_norm(mem, g_mem[l]), ca_wq[l], ca_wkv[l], ca_wo[l])
        h = h + hierarchical_moe(rms_norm(h, g_moe[l]), moe_wg[l], moe_bg[l], moe_we[l], moe_be[l],
                                 moe_w1[l], moe_w3[l], moe_w2[l])
    return rms_norm(h, g_final)
```

```python
import functools
import math

import jax
import jax.numpy as jnp
import numpy as np
from jax import lax
from jax.experimental import pallas as pl
from jax.experimental.pallas import tpu as pltpu

D_MODEL = 1024
HEAD_DIM = 64
RET_WIDTH = 384
SSD_WIDTH = 384
LRU_WIDTH = 256
RET_HEADS = 6
SSD_HEADS = 6
SSD_GROUPS = 2
SSD_HPG = 3
SSD_STATE = 128
SSD_CONV_DIM = SSD_WIDTH + 2 * SSD_GROUPS * SSD_STATE
LRU_BLOCKS = 4
LRU_C = 8.0
CONV_WIDTH = 4
ROPE_BASE = 10000.0
CA_HEADS = 4
CA_HEAD_DIM = 256
N_GROUPS = 4
EXPERTS_PER_GROUP = 8
N_EXPERTS = 32
TOP_K = 2
D_EXPERT = 512
EPS = 1e-6

_OFF_Q, _OFF_K, _OFF_V, _OFF_G = 0, 384, 768, 1152
_OFF_Z = 1536
_OFF_XBC = 1920
_OFF_DT = 2816
_OFF_XR = 2822
_OFF_GATE = 3078
D_IN = 3334

LANE = 128
SUBLANE = 8
VMEM_LIMIT = 56 * 1024 * 1024

CHUNK = 128
ROW_TILE = 512
SSD_GW = 256

F32 = jnp.float32
BF16 = jnp.bfloat16


def _cparams(sem):
    return pltpu.CompilerParams(dimension_semantics=sem, vmem_limit_bytes=VMEM_LIMIT)


def _rms(x, g):
    return x * lax.rsqrt(jnp.mean(x * x, axis=-1, keepdims=True) + EPS) * g


def _inproj_kernel(x_ref, g_ref, wr_ref, ws_ref, wl_ref, pr_ref, ps_ref, pq_ref):
    xn = _rms(x_ref[...], g_ref[...]).astype(BF16)
    pr_ref[...] = jnp.dot(xn, wr_ref[...], preferred_element_type=F32)
    ps_ref[...] = jnp.dot(xn, ws_ref[...], preferred_element_type=F32)
    pq_ref[...] = jnp.dot(xn, wl_ref[...], preferred_element_type=F32)


def _inproj(x2, g, w_ret, w_ssd, w_lru):
    t, d = x2.shape
    tm = 256
    n_r, n_s, n_l = w_ret.shape[1], w_ssd.shape[1], w_lru.shape[1]
    full = lambda n: pl.BlockSpec((d, n), lambda i: (0, 0))
    return pl.pallas_call(
        _inproj_kernel,
        grid=(t // tm,),
        in_specs=[pl.BlockSpec((tm, d), lambda i: (i, 0)),
                  pl.BlockSpec((1, d), lambda i: (0, 0)),
                  full(n_r), full(n_s), full(n_l)],
        out_specs=[pl.BlockSpec((tm, n_r), lambda i: (i, 0)),
                   pl.BlockSpec((tm, n_s), lambda i: (i, 0)),
                   pl.BlockSpec((tm, n_l), lambda i: (i, 0))],
        out_shape=[jax.ShapeDtypeStruct((t, n_r), F32),
                   jax.ShapeDtypeStruct((t, n_s), F32),
                   jax.ShapeDtypeStruct((t, n_l), F32)],
        compiler_params=_cparams(("parallel",)),
        name="inproj",
    )(x2, g.reshape(1, d), w_ret, w_ssd, w_lru)


def _ret_kernel(q_ref, k_ref, v_ref, g_ref, cos_ref, sin_ref, d2_ref, dec_ref, gc_ref,
                o_ref, qs_ref, ks_ref, acc_ref):
    s = q_ref.shape[0]
    n = s // CHUNK
    lane = lax.broadcasted_iota(jnp.int32, (1, LANE), 1)
    mq0 = ((lane // 32) % 2 == 0).astype(F32)
    mq1 = 1.0 - mq0
    mv0 = (lane < HEAD_DIM).astype(F32)
    mv1 = 1.0 - mv0
    krow = lax.broadcasted_iota(jnp.int32, (LANE, LANE), 0)
    vcol = lax.broadcasted_iota(jnp.int32, (LANE, LANE), 1)
    bd = (((krow // 32) % 2) == (vcol // HEAD_DIM)).astype(F32)
    zeta_f = dec_ref[0]
    zeta_b = dec_ref[1]
    xi_f = dec_ref[2]
    xi_b = dec_ref[3]
    gc = gc_ref[...]
    d2 = d2_ref[...]

    def rot(x, c):
        rows = pl.ds(c * CHUNK, CHUNK)
        return x * cos_ref[rows, :] + pltpu.roll(x, 64, 1) * sin_ref[rows, :]

    def fwd(c, st):
        rows = pl.ds(pl.multiple_of(c * CHUNK, CHUNK), CHUNK)
        qc = rot(q_ref[rows, :], c)
        kc = rot(k_ref[rows, :], c) * (HEAD_DIM ** -0.5)
        vc = v_ref[rows, :]
        qs_ref[rows, :] = qc
        ks_ref[rows, :] = kc
        q2 = jnp.concatenate([qc * mq0, qc * mq1], axis=0)
        sc = lax.dot_general(q2, kc, (((1,), (1,)), ((), ())), preferred_element_type=F32) * d2
        o2 = jnp.dot(sc, vc, preferred_element_type=F32)
        o = o2[:CHUNK] * mv0 + o2[CHUNK:] * mv1
        o = o + jnp.dot(qc, st, preferred_element_type=F32) * xi_f
        acc_ref[rows, :] = o
        upd = jnp.dot(kc.T, vc * zeta_f, preferred_element_type=F32) * bd
        return st * gc + upd

    lax.fori_loop(0, n, fwd, jnp.zeros((LANE, LANE), F32))

    def bwd(i, st):
        c = n - 1 - i
        rows = pl.ds(pl.multiple_of(c * CHUNK, CHUNK), CHUNK)
        qc = qs_ref[rows, :]
        kc = ks_ref[rows, :]
        vc = v_ref[rows, :]
        o = acc_ref[rows, :] + jnp.dot(qc, st, preferred_element_type=F32) * xi_b
        oo = o * o
        ss0 = jnp.sum(oo * mv0, axis=1, keepdims=True)
        ss1 = jnp.sum(oo * mv1, axis=1, keepdims=True)
        scale = lax.rsqrt(ss0 * (1.0 / HEAD_DIM) + EPS) * mv0 + lax.rsqrt(ss1 * (1.0 / HEAD_DIM) + EPS) * mv1
        gate = g_ref[rows, :]
        gate = gate * jax.nn.sigmoid(gate)
        o_ref[rows, :] = (gate * (o * scale)).astype(o_ref.dtype)
        upd = jnp.dot(kc.T, vc * zeta_b, preferred_element_type=F32) * bd
        return st * gc + upd

    lax.fori_loop(0, n, bwd, jnp.zeros((LANE, LANE), F32))


def _ret_constants(s):
    half = HEAD_DIM // 2
    inv = ROPE_BASE ** (-jnp.arange(half, dtype=F32) / half)
    ang = jnp.arange(s, dtype=F32)[:, None] * inv[None, :]
    cos = jnp.cos(ang)
    sin = jnp.sin(ang)
    cos2 = jnp.concatenate([cos, cos, cos, cos], axis=1)
    sin2 = jnp.concatenate([-sin, -sin, sin, sin], axis=1)
    log_gamma = jnp.log1p(-jnp.exp2(-5.0 - jnp.arange(RET_HEADS, dtype=F32)))
    lg_pair = log_gamma.reshape(RET_HEADS // 2, 2)
    idx = jnp.arange(CHUNK, dtype=F32)
    adiff = jnp.abs(idx[:, None] - idx[None, :])
    d2 = jnp.exp(adiff[None, None] * lg_pair[:, :, None, None])
    d2 = d2.reshape(RET_HEADS // 2, 2 * CHUNK, CHUNK)
    lg_lane = jnp.repeat(lg_pair, HEAD_DIM, axis=1)
    pos = idx[None, :, None]
    lgl = lg_lane[:, None, :]
    dec = jnp.stack([jnp.exp((CHUNK - 1 - pos) * lgl), jnp.exp(pos * lgl),
                     jnp.exp((pos + 1) * lgl), jnp.exp((CHUNK - pos) * lgl)], axis=1)
    gc = jnp.exp(CHUNK * lg_lane)[:, None, :]
    return cos2, sin2, d2, dec, gc


def _retention(p_ret, b, s):
    p3 = p_ret.reshape(b, s, 4 * RET_WIDTH)
    cos2, sin2, d2, dec, gc = _ret_constants(s)
    npair = RET_HEADS // 2
    col = lambda off: pl.BlockSpec((None, s, LANE), lambda i, p: (i, 0, off + p))
    return pl.pallas_call(
        _ret_kernel,
        grid=(b, npair),
        in_specs=[col(0), col(npair), col(2 * npair), col(3 * npair),
                  pl.BlockSpec((s, LANE), lambda i, p: (0, 0)),
                  pl.BlockSpec((s, LANE), lambda i, p: (0, 0)),
                  pl.BlockSpec((None, 2 * CHUNK, CHUNK), lambda i, p: (p, 0, 0)),
                  pl.BlockSpec((None, 4, CHUNK, LANE), lambda i, p: (p, 0, 0, 0)),
                  pl.BlockSpec((None, 1, LANE), lambda i, p: (p, 0, 0))],
        out_specs=pl.BlockSpec((None, s, LANE), lambda i, p: (i, 0, p)),
        out_shape=jax.ShapeDtypeStruct((b, s, RET_WIDTH), BF16),
        scratch_shapes=[pltpu.VMEM((s, LANE), F32), pltpu.VMEM((s, LANE), F32), pltpu.VMEM((s, LANE), F32)],
        compiler_params=_cparams(("parallel", "parallel")),
        name="retention",
    )(p3, p3, p3, p3, cos2, sin2, d2, dec, gc)


def _ret_perm():
    pair = np.concatenate([np.arange(0, 32), np.arange(64, 96), np.arange(32, 64), np.arange(96, 128)])
    qperm = np.concatenate([128 * p + pair for p in range(RET_HEADS // 2)])
    return np.concatenate([_OFF_Q + qperm, _OFF_K + qperm, _OFF_V + np.arange(RET_WIDTH), _OFF_G + np.arange(RET_WIDTH)])


_PAD = 8


def _softplus(x):
    return jnp.maximum(x, 0.0) + jnp.log1p(jnp.exp(-jnp.abs(x)))


def _silu(x):
    return x * jax.nn.sigmoid(x)


def _expand_heads(cols, base, lane_head):
    acc = jnp.zeros((cols.shape[0], SSD_GW), F32)
    for r in range(SSD_HPG):
        acc = jnp.where(lane_head == r, cols[:, base + r:base + r + 1], acc)
    return acc


def _dwconv_chunk(xp_ref, cw_ref, cb_ref, c):
    rows = CHUNK + 2 * _PAD
    win = xp_ref[pl.ds(pl.multiple_of(c * CHUNK, CHUNK), rows), :]
    acc = cb_ref[...]
    for t in range(CONV_WIDTH):
        shift = (CONV_WIDTH // 2 - t) % rows
        tap = win if shift == 0 else pltpu.roll(win, shift, 0)
        acc = acc + tap[_PAD:_PAD + CHUNK, :] * cw_ref[t:t + 1, :]
    return acc


def _ssd_kernel(xbc_ref, z_ref, dt_ref, cw_ref, cb_ref, dtb_ref, alog_ref, dsk_ref, nw_ref, o_ref,
                xp_ref, xc_ref, y_ref, pc_ref, dts_ref):
    s = z_ref.shape[0]
    n = s // CHUNK
    w = xbc_ref.shape[1]
    xp_ref[0:_PAD, :] = jnp.zeros((_PAD, w), F32)
    xp_ref[_PAD + s:2 * _PAD + s, :] = jnp.zeros((_PAD, w), F32)

    def copy(c, carry):
        rows = pl.ds(pl.multiple_of(c * CHUNK, CHUNK), CHUNK)
        xp_ref[pl.ds(pl.multiple_of(c * CHUNK + _PAD, _PAD), CHUNK), :] = xbc_ref[rows, :]
        return carry

    lax.fori_loop(0, n, copy, 0)

    lane_head = lax.broadcasted_iota(jnp.int32, (1, SSD_GW), 1) // HEAD_DIM
    ii = lax.broadcasted_iota(jnp.int32, (CHUNK, CHUNK), 0)
    jj = lax.broadcasted_iota(jnp.int32, (CHUNK, CHUNK), 1)
    lower = ii >= jj
    tri = lower.astype(F32)
    a_neg = -jnp.exp(alog_ref[...])
    dt_bias = dtb_ref[...]

    def fwd(c, hf):
        rows = pl.ds(pl.multiple_of(c * CHUNK, CHUNK), CHUNK)
        xc = _silu(_dwconv_chunk(xp_ref, cw_ref, cb_ref, c))
        xc_ref[rows, :] = xc
        xs = xc[:, 0:SSD_GW]
        bm = xc[:, SSD_GW:SSD_GW + SSD_STATE]
        cm = xc[:, SSD_GW + SSD_STATE:]
        dt = _softplus(dt_ref[rows, :] + dt_bias)
        dts_ref[rows, :] = dt
        la = dt * a_neg
        p = jnp.dot(tri, la, precision=lax.Precision.HIGHEST, preferred_element_type=F32)
        pc_ref[rows, :] = p
        e = p - la
        pt = p.T
        et = e.T
        dtt = dt.T
        g = lax.dot_general(cm, bm, (((1,), (1,)), ((), ())), preferred_element_type=F32)
        y = jnp.zeros((CHUNK, SSD_GW), F32)
        for r in range(SSD_HPG):
            arg = jnp.where(lower, p[:, r:r + 1] - pt[r:r + 1, :], et[3 + r:4 + r, :] - e[:, 3 + r:4 + r])
            dtj = jnp.where(lower, dtt[r:r + 1, :], dtt[3 + r:4 + r, :])
            m = g * (jnp.exp(arg) * dtj)
            y = y + jnp.dot(m, jnp.where(lane_head == r, xs, 0.0), preferred_element_type=F32)
        pf = _expand_heads(p, 0, lane_head)
        pf_last = _expand_heads(p[CHUNK - 1:CHUNK, :], 0, lane_head)
        dtf = _expand_heads(dt, 0, lane_head)
        y = y + jnp.dot(cm, hf, preferred_element_type=F32) * jnp.exp(pf)
        y_ref[rows, :] = y
        upd = jnp.dot(bm.T, xs * (jnp.exp(pf_last - pf) * dtf), preferred_element_type=F32)
        return hf * jnp.exp(pf_last) + upd

    lax.fori_loop(0, n, fwd, jnp.zeros((SSD_STATE, SSD_GW), F32))

    dsk = dsk_ref[...]
    nw = nw_ref[...]

    def bwd(i, hb):
        c = n - 1 - i
        rows = pl.ds(pl.multiple_of(c * CHUNK, CHUNK), CHUNK)
        xs = xc_ref[rows, 0:SSD_GW]
        bm = xc_ref[rows, SSD_GW:SSD_GW + SSD_STATE]
        cm = xc_ref[rows, SSD_GW + SSD_STATE:]
        dt = dts_ref[rows, :]
        p = pc_ref[rows, :]
        e = p - dt * a_neg
        eb = _expand_heads(e, 3, lane_head)
        tb = _expand_heads(p[CHUNK - 1:CHUNK, :], 3, lane_head)
        dtb = _expand_heads(dt, 3, lane_head)
        y = y_ref[rows, :] + jnp.dot(cm, hb, preferred_element_type=F32) * jnp.exp(tb - eb)
        y = (y + dsk * xs) * _silu(z_ref[rows, :])
        ms = jnp.sum(y * y, axis=1, keepdims=True) * (1.0 / (SSD_HPG * HEAD_DIM))
        o_ref[rows, :] = (y * lax.rsqrt(ms + EPS) * nw).astype(o_ref.dtype)
        upd = jnp.dot(bm.T, xs * (jnp.exp(eb) * dtb), preferred_element_type=F32)
        return hb * jnp.exp(tb) + upd

    lax.fori_loop(0, n, bwd, jnp.zeros((SSD_STATE, SSD_GW), F32))


_SSD_XBC_W = SSD_GW + 2 * SSD_STATE
_SSD_SLAB = SSD_GROUPS * (_SSD_XBC_W + SSD_GW + LANE)


def _ssd_cols():
    gw = SSD_HPG * HEAD_DIM
    pad = lambda k: -np.ones(k, np.int64)
    xbc, zz, dtc, conv = [], [], [], []
    for g in range(SSD_GROUPS):
        xcols = np.arange(gw) + g * gw
        bcols = SSD_WIDTH + g * SSD_STATE + np.arange(SSD_STATE)
        ccols = SSD_WIDTH + SSD_GROUPS * SSD_STATE + g * SSD_STATE + np.arange(SSD_STATE)
        c_idx = np.concatenate([xcols, pad(SSD_GW - gw), bcols, ccols])
        conv.append(c_idx)
        xbc.append(np.where(c_idx >= 0, _OFF_XBC + c_idx, -1))
        zz.append(np.concatenate([_OFF_Z + xcols, pad(SSD_GW - gw)]))
        heads = _OFF_DT + g * SSD_HPG + np.arange(SSD_HPG)
        dtc.append(np.concatenate([heads, heads, pad(LANE - 2 * SSD_HPG)]))
    return np.concatenate(xbc + zz + dtc), np.concatenate(conv)


def _take_cols(a, idx):
    a = jnp.concatenate([a, jnp.zeros(a.shape[:-1] + (1,), a.dtype)], axis=-1)
    return a[..., np.where(idx >= 0, idx, a.shape[-1] - 1)]


def _ssd(p_ssd, b, s, conv_w, conv_b, dt_bias, a_log, d_skip, norm_w):
    p3 = p_ssd.reshape(b, s, _SSD_SLAB)
    _, conv_idx = _ssd_cols()
    cw = _take_cols(conv_w, conv_idx)
    cb = _take_cols(conv_b[None, :], conv_idx)
    gw = SSD_HPG * HEAD_DIM
    zpad = jnp.zeros((SSD_GROUPS, LANE - 2 * SSD_HPG), F32)
    grp = lambda v: jnp.concatenate([v[0].reshape(SSD_GROUPS, SSD_HPG), v[1].reshape(SSD_GROUPS, SSD_HPG), zpad], axis=1)
    dtb = grp(dt_bias)[:, None, :]
    alog = grp(a_log)[:, None, :]
    lanes = lambda v: jnp.pad(v.reshape(SSD_GROUPS, gw), ((0, 0), (0, SSD_GW - gw)))[:, None, :]
    dsk = lanes(jnp.repeat(d_skip, HEAD_DIM))
    nw = lanes(norm_w)
    nxb = _SSD_XBC_W // LANE
    grid_spec = dict(
        grid=(b, SSD_GROUPS),
        in_specs=[pl.BlockSpec((None, s, _SSD_XBC_W), lambda i, g: (i, 0, g)),
                  pl.BlockSpec((None, s, SSD_GW), lambda i, g: (i, 0, SSD_GROUPS * _SSD_XBC_W // SSD_GW + g)),
                  pl.BlockSpec((None, s, LANE), lambda i, g: (i, 0, SSD_GROUPS * (nxb + SSD_GW // LANE) + g)),
                  pl.BlockSpec((CONV_WIDTH, _SSD_XBC_W), lambda i, g: (0, g)),
                  pl.BlockSpec((1, _SSD_XBC_W), lambda i, g: (0, g)),
                  pl.BlockSpec((None, 1, LANE), lambda i, g: (g, 0, 0)),
                  pl.BlockSpec((None, 1, LANE), lambda i, g: (g, 0, 0)),
                  pl.BlockSpec((None, 1, SSD_GW), lambda i, g: (g, 0, 0)),
                  pl.BlockSpec((None, 1, SSD_GW), lambda i, g: (g, 0, 0))],
        out_specs=pl.BlockSpec((None, s, SSD_GW), lambda i, g: (i, 0, g)),
    )
    return pl.pallas_call(
        _ssd_kernel,
        out_shape=jax.ShapeDtypeStruct((b, s, SSD_GROUPS * SSD_GW), BF16),
        scratch_shapes=[pltpu.VMEM((s + 2 * _PAD, _SSD_XBC_W), F32), pltpu.VMEM((s, _SSD_XBC_W), F32),
                        pltpu.VMEM((s, SSD_GW), F32), pltpu.VMEM((s, LANE), F32), pltpu.VMEM((s, LANE), F32)],
        compiler_params=_cparams(("parallel", "parallel")),
        name="ssd",
        **grid_spec,
    )(p3, p3, p3, cw, cb, dtb, alog, dsk, nw)


def _scan_chunk(a, u, reverse):
    row = lax.broadcasted_iota(jnp.int32, (CHUNK, 1), 0)
    d = 1
    while d < CHUNK:
        if reverse:
            keep = row < CHUNK - d
            shift = CHUNK - d
        else:
            keep = row >= d
            shift = d
        a_sh = jnp.where(keep, pltpu.roll(a, shift, 0), 1.0)
        u_sh = jnp.where(keep, pltpu.roll(u, shift, 0), 0.0)
        u = a * u_sh + u
        a = a * a_sh
        d *= 2
    return a, u


def _lru_kernel(x_ref, cw_ref, cb_ref, wg_ref, bg_ref, lam_ref, o_ref, xp_ref, hf_ref, ab_ref, ub_ref):
    s = x_ref.shape[0]
    n = s // CHUNK
    w = LRU_WIDTH
    xp_ref[0:_PAD, :] = jnp.zeros((_PAD, w), F32)
    xp_ref[_PAD + s:2 * _PAD + s, :] = jnp.zeros((_PAD, w), F32)

    def copy(c, carry):
        rows = pl.ds(pl.multiple_of(c * CHUNK, CHUNK), CHUNK)
        xp_ref[pl.ds(pl.multiple_of(c * CHUNK + _PAD, _PAD), CHUNK), :] = x_ref[rows, 0:w]
        return carry

    lax.fori_loop(0, n, copy, 0)
    nsp = -LRU_C * _softplus(-lam_ref[...])

    def gates(pre, xc, k):
        r = jax.nn.sigmoid(pre[:, 2 * k * w:(2 * k + 1) * w])
        i = jax.nn.sigmoid(pre[:, (2 * k + 1) * w:(2 * k + 2) * w])
        log_a = r * nsp[k:k + 1, :]
        a = jnp.exp(log_a)
        return a, jnp.sqrt(-jnp.tanh(log_a) * (a * a + 1.0)) * (i * xc)

    def fwd(c, h):
        rows = pl.ds(pl.multiple_of(c * CHUNK, CHUNK), CHUNK)
        xc = _dwconv_chunk(xp_ref, cw_ref, cb_ref, c)
        pre = jnp.dot(xc.astype(BF16), wg_ref[...], preferred_element_type=F32) + bg_ref[...]
        a_b, u_b = gates(pre, xc, 1)
        ab_ref[rows, :] = a_b
        ub_ref[rows, :] = u_b
        a_f, u_f = gates(pre, xc, 0)
        a_s, u_s = _scan_chunk(a_f, u_f, False)
        hc = u_s + a_s * h
        hf_ref[rows, :] = hc
        return hc[CHUNK - 1:CHUNK, :]

    lax.fori_loop(0, n, fwd, jnp.zeros((1, w), F32))

    def bwd(i, h):
        c = n - 1 - i
        rows = pl.ds(pl.multiple_of(c * CHUNK, CHUNK), CHUNK)
        a_s, u_s = _scan_chunk(ab_ref[rows, :], ub_ref[rows, :], True)
        hc = u_s + a_s * h
        gate = x_ref[rows, w:2 * w]
        o_ref[rows, :] = ((hf_ref[rows, :] + hc) * jax.nn.gelu(gate)).astype(o_ref.dtype)
        return hc[0:1, :]

    lax.fori_loop(0, n, bwd, jnp.zeros((1, w), F32))


def _block_diag(wblk):
    eye = jnp.eye(LRU_BLOCKS, dtype=wblk.dtype)
    return jnp.einsum('hij,hk->hikj', wblk, eye).reshape(LRU_WIDTH, LRU_WIDTH)


def _lru(p_lru, b, s, conv_w, conv_b, lam, wa, ba, wx, bx):
    p3 = p_lru.reshape(b, s, 2 * LRU_WIDTH)
    wg = jnp.concatenate([_block_diag(wa[0]), _block_diag(wx[0]), _block_diag(wa[1]), _block_diag(wx[1])], axis=1).astype(BF16)
    bg = jnp.concatenate([ba[0], bx[0], ba[1], bx[1]])[None, :]
    w = LRU_WIDTH
    const = lambda shape: pl.BlockSpec(shape, lambda i: (0,) * len(shape))
    return pl.pallas_call(
        _lru_kernel,
        grid=(b,),
        in_specs=[pl.BlockSpec((None, s, 2 * w), lambda i: (i, 0, 0)),
                  const((CONV_WIDTH, w)), const((1, w)), const((w, 4 * w)), const((1, 4 * w)), const((2, w))],
        out_specs=pl.BlockSpec((None, s, w), lambda i: (i, 0, 0)),
        out_shape=jax.ShapeDtypeStruct((b, s, w), BF16),
        scratch_shapes=[pltpu.VMEM((s + 2 * _PAD, w), F32), pltpu.VMEM((s, w), F32),
                        pltpu.VMEM((s, w), F32), pltpu.VMEM((s, w), F32)],
        compiler_params=_cparams(("parallel",)),
        name="rglru",
    )(p3, conv_w, conv_b[None, :], wg, bg, lam)


def _outproj_kernel(h_ref, a_ref, b_ref, c_ref, wa_ref, wb_ref, wc_ref, o_ref):
    acc = h_ref[...]
    acc = acc + jnp.dot(a_ref[...], wa_ref[...], preferred_element_type=F32)
    acc = acc + jnp.dot(b_ref[...], wb_ref[...], preferred_element_type=F32)
    acc = acc + jnp.dot(c_ref[...], wc_ref[...], preferred_element_type=F32)
    o_ref[...] = acc


def _outproj(h2, o_ret, o_ssd, o_lru, w_ret, w_ssd, w_lru):
    t, d = h2.shape
    tm = ROW_TILE
    row = lambda n: pl.BlockSpec((tm, n), lambda i: (i, 0))
    full = lambda n: pl.BlockSpec((n, d), lambda i: (0, 0))
    ka, kb, kc = o_ret.shape[1], o_ssd.shape[1], o_lru.shape[1]
    return pl.pallas_call(
        _outproj_kernel,
        grid=(t // tm,),
        in_specs=[row(d), row(ka), row(kb), row(kc), full(ka), full(kb), full(kc)],
        out_specs=row(d),
        out_shape=jax.ShapeDtypeStruct((t, d), F32),
        compiler_params=_cparams(("parallel",)),
        name="outproj",
    )(h2, o_ret, o_ssd, o_lru, w_ret, w_ssd, w_lru)


def _outproj_weights(w_out):
    gw = SSD_HPG * HEAD_DIM
    idx = np.concatenate([np.concatenate([RET_WIDTH + g * gw + np.arange(gw), -np.ones(SSD_GW - gw, np.int64)])
                          for g in range(SSD_GROUPS)])
    w_ssd = _take_cols(w_out.T, idx).T
    return (w_out[:RET_WIDTH].astype(BF16), w_ssd.astype(BF16), w_out[RET_WIDTH + SSD_WIDTH:].astype(BF16))


def _kv_kernel(m_ref, g_ref, w_ref, o_ref):
    mn = _rms(m_ref[...], g_ref[...]).astype(BF16)
    o_ref[...] = jnp.dot(mn, w_ref[...], preferred_element_type=F32).astype(o_ref.dtype)


def _kv_proj(mem2, g, wkv):
    t, d = mem2.shape
    n = wkv.shape[1]
    tm = 256
    return pl.pallas_call(
        _kv_kernel,
        grid=(t // tm,),
        in_specs=[pl.BlockSpec((tm, d), lambda i: (i, 0)), pl.BlockSpec((1, d), lambda i: (0, 0)),
                  pl.BlockSpec((d, n), lambda i: (0, 0))],
        out_specs=pl.BlockSpec((tm, n), lambda i: (i, 0)),
        out_shape=jax.ShapeDtypeStruct((t, n), BF16),
        compiler_params=_cparams(("parallel",)),
        name="kvproj",
    )(mem2, g.reshape(1, d), wkv)


def _xattn_kernel(h_ref, g_ref, wq_ref, k_ref, v_ref, wo_ref, o_ref):
    h = h_ref[...]
    xn = _rms(h, g_ref[...]).astype(BF16)
    q = jnp.dot(xn, wq_ref[...], preferred_element_type=F32).astype(BF16)
    outs = []
    for hd in range(CA_HEADS):
        cols = slice(hd * CA_HEAD_DIM, (hd + 1) * CA_HEAD_DIM)
        sc = lax.dot_general(q[:, cols], k_ref[:, cols], (((1,), (1,)), ((), ())), preferred_element_type=F32)
        sc = sc * (CA_HEAD_DIM ** -0.5)
        e = jnp.exp(sc - jnp.max(sc, axis=-1, keepdims=True))
        p = e / jnp.sum(e, axis=-1, keepdims=True)
        outs.append(jnp.dot(p.astype(BF16), v_ref[:, cols], preferred_element_type=F32).astype(BF16))
    o = jnp.concatenate(outs, axis=-1)
    o_ref[...] = h + jnp.dot(o, wo_ref[...], preferred_element_type=F32)


def _xattn(h2, b, s, g, wq, kv, wo):
    t, d = h2.shape
    m = kv.shape[0] // b
    ts = ROW_TILE
    nts = s // ts
    return pl.pallas_call(
        _xattn_kernel,
        grid=(b, nts),
        in_specs=[pl.BlockSpec((ts, d), lambda i, j: (i * nts + j, 0)),
                  pl.BlockSpec((1, d), lambda i, j: (0, 0)),
                  pl.BlockSpec((d, d), lambda i, j: (0, 0)),
                  pl.BlockSpec((m, d), lambda i, j: (i, 0)),
                  pl.BlockSpec((m, d), lambda i, j: (i, 1)),
                  pl.BlockSpec((d, d), lambda i, j: (0, 0))],
        out_specs=pl.BlockSpec((ts, d), lambda i, j: (i * nts + j, 0)),
        out_shape=jax.ShapeDtypeStruct((t, d), F32),
        compiler_params=_cparams(("parallel", "parallel")),
        name="xattn",
    )(h2, g.reshape(1, d), wq, kv, kv, wo)


_ROUTER_ROWS = SUBLANE + N_EXPERTS


def _first_argmax(v, row):
    m = jnp.max(v, axis=0, keepdims=True)
    return m, jnp.min(jnp.where(v == m, row, SUBLANE), axis=0, keepdims=True)


def _router_kernel(h_ref, g_ref, w_ref, b_ref, ids_ref, gates_ref):
    xn = _rms(h_ref[...], g_ref[...])
    lg = lax.dot_general(w_ref[...], xn, (((1,), (1,)), ((), ())), precision=lax.Precision.HIGHEST,
                         preferred_element_type=F32) + b_ref[...]
    tm = lg.shape[1]
    row = lax.broadcasted_iota(jnp.int32, (SUBLANE, tm), 0)
    gl = jnp.where(row < N_GROUPS, lg[0:SUBLANE], -jnp.inf)
    ge = jnp.exp(gl - jnp.max(gl, axis=0, keepdims=True))
    gp_all = ge / jnp.sum(ge, axis=0, keepdims=True)
    gp, gi = _first_argmax(gp_all, row)
    el = lg[SUBLANE:2 * SUBLANE]
    for g in range(1, N_GROUPS):
        el = jnp.where(gi == g, lg[SUBLANE * (g + 1):SUBLANE * (g + 2)], el)
    ee = jnp.exp(el - jnp.max(el, axis=0, keepdims=True))
    ep = ee / jnp.sum(ee, axis=0, keepdims=True)
    v1, i1 = _first_argmax(ep, row)
    v2, i2 = _first_argmax(jnp.where(row == i1, -1.0, ep), row)
    den = v1 + v2
    ids = jnp.where(row == 0, gi * EXPERTS_PER_GROUP + i1, jnp.where(row == 1, gi * EXPERTS_PER_GROUP + i2, 0))
    gates = jnp.where(row == 0, gp * v1 / den, jnp.where(row == 1, gp * v2 / den, 0.0))
    ids_ref[...] = ids
    gates_ref[...] = gates


def _router(h2, g, wg, bg, we, be):
    t, d = h2.shape
    tm = ROW_TILE
    zrow = jnp.zeros((SUBLANE - N_GROUPS, d), F32)
    w = jnp.concatenate([wg.T, zrow, we.T], axis=0)
    bias = jnp.concatenate([bg, jnp.zeros((SUBLANE - N_GROUPS,), F32), be])[:, None]
    return pl.pallas_call(
        _router_kernel,
        grid=(t // tm,),
        in_specs=[pl.BlockSpec((tm, d), lambda i: (i, 0)), pl.BlockSpec((1, d), lambda i: (0, 0)),
                  pl.BlockSpec((_ROUTER_ROWS, d), lambda i: (0, 0)), pl.BlockSpec((_ROUTER_ROWS, 1), lambda i: (0, 0))],
        out_specs=[pl.BlockSpec((SUBLANE, tm), lambda i: (0, i)), pl.BlockSpec((SUBLANE, tm), lambda i: (0, i))],
        out_shape=[jax.ShapeDtypeStruct((SUBLANE, t), jnp.int32), jax.ShapeDtypeStruct((SUBLANE, t), F32)],
        compiler_params=_cparams(("parallel",)),
        name="router",
    )(h2, g.reshape(1, d), w, bias)


MOE_BLK = 256


def _dispatch_plan(ids, gates, t):
    n_assign = t * TOP_K
    e = ids[:TOP_K].T.reshape(n_assign)
    wts = gates[:TOP_K].T.reshape(n_assign)
    tok = jnp.repeat(jnp.arange(t, dtype=jnp.int32), TOP_K)
    order = jnp.argsort(e)
    e_s = e[order]
    counts = jnp.zeros((N_EXPERTS,), jnp.int32).at[e].add(1)
    padded = (counts + MOE_BLK - 1) // MOE_BLK * MOE_BLK
    start = jnp.cumsum(counts) - counts
    pad_end = jnp.cumsum(padded)
    pad_start = pad_end - padded
    dest = pad_start[e_s] + jnp.arange(n_assign, dtype=jnp.int32) - start[e_s]
    n_blocks = n_assign // MOE_BLK + N_EXPERTS
    slots = n_blocks * MOE_BLK
    buf_tok = jnp.zeros((slots,), jnp.int32).at[dest].set(tok[order])
    buf_w = jnp.zeros((slots,), F32).at[dest].set(wts[order])
    block_start = jnp.arange(n_blocks, dtype=jnp.int32) * MOE_BLK
    block_e = jnp.minimum(jnp.searchsorted(pad_end, block_start, side='right'), N_EXPERTS - 1).astype(jnp.int32)
    n_used = (pad_end[-1] // MOE_BLK).astype(jnp.int32).reshape(1)
    pos = jnp.zeros((n_assign,), jnp.int32).at[order].set(dest).reshape(t, TOP_K)
    return buf_tok.reshape(n_blocks, 1, MOE_BLK), buf_w.reshape(slots, 1), block_e, n_used, pos


def _gather_rows(src_hbm, idx_ref, dst_ref, sem, n):
    def body(r, carry):
        pltpu.make_async_copy(src_hbm.at[pl.ds(idx_ref[r], 1), :], dst_ref.at[pl.ds(r, 1), :], sem).start()
        return carry
    lax.fori_loop(0, n, body, 0, unroll=8)


def _wait_rows(src_hbm, dst_ref, sem):
    pltpu.make_async_copy(src_hbm.at[pl.ds(0, dst_ref.shape[0]), :], dst_ref, sem).wait()


def _expert_kernel(be_ref, nu_ref, tok_ref, nxt_ref, h_hbm, g_ref, wt_ref, w1_ref, w3_ref, w2_ref, o_ref, xbuf, sem):
    i = pl.program_id(0)
    n_used = nu_ref[0]
    slot = i % 2

    @pl.when(i == 0)
    def _():
        _gather_rows(h_hbm, tok_ref.at[0], xbuf.at[0], sem.at[0], MOE_BLK)

    @pl.when(i + 1 < n_used)
    def _():
        _gather_rows(h_hbm, nxt_ref.at[0], xbuf.at[1 - slot], sem.at[1 - slot], MOE_BLK)

    @pl.when(i < n_used)
    def _():
        _wait_rows(h_hbm, xbuf.at[slot], sem.at[slot])
        xn = _rms(xbuf[slot], g_ref[...]).astype(BF16)
        h1 = jnp.dot(xn, w1_ref[...], preferred_element_type=F32)
        h3 = jnp.dot(xn, w3_ref[...], preferred_element_type=F32)
        hid = (_silu(h1) * h3).astype(BF16)
        y = jnp.dot(hid, w2_ref[...], preferred_element_type=F32)
        o_ref[...] = y * wt_ref[...]

    @pl.when(i >= n_used)
    def _():
        o_ref[...] = jnp.zeros_like(o_ref)


def _experts(h2, g, buf_tok, buf_w, block_e, n_used, w1, w3, w2):
    t, d = h2.shape
    n_blocks = buf_tok.shape[0]
    de = w1.shape[2]
    nxt = lambda i, be, nu: (jnp.minimum(i + 1, n_blocks - 1), 0, 0)
    grid_spec = pltpu.PrefetchScalarGridSpec(
        num_scalar_prefetch=2,
        grid=(n_blocks,),
        in_specs=[pl.BlockSpec((None, 1, MOE_BLK), lambda i, be, nu: (i, 0, 0), memory_space=pltpu.SMEM),
                  pl.BlockSpec((None, 1, MOE_BLK), nxt, memory_space=pltpu.SMEM),
                  pl.BlockSpec(memory_space=pl.ANY),
                  pl.BlockSpec((1, d), lambda i, be, nu: (0, 0)),
                  pl.BlockSpec((MOE_BLK, 1), lambda i, be, nu: (i, 0)),
                  pl.BlockSpec((None, d, de), lambda i, be, nu: (be[i], 0, 0)),
                  pl.BlockSpec((None, d, de), lambda i, be, nu: (be[i], 0, 0)),
                  pl.BlockSpec((None, de, d), lambda i, be, nu: (be[i], 0, 0))],
        out_specs=pl.BlockSpec((MOE_BLK, d), lambda i, be, nu: (i, 0)),
        scratch_shapes=[pltpu.VMEM((2, MOE_BLK, d), F32), pltpu.SemaphoreType.DMA((2,))],
    )
    return pl.pallas_call(
        _expert_kernel,
        grid_spec=grid_spec,
        out_shape=jax.ShapeDtypeStruct((n_blocks * MOE_BLK, d), F32),
        compiler_params=_cparams(("arbitrary",)),
        name="experts",
    )(block_e, n_used, buf_tok, buf_tok, h2, g.reshape(1, d), buf_w, w1, w3, w2)


_COMB_TILE = 256


def _combine_kernel(pos_ref, nxt_ref, h_ref, g_ref, y_hbm, o_ref, ybuf, sem, *, final):
    i = pl.program_id(0)
    n = pl.num_programs(0)
    slot = i % 2
    rows = TOP_K * _COMB_TILE

    @pl.when(i == 0)
    def _():
        _gather_rows(y_hbm, pos_ref.at[0], ybuf.at[0], sem.at[0], rows)

    @pl.when(i + 1 < n)
    def _():
        _gather_rows(y_hbm, nxt_ref.at[0], ybuf.at[1 - slot], sem.at[1 - slot], rows)

    _wait_rows(y_hbm, ybuf.at[slot], sem.at[slot])
    out = h_ref[...] + ybuf[slot, 0:_COMB_TILE, :] + ybuf[slot, _COMB_TILE:rows, :]
    if final:
        out = _rms(out, g_ref[...])
    o_ref[...] = out


def _combine(h2, y_sorted, pos, g_final, final):
    t, d = h2.shape
    tm = _COMB_TILE
    nt = t // tm
    pos_t = pos.reshape(nt, tm, TOP_K).transpose(0, 2, 1).reshape(nt, 1, TOP_K * tm)
    nxt = lambda i: (jnp.minimum(i + 1, nt - 1), 0, 0)
    return pl.pallas_call(
        functools.partial(_combine_kernel, final=final),
        grid=(nt,),
        in_specs=[pl.BlockSpec((None, 1, TOP_K * tm), lambda i: (i, 0, 0), memory_space=pltpu.SMEM),
                  pl.BlockSpec((None, 1, TOP_K * tm), nxt, memory_space=pltpu.SMEM),
                  pl.BlockSpec((tm, d), lambda i: (i, 0)),
                  pl.BlockSpec((1, d), lambda i: (0, 0)),
                  pl.BlockSpec(memory_space=pl.ANY)],
        out_specs=pl.BlockSpec((tm, d), lambda i: (i, 0)),
        out_shape=jax.ShapeDtypeStruct((t, d), F32),
        scratch_shapes=[pltpu.VMEM((2, TOP_K * tm, d), F32), pltpu.SemaphoreType.DMA((2,))],
        compiler_params=_cparams(("arbitrary",)),
        name="combine",
    )(pos_t, pos_t, h2, g_final.reshape(1, d), y_sorted)


def _moe(h2, g, wg, bg, we, be, w1, w3, w2, g_final, final):
    t = h2.shape[0]
    ids, gates = _router(h2, g, wg, bg, we, be)
    buf_tok, buf_w, block_e, n_used, pos = _dispatch_plan(ids, gates, t)
    y_sorted = _experts(h2, g, buf_tok, buf_w, block_e, n_used, w1, w3, w2)
    return _combine(h2, y_sorted, pos, g_final, final)


def kernel(x, mem, g_mix, w_in, w_out, ssd_conv_w, ssd_conv_b, ssd_dt_bias, ssd_a_log, ssd_d, ssd_norm, lru_conv_w, lru_conv_b, lru_lambda, lru_wa, lru_ba, lru_wx, lru_bx, g_ca, g_mem, ca_wq, ca_wkv, ca_wo, g_moe, moe_wg, moe_bg, moe_we, moe_be, moe_w1, moe_w3, moe_w2, g_final):
    b, s, d = x.shape
    depth = g_mix.shape[0]
    h = x.reshape(b * s, d)
    mem2 = mem.reshape(b * mem.shape[1], d)
    ret_perm = _ret_perm()
    ssd_cols = _ssd_cols()[0]
    for l in range(depth):
        w_ret = w_in[l][:, ret_perm].astype(BF16)
        w_ssd = _take_cols(w_in[l], ssd_cols).astype(BF16)
        w_lru = w_in[l][:, _OFF_XR:].astype(BF16)
        p_ret, p_ssd, p_lru = _inproj(h, g_mix[l], w_ret, w_ssd, w_lru)
        o_ret = _retention(p_ret, b, s)
        o_ssd = _ssd(p_ssd, b, s, ssd_conv_w[l], ssd_conv_b[l], ssd_dt_bias[l], ssd_a_log[l], ssd_d[l], ssd_norm[l])
        o_lru = _lru(p_lru, b, s, lru_conv_w[l], lru_conv_b[l], lru_lambda[l], lru_wa[l], lru_ba[l], lru_wx[l], lru_bx[l])
        h = _outproj(h, o_ret.reshape(b * s, -1), o_ssd.reshape(b * s, -1), o_lru.reshape(b * s, -1),
                     *_outproj_weights(w_out[l]))
        kv = _kv_proj(mem2, g_mem[l], ca_wkv[l].astype(BF16))
        h = _xattn(h, b, s, g_ca[l], ca_wq[l].astype(BF16), kv, ca_wo[l].astype(BF16))
        h = _moe(h, g_moe[l], moe_wg[l], moe_bg[l], moe_we[l], moe_be[l],
                 moe_w1[l].astype(BF16), moe_w3[l].astype(BF16), moe_w2[l].astype(BF16),
                 g_final, l == depth - 1)
    return h.reshape(b, s, d)
```

```python
import functools
import math

import jax
import jax.numpy as jnp
import numpy as np
from jax import lax
from jax.experimental import pallas as pl
from jax.experimental.pallas import tpu as pltpu

D_MODEL = 1024
HEAD_DIM = 64
RET_WIDTH = 384
SSD_WIDTH = 384
LRU_WIDTH = 256
RET_HEADS = 6
SSD_HEADS = 6
SSD_GROUPS = 2
SSD_HPG = 3
SSD_STATE = 128
SSD_CONV_DIM = SSD_WIDTH + 2 * SSD_GROUPS * SSD_STATE
LRU_BLOCKS = 4
LRU_C = 8.0
CONV_WIDTH = 4
ROPE_BASE = 10000.0
CA_HEADS = 4
CA_HEAD_DIM = 256
N_GROUPS = 4
EXPERTS_PER_GROUP = 8
N_EXPERTS = 32
TOP_K = 2
D_EXPERT = 512
EPS = 1e-6

_OFF_Q, _OFF_K, _OFF_V, _OFF_G = 0, 384, 768, 1152
_OFF_Z = 1536
_OFF_XBC = 1920
_OFF_DT = 2816
_OFF_XR = 2822
_OFF_GATE = 3078
D_IN = 3334

LANE = 128
SUBLANE = 8
VMEM_LIMIT = 56 * 1024 * 1024

CHUNK = 128
ROW_TILE = 512
SSD_GW = 256

F32 = jnp.float32
BF16 = jnp.bfloat16


def _cparams(sem):
    return pltpu.CompilerParams(dimension_semantics=sem, vmem_limit_bytes=VMEM_LIMIT)


def _rms(x, g):
    return x * lax.rsqrt(jnp.mean(x * x, axis=-1, keepdims=True) + EPS) * g


def _inproj_kernel(x_ref, g_ref, wr_ref, ws_ref, wl_ref, pr_ref, ps_ref, pq_ref):
    xn = _rms(x_ref[...], g_ref[...]).astype(BF16)
    pr_ref[...] = jnp.dot(xn, wr_ref[...], preferred_element_type=F32)
    ps_ref[...] = jnp.dot(xn, ws_ref[...], preferred_element_type=F32)
    pq_ref[...] = jnp.dot(xn, wl_ref[...], preferred_element_type=F32)


def _inproj(x2, g, w_ret, w_ssd, w_lru):
    t, d = x2.shape
    tm = 256
    n_r, n_s, n_l = w_ret.shape[1], w_ssd.shape[1], w_lru.shape[1]
    full = lambda n: pl.BlockSpec((d, n), lambda i: (0, 0))
    return pl.pallas_call(
        _inproj_kernel,
        grid=(t // tm,),
        in_specs=[pl.BlockSpec((tm, d), lambda i: (i, 0)),
                  pl.BlockSpec((1, d), lambda i: (0, 0)),
                  full(n_r), full(n_s), full(n_l)],
        out_specs=[pl.BlockSpec((tm, n_r), lambda i: (i, 0)),
                   pl.BlockSpec((tm, n_s), lambda i: (i, 0)),
                   pl.BlockSpec((tm, n_l), lambda i: (i, 0))],
        out_shape=[jax.ShapeDtypeStruct((t, n_r), F32),
                   jax.ShapeDtypeStruct((t, n_s), F32),
                   jax.ShapeDtypeStruct((t, n_l), F32)],
        compiler_params=_cparams(("parallel",)),
        name="inproj",
    )(x2, g.reshape(1, d), w_ret, w_ssd, w_lru)


def _ret_kernel(q_ref, k_ref, v_ref, g_ref, cos_ref, sin_ref, d2_ref, dec_ref, gc_ref,
                o_ref, qs_ref, ks_ref, acc_ref):
    s = q_ref.shape[0]
    n = s // CHUNK
    lane = lax.broadcasted_iota(jnp.int32, (1, LANE), 1)
    mq0 = ((lane // 32) % 2 == 0).astype(F32)
    mq1 = 1.0 - mq0
    mv0 = (lane < HEAD_DIM).astype(F32)
    mv1 = 1.0 - mv0
    krow = lax.broadcasted_iota(jnp.int32, (LANE, LANE), 0)
    vcol = lax.broadcasted_iota(jnp.int32, (LANE, LANE), 1)
    bd = (((krow // 32) % 2) == (vcol // HEAD_DIM)).astype(F32)
    zeta_f = dec_ref[0]
    zeta_b = dec_ref[1]
    xi_f = dec_ref[2]
    xi_b = dec_ref[3]
    gc = gc_ref[...]
    d2 = d2_ref[...]

    def rot(x, c):
        rows = pl.ds(c * CHUNK, CHUNK)
        return x * cos_ref[rows, :] + pltpu.roll(x, 64, 1) * sin_ref[rows, :]

    def fwd(c, st):
        rows = pl.ds(pl.multiple_of(c * CHUNK, CHUNK), CHUNK)
        qc = rot(q_ref[rows, :], c)
        kc = rot(k_ref[rows, :], c) * (HEAD_DIM ** -0.5)
        vc = v_ref[rows, :]
        qs_ref[rows, :] = qc
        ks_ref[rows, :] = kc
        q2 = jnp.concatenate([qc * mq0, qc * mq1], axis=0)
        sc = lax.dot_general(q2, kc, (((1,), (1,)), ((), ())), preferred_element_type=F32) * d2
        o2 = jnp.dot(sc, vc, preferred_element_type=F32)
        o = o2[:CHUNK] * mv0 + o2[CHUNK:] * mv1
        o = o + jnp.dot(qc, st, preferred_element_type=F32) * xi_f
        acc_ref[rows, :] = o
        upd = jnp.dot(kc.T, vc * zeta_f, preferred_element_type=F32) * bd
        return st * gc + upd

    lax.fori_loop(0, n, fwd, jnp.zeros((LANE, LANE), F32))

    def bwd(i, st):
        c = n - 1 - i
        rows = pl.ds(pl.multiple_of(c * CHUNK, CHUNK), CHUNK)
        qc = qs_ref[rows, :]
        kc = ks_ref[rows, :]
        vc = v_ref[rows, :]
        o = acc_ref[rows, :] + jnp.dot(qc, st, preferred_element_type=F32) * xi_b
        oo = o * o
        ss0 = jnp.sum(oo * mv0, axis=1, keepdims=True)
        ss1 = jnp.sum(oo * mv1, axis=1, keepdims=True)
        scale = lax.rsqrt(ss0 * (1.0 / HEAD_DIM) + EPS) * mv0 + lax.rsqrt(ss1 * (1.0 / HEAD_DIM) + EPS) * mv1
        gate = g_ref[rows, :]
        gate = gate * jax.nn.sigmoid(gate)
        o_ref[rows, :] = (gate * (o * scale)).astype(o_ref.dtype)
        upd = jnp.dot(kc.T, vc * zeta_b, preferred_element_type=F32) * bd
        return st * gc + upd

    lax.fori_loop(0, n, bwd, jnp.zeros((LANE, LANE), F32))


def _ret_constants(s):
    half = HEAD_DIM // 2
    inv = ROPE_BASE ** (-jnp.arange(half, dtype=F32) / half)
    ang = jnp.arange(s, dtype=F32)[:, None] * inv[None, :]
    cos = jnp.cos(ang)
    sin = jnp.sin(ang)
    cos2 = jnp.concatenate([cos, cos, cos, cos], axis=1)
    sin2 = jnp.concatenate([-sin, -sin, sin, sin], axis=1)
    log_gamma = jnp.log1p(-jnp.exp2(-5.0 - jnp.arange(RET_HEADS, dtype=F32)))
    lg_pair = log_gamma.reshape(RET_HEADS // 2, 2)
    idx = jnp.arange(CHUNK, dtype=F32)
    adiff = jnp.abs(idx[:, None] - idx[None, :])
    d2 = jnp.exp(adiff[None, None] * lg_pair[:, :, None, None])
    d2 = d2.reshape(RET_HEADS // 2, 2 * CHUNK, CHUNK)
    lg_lane = jnp.repeat(lg_pair, HEAD_DIM, axis=1)
    pos = idx[None, :, None]
    lgl = lg_lane[:, None, :]
    dec = jnp.stack([jnp.exp((CHUNK - 1 - pos) * lgl), jnp.exp(pos * lgl),
                     jnp.exp((pos + 1) * lgl), jnp.exp((CHUNK - pos) * lgl)], axis=1)
    gc = jnp.exp(CHUNK * lg_lane)[:, None, :]
    return cos2, sin2, d2, dec, gc


def _retention(p_ret, b, s):
    p3 = p_ret.reshape(b, s, 4 * RET_WIDTH)
    cos2, sin2, d2, dec, gc = _ret_constants(s)
    npair = RET_HEADS // 2
    col = lambda off: pl.BlockSpec((None, s, LANE), lambda i, p: (i, 0, off + p))
    return pl.pallas_call(
        _ret_kernel,
        grid=(b, npair),
        in_specs=[col(0), col(npair), col(2 * npair), col(3 * npair),
                  pl.BlockSpec((s, LANE), lambda i, p: (0, 0)),
                  pl.BlockSpec((s, LANE), lambda i, p: (0, 0)),
                  pl.BlockSpec((None, 2 * CHUNK, CHUNK), lambda i, p: (p, 0, 0)),
                  pl.BlockSpec((None, 4, CHUNK, LANE), lambda i, p: (p, 0, 0, 0)),
                  pl.BlockSpec((None, 1, LANE), lambda i, p: (p, 0, 0))],
        out_specs=pl.BlockSpec((None, s, LANE), lambda i, p: (i, 0, p)),
        out_shape=jax.ShapeDtypeStruct((b, s, RET_WIDTH), BF16),
        scratch_shapes=[pltpu.VMEM((s, LANE), F32), pltpu.VMEM((s, LANE), F32), pltpu.VMEM((s, LANE), F32)],
        compiler_params=_cparams(("parallel", "parallel")),
        name="retention",
    )(p3, p3, p3, p3, cos2, sin2, d2, dec, gc)


def _ret_perm():
    pair = np.concatenate([np.arange(0, 32), np.arange(64, 96), np.arange(32, 64), np.arange(96, 128)])
    qperm = np.concatenate([128 * p + pair for p in range(RET_HEADS // 2)])
    return np.concatenate([_OFF_Q + qperm, _OFF_K + qperm, _OFF_V + np.arange(RET_WIDTH), _OFF_G + np.arange(RET_WIDTH)])


_PAD = 8


def _softplus(x):
    return jnp.maximum(x, 0.0) + jnp.log1p(jnp.exp(-jnp.abs(x)))


def _silu(x):
    return x * jax.nn.sigmoid(x)


def _expand_heads(cols, base, lane_head):
    acc = jnp.zeros((cols.shape[0], SSD_GW), F32)
    for r in range(SSD_HPG):
        acc = jnp.where(lane_head == r, cols[:, base + r:base + r + 1], acc)
    return acc


def _dwconv_chunk(xp_ref, cw_ref, cb_ref, c):
    rows = CHUNK + 2 * _PAD
    win = xp_ref[pl.ds(pl.multiple_of(c * CHUNK, CHUNK), rows), :]
    acc = cb_ref[...]
    for t in range(CONV_WIDTH):
        shift = (CONV_WIDTH // 2 - t) % rows
        tap = win if shift == 0 else pltpu.roll(win, shift, 0)
        acc = acc + tap[_PAD:_PAD + CHUNK, :] * cw_ref[t:t + 1, :]
    return acc


def _ssd_kernel(xbc_ref, z_ref, dt_ref, cw_ref, cb_ref, dtb_ref, alog_ref, dsk_ref, nw_ref, o_ref,
                xp_ref, xc_ref, y_ref, pc_ref, dts_ref):
    s = z_ref.shape[0]
    n = s // CHUNK
    w = xbc_ref.shape[1]
    xp_ref[0:_PAD, :] = jnp.zeros((_PAD, w), F32)
    xp_ref[_PAD + s:2 * _PAD + s, :] = jnp.zeros((_PAD, w), F32)

    def copy(c, carry):
        rows = pl.ds(pl.multiple_of(c * CHUNK, CHUNK), CHUNK)
        xp_ref[pl.ds(pl.multiple_of(c * CHUNK + _PAD, _PAD), CHUNK), :] = xbc_ref[rows, :]
        return carry

    lax.fori_loop(0, n, copy, 0)

    lane_head = lax.broadcasted_iota(jnp.int32, (1, SSD_GW), 1) // HEAD_DIM
    ii = lax.broadcasted_iota(jnp.int32, (CHUNK, CHUNK), 0)
    jj = lax.broadcasted_iota(jnp.int32, (CHUNK, CHUNK), 1)
    lower = ii >= jj
    tri = lower.astype(F32)
    a_neg = -jnp.exp(alog_ref[...])
    dt_bias = dtb_ref[...]

    def fwd(c, hf):
        rows = pl.ds(pl.multiple_of(c * CHUNK, CHUNK), CHUNK)
        xc = _silu(_dwconv_chunk(xp_ref, cw_ref, cb_ref, c))
        xc_ref[rows, :] = xc
        xs = xc[:, 0:SSD_GW]
        bm = xc[:, SSD_GW:SSD_GW + SSD_STATE]
        cm = xc[:, SSD_GW + SSD_STATE:]
        dt = _softplus(dt_ref[rows, :] + dt_bias)
        dts_ref[rows, :] = dt
        la = dt * a_neg
        p = jnp.dot(tri, la, precision=lax.Precision.HIGHEST, preferred_element_type=F32)
        pc_ref[rows, :] = p
        e = p - la
        pt = p.T
        et = e.T
        dtt = dt.T
        g = lax.dot_general(cm, bm, (((1,), (1,)), ((), ())), preferred_element_type=F32)
        y = jnp.zeros((CHUNK, SSD_GW), F32)
        for r in range(SSD_HPG):
            arg = jnp.where(lower, p[:, r:r + 1] - pt[r:r + 1, :], et[3 + r:4 + r, :] - e[:, 3 + r:4 + r])
            dtj = jnp.where(lower, dtt[r:r + 1, :], dtt[3 + r:4 + r, :])
            m = g * (jnp.exp(arg) * dtj)
            y = y + jnp.dot(m, jnp.where(lane_head == r, xs, 0.0), preferred_element_type=F32)
        pf = _expand_heads(p, 0, lane_head)
        pf_last = _expand_heads(p[CHUNK - 1:CHUNK, :], 0, lane_head)
        dtf = _expand_heads(dt, 0, lane_head)
        y = y + jnp.dot(cm, hf, preferred_element_type=F32) * jnp.exp(pf)
        y_ref[rows, :] = y
        upd = jnp.dot(bm.T, xs * (jnp.exp(pf_last - pf) * dtf), preferred_element_type=F32)
        return hf * jnp.exp(pf_last) + upd

    lax.fori_loop(0, n, fwd, jnp.zeros((SSD_STATE, SSD_GW), F32))

    dsk = dsk_ref[...]
    nw = nw_ref[...]

    def bwd(i, hb):
        c = n - 1 - i
        rows = pl.ds(pl.multiple_of(c * CHUNK, CHUNK), CHUNK)
        xs = xc_ref[rows, 0:SSD_GW]
        bm = xc_ref[rows, SSD_GW:SSD_GW + SSD_STATE]
        cm = xc_ref[rows, SSD_GW + SSD_STATE:]
        dt = dts_ref[rows, :]
        p = pc_ref[rows, :]
        e = p - dt * a_neg
        eb = _expand_heads(e, 3, lane_head)
        tb = _expand_heads(p[CHUNK - 1:CHUNK, :], 3, lane_head)
        dtb = _expand_heads(dt, 3, lane_head)
        y = y_ref[rows, :] + jnp.dot(cm, hb, preferred_element_type=F32) * jnp.exp(tb - eb)
        y = (y + dsk * xs) * _silu(z_ref[rows, :])
        ms = jnp.sum(y * y, axis=1, keepdims=True) * (1.0 / (SSD_HPG * HEAD_DIM))
        o_ref[rows, :] = (y * lax.rsqrt(ms + EPS) * nw).astype(o_ref.dtype)
        upd = jnp.dot(bm.T, xs * (jnp.exp(eb) * dtb), preferred_element_type=F32)
        return hb * jnp.exp(tb) + upd

    lax.fori_loop(0, n, bwd, jnp.zeros((SSD_STATE, SSD_GW), F32))


_SSD_XBC_W = SSD_GW + 2 * SSD_STATE
_SSD_SLAB = SSD_GROUPS * (_SSD_XBC_W + SSD_GW + LANE)


def _ssd_cols():
    gw = SSD_HPG * HEAD_DIM
    pad = lambda k: -np.ones(k, np.int64)
    xbc, zz, dtc, conv = [], [], [], []
    for g in range(SSD_GROUPS):
        xcols = np.arange(gw) + g * gw
        bcols = SSD_WIDTH + g * SSD_STATE + np.arange(SSD_STATE)
        ccols = SSD_WIDTH + SSD_GROUPS * SSD_STATE + g * SSD_STATE + np.arange(SSD_STATE)
        c_idx = np.concatenate([xcols, pad(SSD_GW - gw), bcols, ccols])
        conv.append(c_idx)
        xbc.append(np.where(c_idx >= 0, _OFF_XBC + c_idx, -1))
        zz.append(np.concatenate([_OFF_Z + xcols, pad(SSD_GW - gw)]))
        heads = _OFF_DT + g * SSD_HPG + np.arange(SSD_HPG)
        dtc.append(np.concatenate([heads, heads, pad(LANE - 2 * SSD_HPG)]))
    return np.concatenate(xbc + zz + dtc), np.concatenate(conv)


def _take_cols(a, idx):
    a = jnp.concatenate([a, jnp.zeros(a.shape[:-1] + (1,), a.dtype)], axis=-1)
    return a[..., np.where(idx >= 0, idx, a.shape[-1] - 1)]


def _ssd(p_ssd, b, s, conv_w, conv_b, dt_bias, a_log, d_skip, norm_w):
    p3 = p_ssd.reshape(b, s, _SSD_SLAB)
    _, conv_idx = _ssd_cols()
    cw = _take_cols(conv_w, conv_idx)
    cb = _take_cols(conv_b[None, :], conv_idx)
    gw = SSD_HPG * HEAD_DIM
    zpad = jnp.zeros((SSD_GROUPS, LANE - 2 * SSD_HPG), F32)
    grp = lambda v: jnp.concatenate([v[0].reshape(SSD_GROUPS, SSD_HPG), v[1].reshape(SSD_GROUPS, SSD_HPG), zpad], axis=1)
    dtb = grp(dt_bias)[:, None, :]
    alog = grp(a_log)[:, None, :]
    lanes = lambda v: jnp.pad(v.reshape(SSD_GROUPS, gw), ((0, 0), (0, SSD_GW - gw)))[:, None, :]
    dsk = lanes(jnp.repeat(d_skip, HEAD_DIM))
    nw = lanes(norm_w)
    nxb = _SSD_XBC_W // LANE
    grid_spec = dict(
        grid=(b, SSD_GROUPS),
        in_specs=[pl.BlockSpec((None, s, _SSD_XBC_W), lambda i, g: (i, 0, g)),
                  pl.BlockSpec((None, s, SSD_GW), lambda i, g: (i, 0, SSD_GROUPS * _SSD_XBC_W // SSD_GW + g)),
                  pl.BlockSpec((None, s, LANE), lambda i, g: (i, 0, SSD_GROUPS * (nxb + SSD_GW // LANE) + g)),
                  pl.BlockSpec((CONV_WIDTH, _SSD_XBC_W), lambda i, g: (0, g)),
                  pl.BlockSpec((1, _SSD_XBC_W), lambda i, g: (0, g)),
                  pl.BlockSpec((None, 1, LANE), lambda i, g: (g, 0, 0)),
                  pl.BlockSpec((None, 1, LANE), lambda i, g: (g, 0, 0)),
                  pl.BlockSpec((None, 1, SSD_GW), lambda i, g: (g, 0, 0)),
                  pl.BlockSpec((None, 1, SSD_GW), lambda i, g: (g, 0, 0))],
        out_specs=pl.BlockSpec((None, s, SSD_GW), lambda i, g: (i, 0, g)),
    )
    return pl.pallas_call(
        _ssd_kernel,
        out_shape=jax.ShapeDtypeStruct((b, s, SSD_GROUPS * SSD_GW), BF16),
        scratch_shapes=[pltpu.VMEM((s + 2 * _PAD, _SSD_XBC_W), F32), pltpu.VMEM((s, _SSD_XBC_W), F32),
                        pltpu.VMEM((s, SSD_GW), F32), pltpu.VMEM((s, LANE), F32), pltpu.VMEM((s, LANE), F32)],
        compiler_params=_cparams(("parallel", "parallel")),
        name="ssd",
        **grid_spec,
    )(p3, p3, p3, cw, cb, dtb, alog, dsk, nw)


def _scan_chunk(a, u, reverse):
    row = lax.broadcasted_iota(jnp.int32, (CHUNK, 1), 0)
    d = 1
    while d < CHUNK:
        if reverse:
            keep = row < CHUNK - d
            shift = CHUNK - d
        else:
            keep = row >= d
            shift = d
        a_sh = jnp.where(keep, pltpu.roll(a, shift, 0), 1.0)
        u_sh = jnp.where(keep, pltpu.roll(u, shift, 0), 0.0)
        u = a * u_sh + u
        a = a * a_sh
        d *= 2
    return a, u


def _lru_kernel(x_ref, cw_ref, cb_ref, wg_ref, bg_ref, lam_ref, o_ref, xp_ref, hf_ref, ab_ref, ub_ref):
    s = x_ref.shape[0]
    n = s // CHUNK
    w = LRU_WIDTH
    xp_ref[0:_PAD, :] = jnp.zeros((_PAD, w), F32)
    xp_ref[_PAD + s:2 * _PAD + s, :] = jnp.zeros((_PAD, w), F32)

    def copy(c, carry):
        rows = pl.ds(pl.multiple_of(c * CHUNK, CHUNK), CHUNK)
        xp_ref[pl.ds(pl.multiple_of(c * CHUNK + _PAD, _PAD), CHUNK), :] = x_ref[rows, 0:w]
        return carry

    lax.fori_loop(0, n, copy, 0)
    nsp = -LRU_C * _softplus(-lam_ref[...])

    def gates(pre, xc, k):
        r = jax.nn.sigmoid(pre[:, 2 * k * w:(2 * k + 1) * w])
        i = jax.nn.sigmoid(pre[:, (2 * k + 1) * w:(2 * k + 2) * w])
        log_a = r * nsp[k:k + 1, :]
        a = jnp.exp(log_a)
        return a, jnp.sqrt(-jnp.tanh(log_a) * (a * a + 1.0)) * (i * xc)

    def fwd(c, h):
        rows = pl.ds(pl.multiple_of(c * CHUNK, CHUNK), CHUNK)
        xc = _dwconv_chunk(xp_ref, cw_ref, cb_ref, c)
        pre = jnp.dot(xc.astype(BF16), wg_ref[...], preferred_element_type=F32) + bg_ref[...]
        a_b, u_b = gates(pre, xc, 1)
        ab_ref[rows, :] = a_b
        ub_ref[rows, :] = u_b
        a_f, u_f = gates(pre, xc, 0)
        a_s, u_s = _scan_chunk(a_f, u_f, False)
        hc = u_s + a_s * h
        hf_ref[rows, :] = hc
        return hc[CHUNK - 1:CHUNK, :]

    lax.fori_loop(0, n, fwd, jnp.zeros((1, w), F32))

    def bwd(i, h):
        c = n - 1 - i
        rows = pl.ds(pl.multiple_of(c * CHUNK, CHUNK), CHUNK)
        a_s, u_s = _scan_chunk(ab_ref[rows, :], ub_ref[rows, :], True)
        hc = u_s + a_s * h
        gate = x_ref[rows, w:2 * w]
        o_ref[rows, :] = ((hf_ref[rows, :] + hc) * jax.nn.gelu(gate)).astype(o_ref.dtype)
        return hc[0:1, :]

    lax.fori_loop(0, n, bwd, jnp.zeros((1, w), F32))


def _block_diag(wblk):
    eye = jnp.eye(LRU_BLOCKS, dtype=wblk.dtype)
    return jnp.einsum('hij,hk->hikj', wblk, eye).reshape(LRU_WIDTH, LRU_WIDTH)


def _lru(p_lru, b, s, conv_w, conv_b, lam, wa, ba, wx, bx):
    p3 = p_lru.reshape(b, s, 2 * LRU_WIDTH)
    wg = jnp.concatenate([_block_diag(wa[0]), _block_diag(wx[0]), _block_diag(wa[1]), _block_diag(wx[1])], axis=1).astype(BF16)
    bg = jnp.concatenate([ba[0], bx[0], ba[1], bx[1]])[None, :]
    w = LRU_WIDTH
    const = lambda shape: pl.BlockSpec(shape, lambda i: (0,) * len(shape))
    return pl.pallas_call(
        _lru_kernel,
        grid=(b,),
        in_specs=[pl.BlockSpec((None, s, 2 * w), lambda i: (i, 0, 0)),
                  const((CONV_WIDTH, w)), const((1, w)), const((w, 4 * w)), const((1, 4 * w)), const((2, w))],
        out_specs=pl.BlockSpec((None, s, w), lambda i: (i, 0, 0)),
        out_shape=jax.ShapeDtypeStruct((b, s, w), BF16),
        scratch_shapes=[pltpu.VMEM((s + 2 * _PAD, w), F32), pltpu.VMEM((s, w), F32),
                        pltpu.VMEM((s, w), F32), pltpu.VMEM((s, w), F32)],
        compiler_params=_cparams(("parallel",)),
        name="rglru",
    )(p3, conv_w, conv_b[None, :], wg, bg, lam)


def _outproj_kernel(h_ref, a_ref, b_ref, c_ref, wa_ref, wb_ref, wc_ref, o_ref):
    acc = h_ref[...]
    acc = acc + jnp.dot(a_ref[...], wa_ref[...], preferred_element_type=F32)
    acc = acc + jnp.dot(b_ref[...], wb_ref[...], preferred_element_type=F32)
    acc = acc + jnp.dot(c_ref[...], wc_ref[...], preferred_element_type=F32)
    o_ref[...] = acc


def _outproj(h2, o_ret, o_ssd, o_lru, w_ret, w_ssd, w_lru):
    t, d = h2.shape
    tm = ROW_TILE
    row = lambda n: pl.BlockSpec((tm, n), lambda i: (i, 0))
    full = lambda n: pl.BlockSpec((n, d), lambda i: (0, 0))
    ka, kb, kc = o_ret.shape[1], o_ssd.shape[1], o_lru.shape[1]
    return pl.pallas_call(
        _outproj_kernel,
        grid=(t // tm,),
        in_specs=[row(d), row(ka), row(kb), row(kc), full(ka), full(kb), full(kc)],
        out_specs=row(d),
        out_shape=jax.ShapeDtypeStruct((t, d), F32),
        compiler_params=_cparams(("parallel",)),
        name="outproj",
    )(h2, o_ret, o_ssd, o_lru, w_ret, w_ssd, w_lru)


def _outproj_weights(w_out):
    gw = SSD_HPG * HEAD_DIM
    idx = np.concatenate([np.concatenate([RET_WIDTH + g * gw + np.arange(gw), -np.ones(SSD_GW - gw, np.int64)])
                          for g in range(SSD_GROUPS)])
    w_ssd = _take_cols(w_out.T, idx).T
    return (w_out[:RET_WIDTH].astype(BF16), w_ssd.astype(BF16), w_out[RET_WIDTH + SSD_WIDTH:].astype(BF16))


def _kv_kernel(m_ref, g_ref, w_ref, o_ref):
    mn = _rms(m_ref[...], g_ref[...]).astype(BF16)
    o_ref[...] = jnp.dot(mn, w_ref[...], preferred_element_type=F32).astype(o_ref.dtype)


def _kv_proj(mem2, g, wkv):
    t, d = mem2.shape
    n = wkv.shape[1]
    tm = 256
    return pl.pallas_call(
        _kv_kernel,
        grid=(t // tm,),
        in_specs=[pl.BlockSpec((tm, d), lambda i: (i, 0)), pl.BlockSpec((1, d), lambda i: (0, 0)),
                  pl.BlockSpec((d, n), lambda i: (0, 0))],
        out_specs=pl.BlockSpec((tm, n), lambda i: (i, 0)),
        out_shape=jax.ShapeDtypeStruct((t, n), BF16),
        compiler_params=_cparams(("parallel",)),
        name="kvproj",
    )(mem2, g.reshape(1, d), wkv)


def _xattn_kernel(h_ref, g_ref, wq_ref, k_ref, v_ref, wo_ref, o_ref):
    h = h_ref[...]
    xn = _rms(h, g_ref[...]).astype(BF16)
    q = jnp.dot(xn, wq_ref[...], preferred_element_type=F32).astype(BF16)
    outs = []
    for hd in range(CA_HEADS):
        cols = slice(hd * CA_HEAD_DIM, (hd + 1) * CA_HEAD_DIM)
        sc = lax.dot_general(q[:, cols], k_ref[:, cols], (((1,), (1,)), ((), ())), preferred_element_type=F32)
        sc = sc * (CA_HEAD_DIM ** -0.5)
        e = jnp.exp(sc - jnp.max(sc, axis=-1, keepdims=True))
        p = e / jnp.sum(e, axis=-1, keepdims=True)
        outs.append(jnp.dot(p.astype(BF16), v_ref[:, cols], preferred_element_type=F32).astype(BF16))
    o = jnp.concatenate(outs, axis=-1)
    o_ref[...] = h + jnp.dot(o, wo_ref[...], preferred_element_type=F32)


def _xattn(h2, b, s, g, wq, kv, wo):
    t, d = h2.shape
    m = kv.shape[0] // b
    ts = ROW_TILE
    nts = s // ts
    return pl.pallas_call(
        _xattn_kernel,
        grid=(b, nts),
        in_specs=[pl.BlockSpec((ts, d), lambda i, j: (i * nts + j, 0)),
                  pl.BlockSpec((1, d), lambda i, j: (0, 0)),
                  pl.BlockSpec((d, d), lambda i, j: (0, 0)),
                  pl.BlockSpec((m, d), lambda i, j: (i, 0)),
                  pl.BlockSpec((m, d), lambda i, j: (i, 1)),
                  pl.BlockSpec((d, d), lambda i, j: (0, 0))],
        out_specs=pl.BlockSpec((ts, d), lambda i, j: (i * nts + j, 0)),
        out_shape=jax.ShapeDtypeStruct((t, d), F32),
        compiler_params=_cparams(("parallel", "parallel")),
        name="xattn",
    )(h2, g.reshape(1, d), wq, kv, kv, wo)


_ROUTER_ROWS = SUBLANE + N_EXPERTS


def _first_argmax(v, row):
    m = jnp.max(v, axis=0, keepdims=True)
    return m, jnp.min(jnp.where(v == m, row, SUBLANE), axis=0, keepdims=True)


def _router_kernel(h_ref, g_ref, w_ref, b_ref, ids_ref, gates_ref):
    xn = _rms(h_ref[...], g_ref[...]).astype(BF16)
    lg = lax.dot_general(w_ref[...], xn, (((1,), (1,)), ((), ())), preferred_element_type=F32) + b_ref[...]
    tm = lg.shape[1]
    row = lax.broadcasted_iota(jnp.int32, (SUBLANE, tm), 0)
    gl = jnp.where(row < N_GROUPS, lg[0:SUBLANE], -jnp.inf)
    ge = jnp.exp(gl - jnp.max(gl, axis=0, keepdims=True))
    gp_all = ge / jnp.sum(ge, axis=0, keepdims=True)
    gp, gi = _first_argmax(gp_all, row)
    el = lg[SUBLANE:2 * SUBLANE]
    for g in range(1, N_GROUPS):
        el = jnp.where(gi == g, lg[SUBLANE * (g + 1):SUBLANE * (g + 2)], el)
    ee = jnp.exp(el - jnp.max(el, axis=0, keepdims=True))
    ep = ee / jnp.sum(ee, axis=0, keepdims=True)
    v1, i1 = _first_argmax(ep, row)
    v2, i2 = _first_argmax(jnp.where(row == i1, -1.0, ep), row)
    den = v1 + v2
    ids = jnp.where(row == 0, gi * EXPERTS_PER_GROUP + i1, jnp.where(row == 1, gi * EXPERTS_PER_GROUP + i2, 0))
    gates = jnp.where(row == 0, gp * v1 / den, jnp.where(row == 1, gp * v2 / den, 0.0))
    ids_ref[...] = ids
    gates_ref[...] = gates


def _router(h2, g, wg, bg, we, be):
    t, d = h2.shape
    tm = ROW_TILE
    zrow = jnp.zeros((SUBLANE - N_GROUPS, d), F32)
    w = jnp.concatenate([wg.T, zrow, we.T], axis=0).astype(BF16)
    bias = jnp.concatenate([bg, jnp.zeros((SUBLANE - N_GROUPS,), F32), be])[:, None]
    return pl.pallas_call(
        _router_kernel,
        grid=(t // tm,),
        in_specs=[pl.BlockSpec((tm, d), lambda i: (i, 0)), pl.BlockSpec((1, d), lambda i: (0, 0)),
                  pl.BlockSpec((_ROUTER_ROWS, d), lambda i: (0, 0)), pl.BlockSpec((_ROUTER_ROWS, 1), lambda i: (0, 0))],
        out_specs=[pl.BlockSpec((SUBLANE, tm), lambda i: (0, i)), pl.BlockSpec((SUBLANE, tm), lambda i: (0, i))],
        out_shape=[jax.ShapeDtypeStruct((SUBLANE, t), jnp.int32), jax.ShapeDtypeStruct((SUBLANE, t), F32)],
        compiler_params=_cparams(("parallel",)),
        name="router",
    )(h2, g.reshape(1, d), w, bias)


MOE_BLK = 256


_ASSIGN_BITS = 17


def _dispatch_plan(ids, t):
    n_assign = t * TOP_K
    assert n_assign <= 1 << _ASSIGN_BITS
    e = ids[:TOP_K].T.reshape(n_assign)
    key = jnp.left_shift(e, _ASSIGN_BITS) | jnp.arange(n_assign, dtype=jnp.int32)
    order = jnp.sort(key) & ((1 << _ASSIGN_BITS) - 1)
    order = jnp.concatenate([order, jnp.zeros((MOE_BLK,), jnp.int32)])
    experts = jnp.arange(N_EXPERTS, dtype=jnp.int32)
    counts = jnp.sum((e[None, :] == experts[:, None]).astype(jnp.int32), axis=1)
    start = jnp.cumsum(counts) - counts
    nb = (counts + MOE_BLK - 1) // MOE_BLK
    blk_end = jnp.cumsum(nb)
    n_blocks = n_assign // MOE_BLK + N_EXPERTS
    j = jnp.arange(n_blocks, dtype=jnp.int32)
    be = jnp.minimum(jnp.sum((blk_end[None, :] <= j[:, None]).astype(jnp.int32), axis=1), N_EXPERTS - 1)
    onehot = (be[:, None] == experts[None, :]).astype(jnp.int32)
    pick = lambda v: jnp.sum(onehot * v[None, :], axis=1)
    r = j - pick(blk_end - nb)
    row_start = jnp.clip(pick(start) + r * MOE_BLK, 0, n_assign)
    n_rows = jnp.clip(pick(counts) - r * MOE_BLK, 0, MOE_BLK)
    return order, be, row_start, n_rows, blk_end[-1:].astype(jnp.int32)


def _expert_kernel(be_ref, rs_ref, nr_ref, nu_ref, ord_ref, h_hbm, g_ref, w1_ref, w3_ref, w2_ref, y_hbm,
                   xbuf, ybuf, wb1, wb3, wb2, gsem, ssem, *, n_assign):
    i = pl.program_id(0)
    n_used = nu_ref[0]
    slot = i % 2

    def gather(blk, s):
        base = rs_ref[blk]

        def body(r, carry):
            tok = lax.shift_right_logical(ord_ref[base + r], 1)
            pltpu.make_async_copy(h_hbm.at[pl.ds(tok, 1), :], xbuf.at[s, pl.ds(r, 1), :], gsem.at[s]).start()
            return carry
        lax.fori_loop(0, MOE_BLK, body, 0, unroll=8)

    def scatter(blk, s):
        base = rs_ref[blk]
        cnt = nr_ref[blk]
        dump = n_assign + s * MOE_BLK

        def body(r, carry):
            dest = jnp.where(r < cnt, ord_ref[base + r], dump + r)
            pltpu.make_async_copy(ybuf.at[s, pl.ds(r, 1), :], y_hbm.at[pl.ds(dest, 1), :], ssem.at[s]).start()
            return carry
        lax.fori_loop(0, MOE_BLK, body, 0, unroll=8)

    def wait_gather(s):
        pltpu.make_async_copy(h_hbm.at[pl.ds(0, MOE_BLK), :], xbuf.at[s], gsem.at[s]).wait()

    def wait_scatter(s):
        pltpu.make_async_copy(ybuf.at[s], y_hbm.at[pl.ds(0, MOE_BLK), :], ssem.at[s]).wait()

    @pl.when(i == 0)
    def _():
        ybuf[0] = jnp.zeros((MOE_BLK, ybuf.shape[2]), F32)
        for s in range(2):
            fill = pltpu.make_async_copy(ybuf.at[0], y_hbm.at[pl.ds(n_assign + s * MOE_BLK, MOE_BLK), :], ssem.at[0])
            fill.start()
            fill.wait()
        gather(0, 0)

    @pl.when(i + 1 < n_used)
    def _():
        gather(i + 1, 1 - slot)

    @pl.when(i < n_used)
    def _():
        @pl.when((i == 0) | (be_ref[i] != be_ref[jnp.maximum(i - 1, 0)]))
        def _():
            wb1[...] = w1_ref[...].astype(BF16)
            wb3[...] = w3_ref[...].astype(BF16)
            wb2[...] = w2_ref[...].astype(BF16)

        wait_gather(slot)

        @pl.when(i >= 2)
        def _():
            wait_scatter(slot)

        xn = _rms(xbuf[slot], g_ref[...]).astype(BF16)
        h1 = jnp.dot(xn, wb1[...], preferred_element_type=F32)
        h3 = jnp.dot(xn, wb3[...], preferred_element_type=F32)
        hid = (_silu(h1) * h3).astype(BF16)
        ybuf[slot] = jnp.dot(hid, wb2[...], preferred_element_type=F32)
        scatter(i, slot)

        @pl.when(i == n_used - 1)
        def _():
            wait_scatter(slot)

            @pl.when(i >= 1)
            def _():
                wait_scatter(1 - slot)


def _experts(h2, g, order, block_e, row_start, n_rows, n_used, w1, w3, w2):
    t, d = h2.shape
    n_assign = t * TOP_K
    n_blocks = block_e.shape[0]
    de = w1.shape[2]
    wmap = lambda i, be, rs, nr, nu, od: (be[i], 0, 0)
    grid_spec = pltpu.PrefetchScalarGridSpec(
        num_scalar_prefetch=5,
        grid=(n_blocks,),
        in_specs=[pl.BlockSpec(memory_space=pl.ANY),
                  pl.BlockSpec((1, d), lambda i, *_: (0, 0)),
                  pl.BlockSpec((None, d, de), wmap),
                  pl.BlockSpec((None, d, de), wmap),
                  pl.BlockSpec((None, de, d), wmap)],
        out_specs=pl.BlockSpec(memory_space=pl.ANY),
        scratch_shapes=[pltpu.VMEM((2, MOE_BLK, d), F32), pltpu.VMEM((2, MOE_BLK, d), F32),
                        pltpu.VMEM((d, de), BF16), pltpu.VMEM((d, de), BF16), pltpu.VMEM((de, d), BF16),
                        pltpu.SemaphoreType.DMA((2,)), pltpu.SemaphoreType.DMA((2,))],
    )
    return pl.pallas_call(
        functools.partial(_expert_kernel, n_assign=n_assign),
        grid_spec=grid_spec,
        out_shape=jax.ShapeDtypeStruct((n_assign + 2 * MOE_BLK, d), F32),
        compiler_params=_cparams(("arbitrary",)),
        name="experts",
    )(block_e, row_start, n_rows, n_used, order, h2, g.reshape(1, d), w1, w3, w2)


def _combine_kernel(h_ref, y_ref, gt_ref, g_ref, o_ref, *, final):
    d = h_ref.shape[1]
    gt = gt_ref[...]
    out = h_ref[...] + y_ref[:, 0:d] * gt[:, 0:1] + y_ref[:, d:2 * d] * gt[:, 1:2]
    if final:
        out = _rms(out, g_ref[...])
    o_ref[...] = out


def _combine(h2, y, gates, g_final, final):
    t, d = h2.shape
    tm = ROW_TILE
    y2 = y.reshape(y.shape[0] // TOP_K, TOP_K * d)
    return pl.pallas_call(
        functools.partial(_combine_kernel, final=final),
        grid=(t // tm,),
        in_specs=[pl.BlockSpec((tm, d), lambda i: (i, 0)),
                  pl.BlockSpec((tm, TOP_K * d), lambda i: (i, 0)),
                  pl.BlockSpec((tm, TOP_K), lambda i: (i, 0)),
                  pl.BlockSpec((1, d), lambda i: (0, 0))],
        out_specs=pl.BlockSpec((tm, d), lambda i: (i, 0)),
        out_shape=jax.ShapeDtypeStruct((t, d), F32),
        compiler_params=_cparams(("parallel",)),
        name="combine",
    )(h2, y2, gates[:TOP_K].T, g_final.reshape(1, d))


def _moe(h2, g, wg, bg, we, be, w1, w3, w2, g_final, final):
    t = h2.shape[0]
    ids, gates = _router(h2, g, wg, bg, we, be)
    order, block_e, row_start, n_rows, n_used = _dispatch_plan(ids, t)
    y = _experts(h2, g, order, block_e, row_start, n_rows, n_used, w1, w3, w2)
    return _combine(h2, y, gates, g_final, final)


def kernel(x, mem, g_mix, w_in, w_out, ssd_conv_w, ssd_conv_b, ssd_dt_bias, ssd_a_log, ssd_d, ssd_norm, lru_conv_w, lru_conv_b, lru_lambda, lru_wa, lru_ba, lru_wx, lru_bx, g_ca, g_mem, ca_wq, ca_wkv, ca_wo, g_moe, moe_wg, moe_bg, moe_we, moe_be, moe_w1, moe_w3, moe_w2, g_final):
    b, s, d = x.shape
    depth = g_mix.shape[0]
    h = x.reshape(b * s, d)
    mem2 = mem.reshape(b * mem.shape[1], d)
    ret_perm = _ret_perm()
    ssd_cols = _ssd_cols()[0]
    for l in range(depth):
        w_ret = w_in[l][:, ret_perm].astype(BF16)
        w_ssd = _take_cols(w_in[l], ssd_cols).astype(BF16)
        w_lru = w_in[l][:, _OFF_XR:].astype(BF16)
        p_ret, p_ssd, p_lru = _inproj(h, g_mix[l], w_ret, w_ssd, w_lru)
        o_ret = _retention(p_ret, b, s)
        o_ssd = _ssd(p_ssd, b, s, ssd_conv_w[l], ssd_conv_b[l], ssd_dt_bias[l], ssd_a_log[l], ssd_d[l], ssd_norm[l])
        o_lru = _lru(p_lru, b, s, lru_conv_w[l], lru_conv_b[l], lru_lambda[l], lru_wa[l], lru_ba[l], lru_wx[l], lru_bx[l])
        h = _outproj(h, o_ret.reshape(b * s, -1), o_ssd.reshape(b * s, -1), o_lru.reshape(b * s, -1),
                     *_outproj_weights(w_out[l]))
        kv = _kv_proj(mem2, g_mem[l], ca_wkv[l].astype(BF16))
        h = _xattn(h, b, s, g_ca[l], ca_wq[l].astype(BF16), kv, ca_wo[l].astype(BF16))
        h = _moe(h, g_moe[l], moe_wg[l], moe_bg[l], moe_we[l], moe_be[l],
                 moe_w1[l], moe_w3[l], moe_w2[l], g_final, l == depth - 1)
    return h.reshape(b, s, d)
```

```python
import functools
import math

import jax
import jax.numpy as jnp
import numpy as np
from jax import lax
from jax.experimental import pallas as pl
from jax.experimental.pallas import tpu as pltpu

D_MODEL = 1024
HEAD_DIM = 64
RET_WIDTH = 384
SSD_WIDTH = 384
LRU_WIDTH = 256
RET_HEADS = 6
SSD_HEADS = 6
SSD_GROUPS = 2
SSD_HPG = 3
SSD_STATE = 128
SSD_CONV_DIM = SSD_WIDTH + 2 * SSD_GROUPS * SSD_STATE
LRU_BLOCKS = 4
LRU_C = 8.0
CONV_WIDTH = 4
ROPE_BASE = 10000.0
CA_HEADS = 4
CA_HEAD_DIM = 256
N_GROUPS = 4
EXPERTS_PER_GROUP = 8
N_EXPERTS = 32
TOP_K = 2
D_EXPERT = 512
EPS = 1e-6

_OFF_Q, _OFF_K, _OFF_V, _OFF_G = 0, 384, 768, 1152
_OFF_Z = 1536
_OFF_XBC = 1920
_OFF_DT = 2816
_OFF_XR = 2822
_OFF_GATE = 3078
D_IN = 3334

LANE = 128
SUBLANE = 8
VMEM_LIMIT = 56 * 1024 * 1024

CHUNK = 128
ROW_TILE = 512
SSD_GW = 256

F32 = jnp.float32
BF16 = jnp.bfloat16


def _cparams(sem):
    return pltpu.CompilerParams(dimension_semantics=sem, vmem_limit_bytes=VMEM_LIMIT)


def _rms(x, g):
    return x * lax.rsqrt(jnp.mean(x * x, axis=-1, keepdims=True) + EPS) * g


def _inproj_kernel(x_ref, g_ref, wr_ref, ws_ref, wl_ref, pr_ref, ps_ref, pq_ref):
    xn = _rms(x_ref[...], g_ref[...]).astype(BF16)
    pr_ref[...] = jnp.dot(xn, wr_ref[...], preferred_element_type=F32)
    ps_ref[...] = jnp.dot(xn, ws_ref[...], preferred_element_type=F32)
    pq_ref[...] = jnp.dot(xn, wl_ref[...], preferred_element_type=F32)


def _inproj(x2, g, w_ret, w_ssd, w_lru):
    t, d = x2.shape
    tm = 256
    n_r, n_s, n_l = w_ret.shape[1], w_ssd.shape[1], w_lru.shape[1]
    full = lambda n: pl.BlockSpec((d, n), lambda i: (0, 0))
    return pl.pallas_call(
        _inproj_kernel,
        grid=(t // tm,),
        in_specs=[pl.BlockSpec((tm, d), lambda i: (i, 0)),
                  pl.BlockSpec((1, d), lambda i: (0, 0)),
                  full(n_r), full(n_s), full(n_l)],
        out_specs=[pl.BlockSpec((tm, n_r), lambda i: (i, 0)),
                   pl.BlockSpec((tm, n_s), lambda i: (i, 0)),
                   pl.BlockSpec((tm, n_l), lambda i: (i, 0))],
        out_shape=[jax.ShapeDtypeStruct((t, n_r), F32),
                   jax.ShapeDtypeStruct((t, n_s), F32),
                   jax.ShapeDtypeStruct((t, n_l), F32)],
        compiler_params=_cparams(("parallel",)),
        name="inproj",
    )(x2, g.reshape(1, d), w_ret, w_ssd, w_lru)


def _ret_kernel(q_ref, k_ref, v_ref, g_ref, cos_ref, sin_ref, d2_ref, dec_ref, gc_ref,
                o_ref, qs_ref, ks_ref, acc_ref):
    s = q_ref.shape[0]
    n = s // CHUNK
    lane = lax.broadcasted_iota(jnp.int32, (1, LANE), 1)
    mq0 = ((lane // 32) % 2 == 0).astype(F32)
    mq1 = 1.0 - mq0
    mv0 = (lane < HEAD_DIM).astype(F32)
    mv1 = 1.0 - mv0
    krow = lax.broadcasted_iota(jnp.int32, (LANE, LANE), 0)
    vcol = lax.broadcasted_iota(jnp.int32, (LANE, LANE), 1)
    bd = (((krow // 32) % 2) == (vcol // HEAD_DIM)).astype(F32)
    zeta_f = dec_ref[0]
    zeta_b = dec_ref[1]
    xi_f = dec_ref[2]
    xi_b = dec_ref[3]
    gc = gc_ref[...]
    d2 = d2_ref[...]

    def rot(x, c):
        rows = pl.ds(c * CHUNK, CHUNK)
        return x * cos_ref[rows, :] + pltpu.roll(x, 64, 1) * sin_ref[rows, :]

    def fwd(c, st):
        rows = pl.ds(pl.multiple_of(c * CHUNK, CHUNK), CHUNK)
        qc = rot(q_ref[rows, :], c)
        kc = rot(k_ref[rows, :], c) * (HEAD_DIM ** -0.5)
        vc = v_ref[rows, :]
        qs_ref[rows, :] = qc
        ks_ref[rows, :] = kc
        q2 = jnp.concatenate([qc * mq0, qc * mq1], axis=0)
        sc = lax.dot_general(q2, kc, (((1,), (1,)), ((), ())), preferred_element_type=F32) * d2
        o2 = jnp.dot(sc, vc, preferred_element_type=F32)
        o = o2[:CHUNK] * mv0 + o2[CHUNK:] * mv1
        o = o + jnp.dot(qc, st, preferred_element_type=F32) * xi_f
        acc_ref[rows, :] = o
        upd = jnp.dot(kc.T, vc * zeta_f, preferred_element_type=F32) * bd
        return st * gc + upd

    lax.fori_loop(0, n, fwd, jnp.zeros((LANE, LANE), F32), unroll=2)

    def bwd(i, st):
        c = n - 1 - i
        rows = pl.ds(pl.multiple_of(c * CHUNK, CHUNK), CHUNK)
        qc = qs_ref[rows, :]
        kc = ks_ref[rows, :]
        vc = v_ref[rows, :]
        o = acc_ref[rows, :] + jnp.dot(qc, st, preferred_element_type=F32) * xi_b
        oo = o * o
        ss0 = jnp.sum(oo * mv0, axis=1, keepdims=True)
        ss1 = jnp.sum(oo * mv1, axis=1, keepdims=True)
        scale = lax.rsqrt(ss0 * (1.0 / HEAD_DIM) + EPS) * mv0 + lax.rsqrt(ss1 * (1.0 / HEAD_DIM) + EPS) * mv1
        gate = g_ref[rows, :]
        gate = gate * jax.nn.sigmoid(gate)
        o_ref[rows, :] = (gate * (o * scale)).astype(o_ref.dtype)
        upd = jnp.dot(kc.T, vc * zeta_b, preferred_element_type=F32) * bd
        return st * gc + upd

    lax.fori_loop(0, n, bwd, jnp.zeros((LANE, LANE), F32), unroll=2)


def _ret_constants(s):
    half = HEAD_DIM // 2
    inv = ROPE_BASE ** (-jnp.arange(half, dtype=F32) / half)
    ang = jnp.arange(s, dtype=F32)[:, None] * inv[None, :]
    cos = jnp.cos(ang)
    sin = jnp.sin(ang)
    cos2 = jnp.concatenate([cos, cos, cos, cos], axis=1)
    sin2 = jnp.concatenate([-sin, -sin, sin, sin], axis=1)
    log_gamma = jnp.log1p(-jnp.exp2(-5.0 - jnp.arange(RET_HEADS, dtype=F32)))
    lg_pair = log_gamma.reshape(RET_HEADS // 2, 2)
    idx = jnp.arange(CHUNK, dtype=F32)
    adiff = jnp.abs(idx[:, None] - idx[None, :])
    d2 = jnp.exp(adiff[None, None] * lg_pair[:, :, None, None])
    d2 = d2.reshape(RET_HEADS // 2, 2 * CHUNK, CHUNK)
    lg_lane = jnp.repeat(lg_pair, HEAD_DIM, axis=1)
    pos = idx[None, :, None]
    lgl = lg_lane[:, None, :]
    dec = jnp.stack([jnp.exp((CHUNK - 1 - pos) * lgl), jnp.exp(pos * lgl),
                     jnp.exp((pos + 1) * lgl), jnp.exp((CHUNK - pos) * lgl)], axis=1)
    gc = jnp.exp(CHUNK * lg_lane)[:, None, :]
    return cos2, sin2, d2, dec, gc


def _retention(p_ret, b, s):
    p3 = p_ret.reshape(b, s, 4 * RET_WIDTH)
    cos2, sin2, d2, dec, gc = _ret_constants(s)
    npair = RET_HEADS // 2
    col = lambda off: pl.BlockSpec((None, s, LANE), lambda i, p: (i, 0, off + p))
    return pl.pallas_call(
        _ret_kernel,
        grid=(b, npair),
        in_specs=[col(0), col(npair), col(2 * npair), col(3 * npair),
                  pl.BlockSpec((s, LANE), lambda i, p: (0, 0)),
                  pl.BlockSpec((s, LANE), lambda i, p: (0, 0)),
                  pl.BlockSpec((None, 2 * CHUNK, CHUNK), lambda i, p: (p, 0, 0)),
                  pl.BlockSpec((None, 4, CHUNK, LANE), lambda i, p: (p, 0, 0, 0)),
                  pl.BlockSpec((None, 1, LANE), lambda i, p: (p, 0, 0))],
        out_specs=pl.BlockSpec((None, s, LANE), lambda i, p: (i, 0, p)),
        out_shape=jax.ShapeDtypeStruct((b, s, RET_WIDTH), BF16),
        scratch_shapes=[pltpu.VMEM((s, LANE), F32), pltpu.VMEM((s, LANE), F32), pltpu.VMEM((s, LANE), F32)],
        compiler_params=_cparams(("parallel", "parallel")),
        name="retention",
    )(p3, p3, p3, p3, cos2, sin2, d2, dec, gc)


def _ret_perm():
    pair = np.concatenate([np.arange(0, 32), np.arange(64, 96), np.arange(32, 64), np.arange(96, 128)])
    qperm = np.concatenate([128 * p + pair for p in range(RET_HEADS // 2)])
    return np.concatenate([_OFF_Q + qperm, _OFF_K + qperm, _OFF_V + np.arange(RET_WIDTH), _OFF_G + np.arange(RET_WIDTH)])


_PAD = 8


def _softplus(x):
    return jnp.maximum(x, 0.0) + jnp.log1p(jnp.exp(-jnp.abs(x)))


def _silu(x):
    return x * jax.nn.sigmoid(x)


def _expand_heads(cols, base, lane_head):
    acc = jnp.zeros((cols.shape[0], SSD_GW), F32)
    for r in range(SSD_HPG):
        acc = jnp.where(lane_head == r, cols[:, base + r:base + r + 1], acc)
    return acc


def _dwconv_chunk(xp_ref, cw_ref, cb_ref, c):
    rows = CHUNK + 2 * _PAD
    win = xp_ref[pl.ds(pl.multiple_of(c * CHUNK, CHUNK), rows), :]
    acc = cb_ref[...]
    for t in range(CONV_WIDTH):
        shift = (CONV_WIDTH // 2 - t) % rows
        tap = win if shift == 0 else pltpu.roll(win, shift, 0)
        acc = acc + tap[_PAD:_PAD + CHUNK, :] * cw_ref[t:t + 1, :]
    return acc


def _ssd_kernel(xbc_ref, z_ref, dt_ref, cw_ref, cb_ref, dtb_ref, alog_ref, dsk_ref, nw_ref, o_ref,
                xp_ref, xc_ref, y_ref, pc_ref, dts_ref):
    s = z_ref.shape[0]
    n = s // CHUNK
    w = xbc_ref.shape[1]
    xp_ref[0:_PAD, :] = jnp.zeros((_PAD, w), F32)
    xp_ref[_PAD + s:2 * _PAD + s, :] = jnp.zeros((_PAD, w), F32)

    def copy(c, carry):
        rows = pl.ds(pl.multiple_of(c * CHUNK, CHUNK), CHUNK)
        xp_ref[pl.ds(pl.multiple_of(c * CHUNK + _PAD, _PAD), CHUNK), :] = xbc_ref[rows, :]
        return carry

    lax.fori_loop(0, n, copy, 0)

    lane_head = lax.broadcasted_iota(jnp.int32, (1, SSD_GW), 1) // HEAD_DIM
    ii = lax.broadcasted_iota(jnp.int32, (CHUNK, CHUNK), 0)
    jj = lax.broadcasted_iota(jnp.int32, (CHUNK, CHUNK), 1)
    lower = ii >= jj
    tri = lower.astype(F32)
    a_neg = -jnp.exp(alog_ref[...])
    dt_bias = dtb_ref[...]

    def fwd(c, hf):
        rows = pl.ds(pl.multiple_of(c * CHUNK, CHUNK), CHUNK)
        xc = _silu(_dwconv_chunk(xp_ref, cw_ref, cb_ref, c))
        xc_ref[rows, :] = xc
        xs = xc[:, 0:SSD_GW]
        bm = xc[:, SSD_GW:SSD_GW + SSD_STATE]
        cm = xc[:, SSD_GW + SSD_STATE:]
        dt = _softplus(dt_ref[rows, :] + dt_bias)
        dts_ref[rows, :] = dt
        la = dt * a_neg
        p = jnp.dot(tri, la, precision=lax.Precision.HIGHEST, preferred_element_type=F32)
        pc_ref[rows, :] = p
        e = p - la
        pt = p.T
        et = e.T
        dtt = dt.T
        g = lax.dot_general(cm, bm, (((1,), (1,)), ((), ())), preferred_element_type=F32)
        y = jnp.zeros((CHUNK, SSD_GW), F32)
        for r in range(SSD_HPG):
            arg = jnp.where(lower, p[:, r:r + 1] - pt[r:r + 1, :], et[3 + r:4 + r, :] - e[:, 3 + r:4 + r])
            dtj = jnp.where(lower, dtt[r:r + 1, :], dtt[3 + r:4 + r, :])
            m = g * (jnp.exp(arg) * dtj)
            y = y + jnp.dot(m, jnp.where(lane_head == r, xs, 0.0), preferred_element_type=F32)
        pf = _expand_heads(p, 0, lane_head)
        pf_last = _expand_heads(p[CHUNK - 1:CHUNK, :], 0, lane_head)
        dtf = _expand_heads(dt, 0, lane_head)
        y = y + jnp.dot(cm, hf, preferred_element_type=F32) * jnp.exp(pf)
        y_ref[rows, :] = y
        upd = jnp.dot(bm.T, xs * (jnp.exp(pf_last - pf) * dtf), preferred_element_type=F32)
        return hf * jnp.exp(pf_last) + upd

    lax.fori_loop(0, n, fwd, jnp.zeros((SSD_STATE, SSD_GW), F32), unroll=2)

    dsk = dsk_ref[...]
    nw = nw_ref[...]

    def bwd(i, hb):
        c = n - 1 - i
        rows = pl.ds(pl.multiple_of(c * CHUNK, CHUNK), CHUNK)
        xs = xc_ref[rows, 0:SSD_GW]
        bm = xc_ref[rows, SSD_GW:SSD_GW + SSD_STATE]
        cm = xc_ref[rows, SSD_GW + SSD_STATE:]
        dt = dts_ref[rows, :]
        p = pc_ref[rows, :]
        e = p - dt * a_neg
        eb = _expand_heads(e, 3, lane_head)
        tb = _expand_heads(p[CHUNK - 1:CHUNK, :], 3, lane_head)
        dtb = _expand_heads(dt, 3, lane_head)
        y = y_ref[rows, :] + jnp.dot(cm, hb, preferred_element_type=F32) * jnp.exp(tb - eb)
        y = (y + dsk * xs) * _silu(z_ref[rows, :])
        ms = jnp.sum(y * y, axis=1, keepdims=True) * (1.0 / (SSD_HPG * HEAD_DIM))
        o_ref[rows, :] = (y * lax.rsqrt(ms + EPS) * nw).astype(o_ref.dtype)
        upd = jnp.dot(bm.T, xs * (jnp.exp(eb) * dtb), preferred_element_type=F32)
        return hb * jnp.exp(tb) + upd

    lax.fori_loop(0, n, bwd, jnp.zeros((SSD_STATE, SSD_GW), F32), unroll=2)


_SSD_XBC_W = SSD_GW + 2 * SSD_STATE
_SSD_SLAB = SSD_GROUPS * (_SSD_XBC_W + SSD_GW + LANE)


def _ssd_cols():
    gw = SSD_HPG * HEAD_DIM
    pad = lambda k: -np.ones(k, np.int64)
    xbc, zz, dtc, conv = [], [], [], []
    for g in range(SSD_GROUPS):
        xcols = np.arange(gw) + g * gw
        bcols = SSD_WIDTH + g * SSD_STATE + np.arange(SSD_STATE)
        ccols = SSD_WIDTH + SSD_GROUPS * SSD_STATE + g * SSD_STATE + np.arange(SSD_STATE)
        c_idx = np.concatenate([xcols, pad(SSD_GW - gw), bcols, ccols])
        conv.append(c_idx)
        xbc.append(np.where(c_idx >= 0, _OFF_XBC + c_idx, -1))
        zz.append(np.concatenate([_OFF_Z + xcols, pad(SSD_GW - gw)]))
        heads = _OFF_DT + g * SSD_HPG + np.arange(SSD_HPG)
        dtc.append(np.concatenate([heads, heads, pad(LANE - 2 * SSD_HPG)]))
    return np.concatenate(xbc + zz + dtc), np.concatenate(conv)


def _take_cols(a, idx):
    a = jnp.concatenate([a, jnp.zeros(a.shape[:-1] + (1,), a.dtype)], axis=-1)
    return a[..., np.where(idx >= 0, idx, a.shape[-1] - 1)]


def _ssd(p_ssd, b, s, conv_w, conv_b, dt_bias, a_log, d_skip, norm_w):
    p3 = p_ssd.reshape(b, s, _SSD_SLAB)
    _, conv_idx = _ssd_cols()
    cw = _take_cols(conv_w, conv_idx)
    cb = _take_cols(conv_b[None, :], conv_idx)
    gw = SSD_HPG * HEAD_DIM
    zpad = jnp.zeros((SSD_GROUPS, LANE - 2 * SSD_HPG), F32)
    grp = lambda v: jnp.concatenate([v[0].reshape(SSD_GROUPS, SSD_HPG), v[1].reshape(SSD_GROUPS, SSD_HPG), zpad], axis=1)
    dtb = grp(dt_bias)[:, None, :]
    alog = grp(a_log)[:, None, :]
    lanes = lambda v: jnp.pad(v.reshape(SSD_GROUPS, gw), ((0, 0), (0, SSD_GW - gw)))[:, None, :]
    dsk = lanes(jnp.repeat(d_skip, HEAD_DIM))
    nw = lanes(norm_w)
    nxb = _SSD_XBC_W // LANE
    grid_spec = dict(
        grid=(b, SSD_GROUPS),
        in_specs=[pl.BlockSpec((None, s, _SSD_XBC_W), lambda i, g: (i, 0, g)),
                  pl.BlockSpec((None, s, SSD_GW), lambda i, g: (i, 0, SSD_GROUPS * _SSD_XBC_W // SSD_GW + g)),
                  pl.BlockSpec((None, s, LANE), lambda i, g: (i, 0, SSD_GROUPS * (nxb + SSD_GW // LANE) + g)),
                  pl.BlockSpec((CONV_WIDTH, _SSD_XBC_W), lambda i, g: (0, g)),
                  pl.BlockSpec((1, _SSD_XBC_W), lambda i, g: (0, g)),
                  pl.BlockSpec((None, 1, LANE), lambda i, g: (g, 0, 0)),
                  pl.BlockSpec((None, 1, LANE), lambda i, g: (g, 0, 0)),
                  pl.BlockSpec((None, 1, SSD_GW), lambda i, g: (g, 0, 0)),
                  pl.BlockSpec((None, 1, SSD_GW), lambda i, g: (g, 0, 0))],
        out_specs=pl.BlockSpec((None, s, SSD_GW), lambda i, g: (i, 0, g)),
    )
    return pl.pallas_call(
        _ssd_kernel,
        out_shape=jax.ShapeDtypeStruct((b, s, SSD_GROUPS * SSD_GW), BF16),
        scratch_shapes=[pltpu.VMEM((s + 2 * _PAD, _SSD_XBC_W), F32), pltpu.VMEM((s, _SSD_XBC_W), F32),
                        pltpu.VMEM((s, SSD_GW), F32), pltpu.VMEM((s, LANE), F32), pltpu.VMEM((s, LANE), F32)],
        compiler_params=_cparams(("parallel", "parallel")),
        name="ssd",
        **grid_spec,
    )(p3, p3, p3, cw, cb, dtb, alog, dsk, nw)


def _scan_chunk(a, u, h, reverse):
    sub = lax.broadcasted_iota(jnp.int32, (CHUNK, 1), 0) % SUBLANE
    d = 1
    while d < SUBLANE:
        keep = (sub < SUBLANE - d) if reverse else (sub >= d)
        shift = CHUNK - d if reverse else d
        a_sh = jnp.where(keep, pltpu.roll(a, shift, 0), 1.0)
        u_sh = jnp.where(keep, pltpu.roll(u, shift, 0), 0.0)
        u = a * u_sh + u
        a = a * a_sh
        d *= 2
    n_groups = CHUNK // SUBLANE
    out = [None] * n_groups
    for g in (range(n_groups - 1, -1, -1) if reverse else range(n_groups)):
        rows = slice(g * SUBLANE, (g + 1) * SUBLANE)
        hg = u[rows] + a[rows] * h
        out[g] = hg
        h = hg[0:1] if reverse else hg[SUBLANE - 1:SUBLANE]
    return jnp.concatenate(out, axis=0), h


def _lru_kernel(x_ref, cw_ref, cb_ref, wg_ref, bg_ref, lam_ref, o_ref, xp_ref, hf_ref, ab_ref, ub_ref):
    s = x_ref.shape[0]
    n = s // CHUNK
    w = LRU_WIDTH
    xp_ref[0:_PAD, :] = jnp.zeros((_PAD, w), F32)
    xp_ref[_PAD + s:2 * _PAD + s, :] = jnp.zeros((_PAD, w), F32)

    def copy(c, carry):
        rows = pl.ds(pl.multiple_of(c * CHUNK, CHUNK), CHUNK)
        xp_ref[pl.ds(pl.multiple_of(c * CHUNK + _PAD, _PAD), CHUNK), :] = x_ref[rows, 0:w]
        return carry

    lax.fori_loop(0, n, copy, 0)
    nsp = -LRU_C * _softplus(-lam_ref[...])

    def gates(pre, xc, k):
        r = jax.nn.sigmoid(pre[:, 2 * k * w:(2 * k + 1) * w])
        i = jax.nn.sigmoid(pre[:, (2 * k + 1) * w:(2 * k + 2) * w])
        log_a = r * nsp[k:k + 1, :]
        a = jnp.exp(log_a)
        return a, jnp.sqrt(-jnp.tanh(log_a) * (a * a + 1.0)) * (i * xc)

    def fwd(c, h):
        rows = pl.ds(pl.multiple_of(c * CHUNK, CHUNK), CHUNK)
        xc = _dwconv_chunk(xp_ref, cw_ref, cb_ref, c)
        pre = jnp.dot(xc.astype(BF16), wg_ref[...], preferred_element_type=F32) + bg_ref[...]
        a_b, u_b = gates(pre, xc, 1)
        ab_ref[rows, :] = a_b
        ub_ref[rows, :] = u_b
        a_f, u_f = gates(pre, xc, 0)
        hc, h = _scan_chunk(a_f, u_f, h, False)
        hf_ref[rows, :] = hc
        return h

    lax.fori_loop(0, n, fwd, jnp.zeros((1, w), F32))

    def bwd(i, h):
        c = n - 1 - i
        rows = pl.ds(pl.multiple_of(c * CHUNK, CHUNK), CHUNK)
        hc, h = _scan_chunk(ab_ref[rows, :], ub_ref[rows, :], h, True)
        gate = x_ref[rows, w:2 * w]
        o_ref[rows, :] = ((hf_ref[rows, :] + hc) * jax.nn.gelu(gate)).astype(o_ref.dtype)
        return h

    lax.fori_loop(0, n, bwd, jnp.zeros((1, w), F32))


def _block_diag(wblk):
    eye = jnp.eye(LRU_BLOCKS, dtype=wblk.dtype)
    return jnp.einsum('hij,hk->hikj', wblk, eye).reshape(LRU_WIDTH, LRU_WIDTH)


def _lru(p_lru, b, s, conv_w, conv_b, lam, wa, ba, wx, bx):
    p3 = p_lru.reshape(b, s, 2 * LRU_WIDTH)
    wg = jnp.concatenate([_block_diag(wa[0]), _block_diag(wx[0]), _block_diag(wa[1]), _block_diag(wx[1])], axis=1).astype(BF16)
    bg = jnp.concatenate([ba[0], bx[0], ba[1], bx[1]])[None, :]
    w = LRU_WIDTH
    const = lambda shape: pl.BlockSpec(shape, lambda i: (0,) * len(shape))
    return pl.pallas_call(
        _lru_kernel,
        grid=(b,),
        in_specs=[pl.BlockSpec((None, s, 2 * w), lambda i: (i, 0, 0)),
                  const((CONV_WIDTH, w)), const((1, w)), const((w, 4 * w)), const((1, 4 * w)), const((2, w))],
        out_specs=pl.BlockSpec((None, s, w), lambda i: (i, 0, 0)),
        out_shape=jax.ShapeDtypeStruct((b, s, w), BF16),
        scratch_shapes=[pltpu.VMEM((s + 2 * _PAD, w), F32), pltpu.VMEM((s, w), F32),
                        pltpu.VMEM((s, w), F32), pltpu.VMEM((s, w), F32)],
        compiler_params=_cparams(("parallel",)),
        name="rglru",
    )(p3, conv_w, conv_b[None, :], wg, bg, lam)


def _outproj_weights(w_out):
    gw = SSD_HPG * HEAD_DIM
    idx = np.concatenate([np.concatenate([RET_WIDTH + g * gw + np.arange(gw), -np.ones(SSD_GW - gw, np.int64)])
                          for g in range(SSD_GROUPS)])
    w_ssd = _take_cols(w_out.T, idx).T
    return (w_out[:RET_WIDTH].astype(BF16), w_ssd.astype(BF16), w_out[RET_WIDTH + SSD_WIDTH:].astype(BF16))


def _kv_kernel(m_ref, g_ref, w_ref, o_ref):
    mn = _rms(m_ref[...], g_ref[...]).astype(BF16)
    o_ref[...] = jnp.dot(mn, w_ref[...], preferred_element_type=F32).astype(o_ref.dtype)


def _kv_proj(mem2, g, wkv):
    t, d = mem2.shape
    n = wkv.shape[1]
    tm = 256
    return pl.pallas_call(
        _kv_kernel,
        grid=(t // tm,),
        in_specs=[pl.BlockSpec((tm, d), lambda i: (i, 0)), pl.BlockSpec((1, d), lambda i: (0, 0)),
                  pl.BlockSpec((d, n), lambda i: (0, 0))],
        out_specs=pl.BlockSpec((tm, n), lambda i: (i, 0)),
        out_shape=jax.ShapeDtypeStruct((t, n), BF16),
        compiler_params=_cparams(("parallel",)),
        name="kvproj",
    )(mem2, g.reshape(1, d), wkv)


def _xattn_tile(h, g, wq_ref, k_ref, v_ref, wo_ref):
    xn = _rms(h, g).astype(BF16)
    q = jnp.dot(xn, wq_ref[...], preferred_element_type=F32).astype(BF16)
    outs = []
    for hd in range(CA_HEADS):
        cols = slice(hd * CA_HEAD_DIM, (hd + 1) * CA_HEAD_DIM)
        sc = lax.dot_general(q[:, cols], k_ref[:, cols], (((1,), (1,)), ((), ())), preferred_element_type=F32)
        sc = sc * (CA_HEAD_DIM ** -0.5)
        e = jnp.exp(sc - jnp.max(sc, axis=-1, keepdims=True))
        p = e / jnp.sum(e, axis=-1, keepdims=True)
        outs.append(jnp.dot(p.astype(BF16), v_ref[:, cols], preferred_element_type=F32).astype(BF16))
    o = jnp.concatenate(outs, axis=-1)
    return h + jnp.dot(o, wo_ref[...], preferred_element_type=F32)


_ROUTER_ROWS = SUBLANE + N_EXPERTS


def _first_argmax(v, row):
    m = jnp.max(v, axis=0, keepdims=True)
    return m, jnp.min(jnp.where(v == m, row, SUBLANE), axis=0, keepdims=True)


def _router_tile(h, g, w_ref, b_ref):
    xn = _rms(h, g).astype(BF16)
    lg = lax.dot_general(w_ref[...], xn, (((1,), (1,)), ((), ())), preferred_element_type=F32) + b_ref[...]
    tm = lg.shape[1]
    row = lax.broadcasted_iota(jnp.int32, (SUBLANE, tm), 0)
    gl = jnp.where(row < N_GROUPS, lg[0:SUBLANE], -jnp.inf)
    ge = jnp.exp(gl - jnp.max(gl, axis=0, keepdims=True))
    gp_all = ge / jnp.sum(ge, axis=0, keepdims=True)
    gp, gi = _first_argmax(gp_all, row)
    el = lg[SUBLANE:2 * SUBLANE]
    for g in range(1, N_GROUPS):
        el = jnp.where(gi == g, lg[SUBLANE * (g + 1):SUBLANE * (g + 2)], el)
    ee = jnp.exp(el - jnp.max(el, axis=0, keepdims=True))
    ep = ee / jnp.sum(ee, axis=0, keepdims=True)
    v1, i1 = _first_argmax(ep, row)
    v2, i2 = _first_argmax(jnp.where(row == i1, -1.0, ep), row)
    den = v1 + v2
    ids = jnp.where(row == 0, gi * EXPERTS_PER_GROUP + i1, jnp.where(row == 1, gi * EXPERTS_PER_GROUP + i2, 0))
    gates = jnp.where(row == 0, gp * v1 / den, jnp.where(row == 1, gp * v2 / den, 0.0))
    return ids, gates


def _post_kernel(h_ref, a_ref, b_ref, c_ref, wa_ref, wb_ref, wc_ref, gca_ref, wq_ref, k_ref, v_ref, wo_ref,
                 gmoe_ref, wr_ref, br_ref, o_ref, ids_ref, gates_ref):
    h = h_ref[...]
    h = h + jnp.dot(a_ref[...], wa_ref[...], preferred_element_type=F32)
    h = h + jnp.dot(b_ref[...], wb_ref[...], preferred_element_type=F32)
    h = h + jnp.dot(c_ref[...], wc_ref[...], preferred_element_type=F32)
    h = _xattn_tile(h, gca_ref[...], wq_ref, k_ref, v_ref, wo_ref)
    o_ref[...] = h
    ids, gates = _router_tile(h, gmoe_ref[...], wr_ref, br_ref)
    ids_ref[...] = ids
    gates_ref[...] = gates


def _post_mixer(h2, b, s, o_ret, o_ssd, o_lru, w_ret, w_ssd, w_lru, g_ca, wq, kv, wo, g_moe, wg, bg, we, be):
    t, d = h2.shape
    m = kv.shape[0] // b
    ts = ROW_TILE
    nts = s // ts
    zrow = jnp.zeros((SUBLANE - N_GROUPS, d), F32)
    wr = jnp.concatenate([wg.T, zrow, we.T], axis=0).astype(BF16)
    br = jnp.concatenate([bg, jnp.zeros((SUBLANE - N_GROUPS,), F32), be])[:, None]
    row = lambda n: pl.BlockSpec((ts, n), lambda i, j: (i * nts + j, 0))
    const = lambda r, c: pl.BlockSpec((r, c), lambda i, j: (0, 0))
    ka, kb, kc = o_ret.shape[1], o_ssd.shape[1], o_lru.shape[1]
    lanes = pl.BlockSpec((SUBLANE, ts), lambda i, j: (0, i * nts + j))
    return pl.pallas_call(
        _post_kernel,
        grid=(b, nts),
        in_specs=[row(d), row(ka), row(kb), row(kc), const(ka, d), const(kb, d), const(kc, d),
                  const(1, d), const(d, d),
                  pl.BlockSpec((m, d), lambda i, j: (i, 0)),
                  pl.BlockSpec((m, d), lambda i, j: (i, 1)),
                  const(d, d), const(1, d), const(_ROUTER_ROWS, d), const(_ROUTER_ROWS, 1)],
        out_specs=[row(d), lanes, lanes],
        out_shape=[jax.ShapeDtypeStruct((t, d), F32), jax.ShapeDtypeStruct((SUBLANE, t), jnp.int32),
                   jax.ShapeDtypeStruct((SUBLANE, t), F32)],
        compiler_params=_cparams(("parallel", "parallel")),
        name="post_mixer",
    )(h2, o_ret, o_ssd, o_lru, w_ret, w_ssd, w_lru, g_ca.reshape(1, d), wq, kv, kv, wo,
      g_moe.reshape(1, d), wr, br)


MOE_BLK = 256


_ASSIGN_BITS = 17


def _dispatch_plan(ids, t):
    n_assign = t * TOP_K
    assert n_assign <= 1 << _ASSIGN_BITS
    e = ids[:TOP_K].reshape(n_assign)
    key =jnp.left_shift(e, _ASSIGN_BITS) | jnp.arange(n_assign, dtype=jnp.int32)
    order = jnp.sort(key) & ((1 << _ASSIGN_BITS) - 1)
    order = jnp.concatenate([order, jnp.zeros((MOE_BLK,), jnp.int32)])
    experts = jnp.arange(N_EXPERTS, dtype=jnp.int32)
    counts = jnp.sum((e[None, :] == experts[:, None]).astype(jnp.int32), axis=1)
    start = jnp.cumsum(counts) - counts
    nb = (counts + MOE_BLK - 1) // MOE_BLK
    blk_end = jnp.cumsum(nb)
    n_blocks = n_assign // MOE_BLK + N_EXPERTS
    j = jnp.arange(n_blocks, dtype=jnp.int32)
    be = jnp.minimum(jnp.sum((blk_end[None, :] <= j[:, None]).astype(jnp.int32), axis=1), N_EXPERTS - 1)
    onehot = (be[:, None] == experts[None, :]).astype(jnp.int32)
    pick = lambda v: jnp.sum(onehot * v[None, :], axis=1)
    r = j - pick(blk_end - nb)
    row_start = jnp.clip(pick(start) + r * MOE_BLK, 0, n_assign)
    n_rows = jnp.clip(pick(counts) - r * MOE_BLK, 0, MOE_BLK)
    return order, be, row_start, n_rows, blk_end[-1:].astype(jnp.int32)


def _expert_kernel(be_ref, rs_ref, nr_ref, nu_ref, ord_ref, h_hbm, g_ref, w1_ref, w3_ref, w2_ref, y_hbm,
                   xbuf, ybuf, wb1, wb3, wb2, gsem, ssem, *, n_tokens):
    n_assign = TOP_K * n_tokens
    i = pl.program_id(0)
    n_used = nu_ref[0]
    slot = i % 2

    def gather(blk, s):
        base = rs_ref[blk]

        def body(r, carry):
            a = ord_ref[base + r]
            tok = jnp.where(a >= n_tokens, a - n_tokens, a)
            pltpu.make_async_copy(h_hbm.at[pl.ds(tok, 1), :], xbuf.at[s, pl.ds(r, 1), :], gsem.at[s]).start()
            return carry
        lax.fori_loop(0, MOE_BLK, body, 0, unroll=8)

    def scatter(blk, s):
        base = rs_ref[blk]
        cnt = nr_ref[blk]
        dump = n_assign + s * MOE_BLK

        def body(r, carry):
            dest = jnp.where(r < cnt, ord_ref[base + r], dump + r)
            pltpu.make_async_copy(ybuf.at[s, pl.ds(r, 1), :], y_hbm.at[pl.ds(dest, 1), :], ssem.at[s]).start()
            return carry
        lax.fori_loop(0, MOE_BLK, body, 0, unroll=8)

    def wait_gather(s):
        pltpu.make_async_copy(h_hbm.at[pl.ds(0, MOE_BLK), :], xbuf.at[s], gsem.at[s]).wait()

    def wait_scatter(s):
        pltpu.make_async_copy(ybuf.at[s], y_hbm.at[pl.ds(0, MOE_BLK), :], ssem.at[s]).wait()

    @pl.when(i == 0)
    def _():
        ybuf[0] = jnp.zeros((MOE_BLK, ybuf.shape[2]), F32)
        for s in range(2):
            fill = pltpu.make_async_copy(ybuf.at[0], y_hbm.at[pl.ds(n_assign + s * MOE_BLK, MOE_BLK), :], ssem.at[0])
            fill.start()
            fill.wait()
        gather(0, 0)

    @pl.when(i + 1 < n_used)
    def _():
        gather(i + 1, 1 - slot)

    @pl.when(i < n_used)
    def _():
        @pl.when((i == 0) | (be_ref[i] != be_ref[jnp.maximum(i - 1, 0)]))
        def _():
            wb1[...] = w1_ref[...].astype(BF16)
            wb3[...] = w3_ref[...].astype(BF16)
            wb2[...] = w2_ref[...].astype(BF16)

        wait_gather(slot)

        @pl.when(i >= 2)
        def _():
            wait_scatter(slot)

        xn = _rms(xbuf[slot], g_ref[...]).astype(BF16)
        h1 = jnp.dot(xn, wb1[...], preferred_element_type=F32)
        h3 = jnp.dot(xn, wb3[...], preferred_element_type=F32)
        hid = (_silu(h1) * h3).astype(BF16)
        ybuf[slot] = jnp.dot(hid, wb2[...], preferred_element_type=F32)
        scatter(i, slot)

        @pl.when(i == n_used - 1)
        def _():
            wait_scatter(slot)

            @pl.when(i >= 1)
            def _():
                wait_scatter(1 - slot)


def _experts(h2, g, order, block_e, row_start, n_rows, n_used, w1, w3, w2, layer):
    t, d = h2.shape
    n_assign = t * TOP_K
    n_blocks = block_e.shape[0]
    de = w1.shape[3]
    wmap = lambda i, be, rs, nr, nu, od: (layer, be[i], 0, 0)
    grid_spec = pltpu.PrefetchScalarGridSpec(
        num_scalar_prefetch=5,
        grid=(n_blocks,),
        in_specs=[pl.BlockSpec(memory_space=pl.ANY),
                  pl.BlockSpec((1, d), lambda i, *_: (0, 0)),
                  pl.BlockSpec((None, None, d, de), wmap),
                  pl.BlockSpec((None, None, d, de), wmap),
                  pl.BlockSpec((None, None, de, d), wmap)],
        out_specs=pl.BlockSpec(memory_space=pl.ANY),
        scratch_shapes=[pltpu.VMEM((2, MOE_BLK, d), F32), pltpu.VMEM((2, MOE_BLK, d), F32),
                        pltpu.VMEM((d, de), BF16), pltpu.VMEM((d, de), BF16), pltpu.VMEM((de, d), BF16),
                        pltpu.SemaphoreType.DMA((2,)), pltpu.SemaphoreType.DMA((2,))],
    )
    return pl.pallas_call(
        functools.partial(_expert_kernel, n_tokens=t),
        grid_spec=grid_spec,
        out_shape=jax.ShapeDtypeStruct((n_assign + 2 * MOE_BLK, d), F32),
        compiler_params=_cparams(("arbitrary",)),
        name="experts",
    )(block_e, row_start, n_rows, n_used, order, h2, g.reshape(1, d), w1, w3, w2)


def _combine_kernel(h_ref, y0_ref, y1_ref, gt_ref, g_ref, o_ref, *, final):
    gt = gt_ref[...]
    out = h_ref[...] + y0_ref[...] * gt[:, 0:1] + y1_ref[...] * gt[:, 1:2]
    if final:
        out = _rms(out, g_ref[...])
    o_ref[...] = out


def _combine(h2, y, gates, g_final, final):
    t, d = h2.shape
    tm = ROW_TILE
    nt = t // tm
    return pl.pallas_call(
        functools.partial(_combine_kernel, final=final),
        grid=(nt,),
        in_specs=[pl.BlockSpec((tm, d), lambda i: (i, 0)),
                  pl.BlockSpec((tm, d), lambda i: (i, 0)),
                  pl.BlockSpec((tm, d), lambda i: (nt + i, 0)),
                  pl.BlockSpec((tm, TOP_K), lambda i: (i, 0)),
                  pl.BlockSpec((1, d), lambda i: (0, 0))],
        out_specs=pl.BlockSpec((tm, d), lambda i: (i, 0)),
        out_shape=jax.ShapeDtypeStruct((t, d), F32),
        compiler_params=_cparams(("parallel",)),
        name="combine",
    )(h2, y, y, gates[:TOP_K].T, g_final.reshape(1, d))


def _moe(h2, ids, gates, g, w1, w3, w2, layer, g_final, final):
    t = h2.shape[0]
    order, block_e, row_start, n_rows, n_used = _dispatch_plan(ids, t)
    y = _experts(h2, g, order, block_e, row_start, n_rows, n_used, w1, w3, w2, layer)
    return _combine(h2, y, gates, g_final, final)


def kernel(x, mem, g_mix, w_in, w_out, ssd_conv_w, ssd_conv_b, ssd_dt_bias, ssd_a_log, ssd_d, ssd_norm, lru_conv_w, lru_conv_b, lru_lambda, lru_wa, lru_ba, lru_wx, lru_bx, g_ca, g_mem, ca_wq, ca_wkv, ca_wo, g_moe, moe_wg, moe_bg, moe_we, moe_be, moe_w1, moe_w3, moe_w2, g_final):
    b, s, d = x.shape
    depth = g_mix.shape[0]
    h = x.reshape(b * s, d)
    mem2 = mem.reshape(b * mem.shape[1], d)
    ret_perm = _ret_perm()
    ssd_cols = _ssd_cols()[0]
    for l in range(depth):
        w_ret = w_in[l][:, ret_perm].astype(BF16)
        w_ssd = _take_cols(w_in[l], ssd_cols).astype(BF16)
        w_lru = w_in[l][:, _OFF_XR:].astype(BF16)
        p_ret, p_ssd, p_lru = _inproj(h, g_mix[l], w_ret, w_ssd, w_lru)
        o_ret = _retention(p_ret, b, s)
        o_ssd = _ssd(p_ssd, b, s, ssd_conv_w[l], ssd_conv_b[l], ssd_dt_bias[l], ssd_a_log[l], ssd_d[l], ssd_norm[l])
        o_lru = _lru(p_lru, b, s, lru_conv_w[l], lru_conv_b[l], lru_lambda[l], lru_wa[l], lru_ba[l], lru_wx[l], lru_bx[l])
        kv = _kv_proj(mem2, g_mem[l], ca_wkv[l].astype(BF16))
        h, ids, gates = _post_mixer(h, b, s, o_ret.reshape(b * s, -1), o_ssd.reshape(b * s, -1),
                                    o_lru.reshape(b * s, -1), *_outproj_weights(w_out[l]),
                                    g_ca[l], ca_wq[l].astype(BF16), kv, ca_wo[l].astype(BF16),
                                    g_moe[l], moe_wg[l], moe_bg[l], moe_we[l], moe_be[l])
        h = _moe(h, ids, gates, g_moe[l], moe_w1, moe_w3, moe_w2, l, g_final, l == depth - 1)
    return h.reshape(b, s, d)
```

```python
import functools
import math

import jax
import jax.numpy as jnp
import numpy as np
from jax import lax
from jax.experimental import pallas as pl
from jax.experimental.pallas import tpu as pltpu

D_MODEL = 1024
HEAD_DIM = 64
RET_WIDTH = 384
SSD_WIDTH = 384
LRU_WIDTH = 256
RET_HEADS = 6
SSD_HEADS = 6
SSD_GROUPS = 2
SSD_HPG = 3
SSD_STATE = 128
SSD_CONV_DIM = SSD_WIDTH + 2 * SSD_GROUPS * SSD_STATE
LRU_BLOCKS = 4
LRU_C = 8.0
CONV_WIDTH = 4
ROPE_BASE = 10000.0
CA_HEADS = 4
CA_HEAD_DIM = 256
N_GROUPS = 4
EXPERTS_PER_GROUP = 8
N_EXPERTS = 32
TOP_K = 2
D_EXPERT = 512
EPS = 1e-6

_OFF_Q, _OFF_K, _OFF_V, _OFF_G = 0, 384, 768, 1152
_OFF_Z = 1536
_OFF_XBC = 1920
_OFF_DT = 2816
_OFF_XR = 2822
_OFF_GATE = 3078
D_IN = 3334

LANE = 128
SUBLANE = 8
VMEM_LIMIT = 56 * 1024 * 1024

CHUNK = 128
ROW_TILE = 512
SSD_GW = 256

F32 = jnp.float32
BF16 = jnp.bfloat16


def _cparams(sem):
    return pltpu.CompilerParams(dimension_semantics=sem, vmem_limit_bytes=VMEM_LIMIT)


def _rms(x, g):
    return x * lax.rsqrt(jnp.mean(x * x, axis=-1, keepdims=True) + EPS) * g


def _inproj_kernel(x_ref, g_ref, wr_ref, ws_ref, wl_ref, pr_ref, ps_ref, pq_ref):
    xn = _rms(x_ref[...], g_ref[...]).astype(BF16)
    pr_ref[...] = jnp.dot(xn, wr_ref[...], preferred_element_type=F32)
    ps_ref[...] = jnp.dot(xn, ws_ref[...], preferred_element_type=F32)
    pq_ref[...] = jnp.dot(xn, wl_ref[...], preferred_element_type=F32)


def _inproj(x2, g, w_ret, w_ssd, w_lru):
    t, d = x2.shape
    tm = 256
    n_r, n_s, n_l = w_ret.shape[1], w_ssd.shape[1], w_lru.shape[1]
    full = lambda n: pl.BlockSpec((d, n), lambda i: (0, 0))
    return pl.pallas_call(
        _inproj_kernel,
        grid=(t // tm,),
        in_specs=[pl.BlockSpec((tm, d), lambda i: (i, 0)),
                  pl.BlockSpec((1, d), lambda i: (0, 0)),
                  full(n_r), full(n_s), full(n_l)],
        out_specs=[pl.BlockSpec((tm, n_r), lambda i: (i, 0)),
                   pl.BlockSpec((tm, n_s), lambda i: (i, 0)),
                   pl.BlockSpec((tm, n_l), lambda i: (i, 0))],
        out_shape=[jax.ShapeDtypeStruct((t, n_r), F32),
                   jax.ShapeDtypeStruct((t, n_s), F32),
                   jax.ShapeDtypeStruct((t, n_l), F32)],
        compiler_params=_cparams(("parallel",)),
        name="inproj",
    )(x2, g.reshape(1, d), w_ret, w_ssd, w_lru)


def _ret_kernel(q_ref, k_ref, v_ref, g_ref, cos_ref, sin_ref, d2_ref, dec_ref, gc_ref,
                o_ref, qs_ref, ks_ref, acc_ref):
    s = q_ref.shape[0]
    n = s // CHUNK
    lane = lax.broadcasted_iota(jnp.int32, (1, LANE), 1)
    mq0 = ((lane // 32) % 2 == 0).astype(F32)
    mq1 = 1.0 - mq0
    mv0 = (lane < HEAD_DIM).astype(F32)
    mv1 = 1.0 - mv0
    krow = lax.broadcasted_iota(jnp.int32, (LANE, LANE), 0)
    vcol = lax.broadcasted_iota(jnp.int32, (LANE, LANE), 1)
    bd = (((krow // 32) % 2) == (vcol // HEAD_DIM)).astype(F32)
    zeta_f = dec_ref[0]
    zeta_b = dec_ref[1]
    xi_f = dec_ref[2]
    xi_b = dec_ref[3]
    gc = gc_ref[...]
    d2 = d2_ref[...]

    def rot(x, c):
        rows = pl.ds(c * CHUNK, CHUNK)
        return x * cos_ref[rows, :] + pltpu.roll(x, 64, 1) * sin_ref[rows, :]

    def fwd(c, st):
        rows = pl.ds(pl.multiple_of(c * CHUNK, CHUNK), CHUNK)
        qc = rot(q_ref[rows, :], c)
        kc = rot(k_ref[rows, :], c) * (HEAD_DIM ** -0.5)
        vc = v_ref[rows, :]
        qs_ref[rows, :] = qc
        ks_ref[rows, :] = kc
        q2 = jnp.concatenate([qc * mq0, qc * mq1], axis=0)
        sc = lax.dot_general(q2, kc, (((1,), (1,)), ((), ())), preferred_element_type=F32) * d2
        o2 = jnp.dot(sc, vc, preferred_element_type=F32)
        o = o2[:CHUNK] * mv0 + o2[CHUNK:] * mv1
        o = o + jnp.dot(qc, st, preferred_element_type=F32) * xi_f
        acc_ref[rows, :] = o
        upd = jnp.dot(kc.T, vc * zeta_f, preferred_element_type=F32) * bd
        return st * gc + upd

    lax.fori_loop(0, n, fwd, jnp.zeros((LANE, LANE), F32), unroll=4)

    def bwd(i, st):
        c = n - 1 - i
        rows = pl.ds(pl.multiple_of(c * CHUNK, CHUNK), CHUNK)
        qc = qs_ref[rows, :]
        kc = ks_ref[rows, :]
        vc = v_ref[rows, :]
        o = acc_ref[rows, :] + jnp.dot(qc, st, preferred_element_type=F32) * xi_b
        oo = o * o
        ss0 = jnp.sum(oo * mv0, axis=1, keepdims=True)
        ss1 = jnp.sum(oo * mv1, axis=1, keepdims=True)
        scale = lax.rsqrt(ss0 * (1.0 / HEAD_DIM) + EPS) * mv0 + lax.rsqrt(ss1 * (1.0 / HEAD_DIM) + EPS) * mv1
        gate = g_ref[rows, :]
        gate = gate * jax.nn.sigmoid(gate)
        o_ref[rows, :] = (gate * (o * scale)).astype(o_ref.dtype)
        upd = jnp.dot(kc.T, vc * zeta_b, preferred_element_type=F32) * bd
        return st * gc + upd

    lax.fori_loop(0, n, bwd, jnp.zeros((LANE, LANE), F32), unroll=4)


def _ret_constants(s):
    half = HEAD_DIM // 2
    inv = ROPE_BASE ** (-jnp.arange(half, dtype=F32) / half)
    ang = jnp.arange(s, dtype=F32)[:, None] * inv[None, :]
    cos = jnp.cos(ang)
    sin = jnp.sin(ang)
    cos2 = jnp.concatenate([cos, cos, cos, cos], axis=1)
    sin2 = jnp.concatenate([-sin, -sin, sin, sin], axis=1)
    log_gamma = jnp.log1p(-jnp.exp2(-5.0 - jnp.arange(RET_HEADS, dtype=F32)))
    lg_pair = log_gamma.reshape(RET_HEADS // 2, 2)
    idx = jnp.arange(CHUNK, dtype=F32)
    adiff = jnp.abs(idx[:, None] - idx[None, :])
    d2 = jnp.exp(adiff[None, None] * lg_pair[:, :, None, None])
    d2 = d2.reshape(RET_HEADS // 2, 2 * CHUNK, CHUNK)
    lg_lane = jnp.repeat(lg_pair, HEAD_DIM, axis=1)
    pos = idx[None, :, None]
    lgl = lg_lane[:, None, :]
    dec = jnp.stack([jnp.exp((CHUNK - 1 - pos) * lgl), jnp.exp(pos * lgl),
                     jnp.exp((pos + 1) * lgl), jnp.exp((CHUNK - pos) * lgl)], axis=1)
    gc = jnp.exp(CHUNK * lg_lane)[:, None, :]
    return cos2, sin2, d2, dec, gc


def _retention(p_ret, b, s):
    p3 = p_ret.reshape(b, s, 4 * RET_WIDTH)
    cos2, sin2, d2, dec, gc = _ret_constants(s)
    npair = RET_HEADS // 2
    col = lambda off: pl.BlockSpec((None, s, LANE), lambda i, p: (i, 0, off + p))
    return pl.pallas_call(
        _ret_kernel,
        grid=(b, npair),
        in_specs=[col(0), col(npair), col(2 * npair), col(3 * npair),
                  pl.BlockSpec((s, LANE), lambda i, p: (0, 0)),
                  pl.BlockSpec((s, LANE), lambda i, p: (0, 0)),
                  pl.BlockSpec((None, 2 * CHUNK, CHUNK), lambda i, p: (p, 0, 0)),
                  pl.BlockSpec((None, 4, CHUNK, LANE), lambda i, p: (p, 0, 0, 0)),
                  pl.BlockSpec((None, 1, LANE), lambda i, p: (p, 0, 0))],
        out_specs=pl.BlockSpec((None, s, LANE), lambda i, p: (i, 0, p)),
        out_shape=jax.ShapeDtypeStruct((b, s, RET_WIDTH), BF16),
        scratch_shapes=[pltpu.VMEM((s, LANE), F32), pltpu.VMEM((s, LANE), F32), pltpu.VMEM((s, LANE), F32)],
        compiler_params=_cparams(("parallel", "parallel")),
        name="retention",
    )(p3, p3, p3, p3, cos2, sin2, d2, dec, gc)


def _ret_perm():
    pair = np.concatenate([np.arange(0, 32), np.arange(64, 96), np.arange(32, 64), np.arange(96, 128)])
    qperm = np.concatenate([128 * p + pair for p in range(RET_HEADS // 2)])
    return np.concatenate([_OFF_Q + qperm, _OFF_K + qperm, _OFF_V + np.arange(RET_WIDTH), _OFF_G + np.arange(RET_WIDTH)])


_PAD = 8


def _softplus(x):
    return jnp.maximum(x, 0.0) + jnp.log1p(jnp.exp(-jnp.abs(x)))


def _silu(x):
    return x * jax.nn.sigmoid(x)


def _expand_heads(cols, base, lane_head):
    acc = jnp.zeros((cols.shape[0], SSD_GW), F32)
    for r in range(SSD_HPG):
        acc = jnp.where(lane_head == r, cols[:, base + r:base + r + 1], acc)
    return acc


def _dwconv_chunk(xp_ref, cw_ref, cb_ref, c):
    rows = CHUNK + 2 * _PAD
    win = xp_ref[pl.ds(pl.multiple_of(c * CHUNK, CHUNK), rows), :]
    acc = cb_ref[...]
    for t in range(CONV_WIDTH):
        shift = (CONV_WIDTH // 2 - t) % rows
        tap = win if shift == 0 else pltpu.roll(win, shift, 0)
        acc = acc + tap[_PAD:_PAD + CHUNK, :] * cw_ref[t:t + 1, :]
    return acc


def _ssd_kernel(xbc_ref, z_ref, dt_ref, cw_ref, cb_ref, dtb_ref, alog_ref, dsk_ref, nw_ref, o_ref,
                xp_ref, xc_ref, y_ref, pc_ref, dts_ref):
    s = z_ref.shape[0]
    n = s // CHUNK
    w = xbc_ref.shape[1]
    xp_ref[0:_PAD, :] = jnp.zeros((_PAD, w), F32)
    xp_ref[_PAD + s:2 * _PAD + s, :] = jnp.zeros((_PAD, w), F32)

    def copy(c, carry):
        rows = pl.ds(pl.multiple_of(c * CHUNK, CHUNK), CHUNK)
        xp_ref[pl.ds(pl.multiple_of(c * CHUNK + _PAD, _PAD), CHUNK), :] = xbc_ref[rows, :]
        return carry

    lax.fori_loop(0, n, copy, 0)

    lane_head = lax.broadcasted_iota(jnp.int32, (1, SSD_GW), 1) // HEAD_DIM
    ii = lax.broadcasted_iota(jnp.int32, (CHUNK, CHUNK), 0)
    jj = lax.broadcasted_iota(jnp.int32, (CHUNK, CHUNK), 1)
    lower = ii >= jj
    tri = lower.astype(F32)
    a_neg = -jnp.exp(alog_ref[...])
    dt_bias = dtb_ref[...]

    def fwd(c, hf):
        rows = pl.ds(pl.multiple_of(c * CHUNK, CHUNK), CHUNK)
        xc = _silu(_dwconv_chunk(xp_ref, cw_ref, cb_ref, c))
        xc_ref[rows, :] = xc
        xs = xc[:, 0:SSD_GW]
        bm = xc[:, SSD_GW:SSD_GW + SSD_STATE]
        cm = xc[:, SSD_GW + SSD_STATE:]
        dt = _softplus(dt_ref[rows, :] + dt_bias)
        dts_ref[rows, :] = dt
        la = dt * a_neg
        p = jnp.dot(tri, la, precision=lax.Precision.HIGHEST, preferred_element_type=F32)
        pc_ref[rows, :] = p
        e = p - la
        pt = p.T
        et = e.T
        dtt = dt.T
        g = lax.dot_general(cm, bm, (((1,), (1,)), ((), ())), preferred_element_type=F32)
        y = jnp.zeros((CHUNK, SSD_GW), F32)
        for r in range(SSD_HPG):
            arg = jnp.where(lower, p[:, r:r + 1] - pt[r:r + 1, :], et[3 + r:4 + r, :] - e[:, 3 + r:4 + r])
            dtj = jnp.where(lower, dtt[r:r + 1, :], dtt[3 + r:4 + r, :])
            m = g * (jnp.exp(arg) * dtj)
            y = y + jnp.dot(m, jnp.where(lane_head == r, xs, 0.0), preferred_element_type=F32)
        pf = _expand_heads(p, 0, lane_head)
        pf_last = _expand_heads(p[CHUNK - 1:CHUNK, :], 0, lane_head)
        dtf = _expand_heads(dt, 0, lane_head)
        y = y + jnp.dot(cm, hf, preferred_element_type=F32) * jnp.exp(pf)
        y_ref[rows, :] = y
        upd = jnp.dot(bm.T, xs * (jnp.exp(pf_last - pf) * dtf), preferred_element_type=F32)
        return hf * jnp.exp(pf_last) + upd

    lax.fori_loop(0, n, fwd, jnp.zeros((SSD_STATE, SSD_GW), F32), unroll=2)

    dsk = dsk_ref[...]
    nw = nw_ref[...]

    def bwd(i, hb):
        c = n - 1 - i
        rows = pl.ds(pl.multiple_of(c * CHUNK, CHUNK), CHUNK)
        xs = xc_ref[rows, 0:SSD_GW]
        bm = xc_ref[rows, SSD_GW:SSD_GW + SSD_STATE]
        cm = xc_ref[rows, SSD_GW + SSD_STATE:]
        dt = dts_ref[rows, :]
        p = pc_ref[rows, :]
        e = p - dt * a_neg
        eb = _expand_heads(e, 3, lane_head)
        tb = _expand_heads(p[CHUNK - 1:CHUNK, :], 3, lane_head)
        dtb = _expand_heads(dt, 3, lane_head)
        y = y_ref[rows, :] + jnp.dot(cm, hb, preferred_element_type=F32) * jnp.exp(tb - eb)
        y = (y + dsk * xs) * _silu(z_ref[rows, :])
        ms = jnp.sum(y * y, axis=1, keepdims=True) * (1.0 / (SSD_HPG * HEAD_DIM))
        o_ref[rows, :] = (y * lax.rsqrt(ms + EPS) * nw).astype(o_ref.dtype)
        upd = jnp.dot(bm.T, xs * (jnp.exp(eb) * dtb), preferred_element_type=F32)
        return hb * jnp.exp(tb) + upd

    lax.fori_loop(0, n, bwd, jnp.zeros((SSD_STATE, SSD_GW), F32), unroll=2)


_SSD_XBC_W = SSD_GW + 2 * SSD_STATE
_SSD_SLAB = SSD_GROUPS * (_SSD_XBC_W + SSD_GW + LANE)


def _ssd_cols():
    gw = SSD_HPG * HEAD_DIM
    pad = lambda k: -np.ones(k, np.int64)
    xbc, zz, dtc, conv = [], [], [], []
    for g in range(SSD_GROUPS):
        xcols = np.arange(gw) + g * gw
        bcols = SSD_WIDTH + g * SSD_STATE + np.arange(SSD_STATE)
        ccols = SSD_WIDTH + SSD_GROUPS * SSD_STATE + g * SSD_STATE + np.arange(SSD_STATE)
        c_idx = np.concatenate([xcols, pad(SSD_GW - gw), bcols, ccols])
        conv.append(c_idx)
        xbc.append(np.where(c_idx >= 0, _OFF_XBC + c_idx, -1))
        zz.append(np.concatenate([_OFF_Z + xcols, pad(SSD_GW - gw)]))
        heads = _OFF_DT + g * SSD_HPG + np.arange(SSD_HPG)
        dtc.append(np.concatenate([heads, heads, pad(LANE - 2 * SSD_HPG)]))
    return np.concatenate(xbc + zz + dtc), np.concatenate(conv)


def _take_cols(a, idx):
    a = jnp.concatenate([a, jnp.zeros(a.shape[:-1] + (1,), a.dtype)], axis=-1)
    return a[..., np.where(idx >= 0, idx, a.shape[-1] - 1)]


def _ssd(p_ssd, b, s, conv_w, conv_b, dt_bias, a_log, d_skip, norm_w):
    p3 = p_ssd.reshape(b, s, _SSD_SLAB)
    _, conv_idx = _ssd_cols()
    cw = _take_cols(conv_w, conv_idx)
    cb = _take_cols(conv_b[None, :], conv_idx)
    gw = SSD_HPG * HEAD_DIM
    zpad = jnp.zeros((SSD_GROUPS, LANE - 2 * SSD_HPG), F32)
    grp = lambda v: jnp.concatenate([v[0].reshape(SSD_GROUPS, SSD_HPG), v[1].reshape(SSD_GROUPS, SSD_HPG), zpad], axis=1)
    dtb = grp(dt_bias)[:, None, :]
    alog = grp(a_log)[:, None, :]
    lanes = lambda v: jnp.pad(v.reshape(SSD_GROUPS, gw), ((0, 0), (0, SSD_GW - gw)))[:, None, :]
    dsk = lanes(jnp.repeat(d_skip, HEAD_DIM))
    nw = lanes(norm_w)
    nxb = _SSD_XBC_W // LANE
    grid_spec = dict(
        grid=(b, SSD_GROUPS),
        in_specs=[pl.BlockSpec((None, s, _SSD_XBC_W), lambda i, g: (i, 0, g)),
                  pl.BlockSpec((None, s, SSD_GW), lambda i, g: (i, 0, SSD_GROUPS * _SSD_XBC_W // SSD_GW + g)),
                  pl.BlockSpec((None, s, LANE), lambda i, g: (i, 0, SSD_GROUPS * (nxb + SSD_GW // LANE) + g)),
                  pl.BlockSpec((CONV_WIDTH, _SSD_XBC_W), lambda i, g: (0, g)),
                  pl.BlockSpec((1, _SSD_XBC_W), lambda i, g: (0, g)),
                  pl.BlockSpec((None, 1, LANE), lambda i, g: (g, 0, 0)),
                  pl.BlockSpec((None, 1, LANE), lambda i, g: (g, 0, 0)),
                  pl.BlockSpec((None, 1, SSD_GW), lambda i, g: (g, 0, 0)),
                  pl.BlockSpec((None, 1, SSD_GW), lambda i, g: (g, 0, 0))],
        out_specs=pl.BlockSpec((None, s, SSD_GW), lambda i, g: (i, 0, g)),
    )
    return pl.pallas_call(
        _ssd_kernel,
        out_shape=jax.ShapeDtypeStruct((b, s, SSD_GROUPS * SSD_GW), BF16),
        scratch_shapes=[pltpu.VMEM((s + 2 * _PAD, _SSD_XBC_W), F32), pltpu.VMEM((s, _SSD_XBC_W), F32),
                        pltpu.VMEM((s, SSD_GW), F32), pltpu.VMEM((s, LANE), F32), pltpu.VMEM((s, LANE), F32)],
        compiler_params=_cparams(("parallel", "parallel")),
        name="ssd",
        **grid_spec,
    )(p3, p3, p3, cw, cb, dtb, alog, dsk, nw)


def _scan_chunk(a, u, h, reverse):
    sub = lax.broadcasted_iota(jnp.int32, (CHUNK, 1), 0) % SUBLANE
    d = 1
    while d < SUBLANE:
        keep = (sub < SUBLANE - d) if reverse else (sub >= d)
        shift = CHUNK - d if reverse else d
        a_sh = jnp.where(keep, pltpu.roll(a, shift, 0), 1.0)
        u_sh = jnp.where(keep, pltpu.roll(u, shift, 0), 0.0)
        u = a * u_sh + u
        a = a * a_sh
        d *= 2
    n_groups = CHUNK // SUBLANE
    out = [None] * n_groups
    for g in (range(n_groups - 1, -1, -1) if reverse else range(n_groups)):
        rows = slice(g * SUBLANE, (g + 1) * SUBLANE)
        hg = u[rows] + a[rows] * h
        out[g] = hg
        h = hg[0:1] if reverse else hg[SUBLANE - 1:SUBLANE]
    return jnp.concatenate(out, axis=0), h


def _lru_kernel(x_ref, cw_ref, cb_ref, wg_ref, bg_ref, lam_ref, o_ref, xp_ref, hf_ref, ab_ref, ub_ref):
    s = x_ref.shape[0]
    n = s // CHUNK
    w = LRU_WIDTH
    xp_ref[0:_PAD, :] = jnp.zeros((_PAD, w), F32)
    xp_ref[_PAD + s:2 * _PAD + s, :] = jnp.zeros((_PAD, w), F32)

    def copy(c, carry):
        rows = pl.ds(pl.multiple_of(c * CHUNK, CHUNK), CHUNK)
        xp_ref[pl.ds(pl.multiple_of(c * CHUNK + _PAD, _PAD), CHUNK), :] = x_ref[rows, 0:w]
        return carry

    lax.fori_loop(0, n, copy, 0)
    nsp = -LRU_C * _softplus(-lam_ref[...])

    def gates(pre, xc, k):
        r = jax.nn.sigmoid(pre[:, 2 * k * w:(2 * k + 1) * w])
        i = jax.nn.sigmoid(pre[:, (2 * k + 1) * w:(2 * k + 2) * w])
        log_a = r * nsp[k:k + 1, :]
        a = jnp.exp(log_a)
        return a, jnp.sqrt(-jnp.tanh(log_a) * (a * a + 1.0)) * (i * xc)

    def fwd(c, h):
        rows = pl.ds(pl.multiple_of(c * CHUNK, CHUNK), CHUNK)
        xc = _dwconv_chunk(xp_ref, cw_ref, cb_ref, c)
        pre = jnp.dot(xc.astype(BF16), wg_ref[...], preferred_element_type=F32) + bg_ref[...]
        a_b, u_b = gates(pre, xc, 1)
        ab_ref[rows, :] = a_b
        ub_ref[rows, :] = u_b
        a_f, u_f = gates(pre, xc, 0)
        hc, h = _scan_chunk(a_f, u_f, h, False)
        hf_ref[rows, :] = hc
        return h

    lax.fori_loop(0, n, fwd, jnp.zeros((1, w), F32))

    def bwd(i, h):
        c = n - 1 - i
        rows = pl.ds(pl.multiple_of(c * CHUNK, CHUNK), CHUNK)
        hc, h = _scan_chunk(ab_ref[rows, :], ub_ref[rows, :], h, True)
        gate = x_ref[rows, w:2 * w]
        o_ref[rows, :] = ((hf_ref[rows, :] + hc) * jax.nn.gelu(gate)).astype(o_ref.dtype)
        return h

    lax.fori_loop(0, n, bwd, jnp.zeros((1, w), F32))


def _block_diag(wblk):
    eye = jnp.eye(LRU_BLOCKS, dtype=wblk.dtype)
    return jnp.einsum('hij,hk->hikj', wblk, eye).reshape(LRU_WIDTH, LRU_WIDTH)


def _lru(p_lru, b, s, conv_w, conv_b, lam, wa, ba, wx, bx):
    p3 = p_lru.reshape(b, s, 2 * LRU_WIDTH)
    wg = jnp.concatenate([_block_diag(wa[0]), _block_diag(wx[0]), _block_diag(wa[1]), _block_diag(wx[1])], axis=1).astype(BF16)
    bg = jnp.concatenate([ba[0], bx[0], ba[1], bx[1]])[None, :]
    w = LRU_WIDTH
    const = lambda shape: pl.BlockSpec(shape, lambda i: (0,) * len(shape))
    return pl.pallas_call(
        _lru_kernel,
        grid=(b,),
        in_specs=[pl.BlockSpec((None, s, 2 * w), lambda i: (i, 0, 0)),
                  const((CONV_WIDTH, w)), const((1, w)), const((w, 4 * w)), const((1, 4 * w)), const((2, w))],
        out_specs=pl.BlockSpec((None, s, w), lambda i: (i, 0, 0)),
        out_shape=jax.ShapeDtypeStruct((b, s, w), BF16),
        scratch_shapes=[pltpu.VMEM((s + 2 * _PAD, w), F32), pltpu.VMEM((s, w), F32),
                        pltpu.VMEM((s, w), F32), pltpu.VMEM((s, w), F32)],
        compiler_params=_cparams(("parallel",)),
        name="rglru",
    )(p3, conv_w, conv_b[None, :], wg, bg, lam)


def _outproj_weights(w_out):
    gw = SSD_HPG * HEAD_DIM
    idx = np.concatenate([np.concatenate([RET_WIDTH + g * gw + np.arange(gw), -np.ones(SSD_GW - gw, np.int64)])
                          for g in range(SSD_GROUPS)])
    w_ssd = _take_cols(w_out.T, idx).T
    return (w_out[:RET_WIDTH].astype(BF16), w_ssd.astype(BF16), w_out[RET_WIDTH + SSD_WIDTH:].astype(BF16))


def _kv_kernel(m_ref, g_ref, w_ref, o_ref):
    mn = _rms(m_ref[...], g_ref[...]).astype(BF16)
    o_ref[...] = jnp.dot(mn, w_ref[...], preferred_element_type=F32).astype(o_ref.dtype)


def _kv_proj(mem2, g, wkv):
    t, d = mem2.shape
    n = wkv.shape[1]
    tm = 256
    return pl.pallas_call(
        _kv_kernel,
        grid=(t // tm,),
        in_specs=[pl.BlockSpec((tm, d), lambda i: (i, 0)), pl.BlockSpec((1, d), lambda i: (0, 0)),
                  pl.BlockSpec((d, n), lambda i: (0, 0))],
        out_specs=pl.BlockSpec((tm, n), lambda i: (i, 0)),
        out_shape=jax.ShapeDtypeStruct((t, n), BF16),
        compiler_params=_cparams(("parallel",)),
        name="kvproj",
    )(mem2, g.reshape(1, d), wkv)


def _xattn_tile(h, g, wq_ref, k_ref, v_ref, wo_ref):
    xn = _rms(h, g).astype(BF16)
    q = jnp.dot(xn, wq_ref[...], preferred_element_type=F32).astype(BF16)
    outs = []
    for hd in range(CA_HEADS):
        cols = slice(hd * CA_HEAD_DIM, (hd + 1) * CA_HEAD_DIM)
        sc = lax.dot_general(q[:, cols], k_ref[:, cols], (((1,), (1,)), ((), ())), preferred_element_type=F32)
        sc = sc * (CA_HEAD_DIM ** -0.5)
        e = jnp.exp(sc - jnp.max(sc, axis=-1, keepdims=True))
        p = e / jnp.sum(e, axis=-1, keepdims=True)
        outs.append(jnp.dot(p.astype(BF16), v_ref[:, cols], preferred_element_type=F32).astype(BF16))
    o = jnp.concatenate(outs, axis=-1)
    return h + jnp.dot(o, wo_ref[...], preferred_element_type=F32)


_ROUTER_ROWS = SUBLANE + N_EXPERTS


def _first_argmax(v, row):
    m = jnp.max(v, axis=0, keepdims=True)
    return m, jnp.min(jnp.where(v == m, row, SUBLANE), axis=0, keepdims=True)


def _router_tile(h, g, w_ref, b_ref):
    xn = _rms(h, g).astype(BF16)
    lg = lax.dot_general(w_ref[...], xn, (((1,), (1,)), ((), ())), preferred_element_type=F32) + b_ref[...]
    tm = lg.shape[1]
    row = lax.broadcasted_iota(jnp.int32, (SUBLANE, tm), 0)
    gl = jnp.where(row < N_GROUPS, lg[0:SUBLANE], -jnp.inf)
    ge = jnp.exp(gl - jnp.max(gl, axis=0, keepdims=True))
    gp_all = ge / jnp.sum(ge, axis=0, keepdims=True)
    gp, gi = _first_argmax(gp_all, row)
    el = lg[SUBLANE:2 * SUBLANE]
    for g in range(1, N_GROUPS):
        el = jnp.where(gi == g, lg[SUBLANE * (g + 1):SUBLANE * (g + 2)], el)
    ee = jnp.exp(el - jnp.max(el, axis=0, keepdims=True))
    ep = ee / jnp.sum(ee, axis=0, keepdims=True)
    v1, i1 = _first_argmax(ep, row)
    v2, i2 = _first_argmax(jnp.where(row == i1, -1.0, ep), row)
    den = v1 + v2
    ids = jnp.where(row == 0, gi * EXPERTS_PER_GROUP + i1, jnp.where(row == 1, gi * EXPERTS_PER_GROUP + i2, 0))
    gates = jnp.where(row == 0, gp * v1 / den, jnp.where(row == 1, gp * v2 / den, 0.0))
    return ids, gates


assert D_MODEL == SUBLANE * LANE


def _token_tile(r):
    return pl.ds(pl.multiple_of(r * SUBLANE, SUBLANE), SUBLANE)


def _rows_to_tiles(ref, x):
    for j in range(SUBLANE):
        ref[pl.ds(j, x.shape[0], stride=SUBLANE), :] = x[:, j * LANE:(j + 1) * LANE]


def _tiles_to_rows(ref, n):
    return jnp.concatenate([ref[pl.ds(j, n, stride=SUBLANE), :] for j in range(SUBLANE)], axis=1)


def _post_kernel(h_ref, a_ref, b_ref, c_ref, wa_ref, wb_ref, wc_ref, gca_ref, wq_ref, k_ref, v_ref, wo_ref,
                 gmoe_ref, wr_ref, br_ref, o_ref, ot_ref, ids_ref, gates_ref):
    h = h_ref[...]
    h = h + jnp.dot(a_ref[...], wa_ref[...], preferred_element_type=F32)
    h = h + jnp.dot(b_ref[...], wb_ref[...], preferred_element_type=F32)
    h = h + jnp.dot(c_ref[...], wc_ref[...], preferred_element_type=F32)
    h = _xattn_tile(h, gca_ref[...], wq_ref, k_ref, v_ref, wo_ref)
    o_ref[...] = h
    _rows_to_tiles(ot_ref, h)
    ids, gates = _router_tile(h, gmoe_ref[...], wr_ref, br_ref)
    ids_ref[...] = ids
    gates_ref[...] = gates


def _post_mixer(h2, b, s, o_ret, o_ssd, o_lru, w_ret, w_ssd, w_lru, g_ca, wq, kv, wo, g_moe, wg, bg, we, be):
    t, d = h2.shape
    m = kv.shape[0] // b
    ts = ROW_TILE
    nts = s // ts
    zrow = jnp.zeros((SUBLANE - N_GROUPS, d), F32)
    wr = jnp.concatenate([wg.T, zrow, we.T], axis=0).astype(BF16)
    br = jnp.concatenate([bg, jnp.zeros((SUBLANE - N_GROUPS,), F32), be])[:, None]
    row = lambda n: pl.BlockSpec((ts, n), lambda i, j: (i * nts + j, 0))
    const = lambda r, c: pl.BlockSpec((r, c), lambda i, j: (0, 0))
    ka, kb, kc = o_ret.shape[1], o_ssd.shape[1], o_lru.shape[1]
    lanes = pl.BlockSpec((SUBLANE, ts), lambda i, j: (0, i * nts + j))
    return pl.pallas_call(
        _post_kernel,
        grid=(b, nts),
        in_specs=[row(d), row(ka), row(kb), row(kc), const(ka, d), const(kb, d), const(kc, d),
                  const(1, d), const(d, d),
                  pl.BlockSpec((m, d), lambda i, j: (i, 0)),
                  pl.BlockSpec((m, d), lambda i, j: (i, 1)),
                  const(d, d), const(1, d), const(_ROUTER_ROWS, d), const(_ROUTER_ROWS, 1)],
        out_specs=[row(d), pl.BlockSpec((ts * SUBLANE, LANE), lambda i, j: (i * nts + j, 0)), lanes, lanes],
        out_shape=[jax.ShapeDtypeStruct((t, d), F32), jax.ShapeDtypeStruct((t * SUBLANE, LANE), F32),
                   jax.ShapeDtypeStruct((SUBLANE, t), jnp.int32), jax.ShapeDtypeStruct((SUBLANE, t), F32)],
        compiler_params=_cparams(("parallel", "parallel")),
        name="post_mixer",
    )(h2, o_ret, o_ssd, o_lru, w_ret, w_ssd, w_lru, g_ca.reshape(1, d), wq, kv, kv, wo,
      g_moe.reshape(1, d), wr, br)


MOE_BLK = 256


_ASSIGN_BITS = 17


def _dispatch_plan(ids, t):
    n_assign = t * TOP_K
    assert n_assign <= 1 << _ASSIGN_BITS
    e = ids[:TOP_K].reshape(n_assign)
    key =jnp.left_shift(e, _ASSIGN_BITS) | jnp.arange(n_assign, dtype=jnp.int32)
    order = jnp.sort(key) & ((1 << _ASSIGN_BITS) - 1)
    order = jnp.concatenate([order, jnp.zeros((MOE_BLK,), jnp.int32)])
    experts = jnp.arange(N_EXPERTS, dtype=jnp.int32)
    counts = jnp.sum((e[None, :] == experts[:, None]).astype(jnp.int32), axis=1)
    start = jnp.cumsum(counts) - counts
    nb = (counts + MOE_BLK - 1) // MOE_BLK
    blk_end = jnp.cumsum(nb)
    n_blocks = n_assign // MOE_BLK + N_EXPERTS
    j = jnp.arange(n_blocks, dtype=jnp.int32)
    be = jnp.minimum(jnp.sum((blk_end[None, :] <= j[:, None]).astype(jnp.int32), axis=1), N_EXPERTS - 1)
    onehot = (be[:, None] == experts[None, :]).astype(jnp.int32)
    pick = lambda v: jnp.sum(onehot * v[None, :], axis=1)
    r = j - pick(blk_end - nb)
    row_start = jnp.clip(pick(start) + r * MOE_BLK, 0, n_assign)
    n_rows = jnp.clip(pick(counts) - r * MOE_BLK, 0, MOE_BLK)
    return order, be, row_start, n_rows, blk_end[-1:].astype(jnp.int32)


def _expert_kernel(be_ref, rs_ref, nr_ref, nu_ref, ord_ref, h_hbm, g_ref, w1_ref, w3_ref, w2_ref, y_hbm,
                   xbuf, ybuf, wb1, wb3, wb2, gsem, ssem, *, n_tokens):
    n_assign = TOP_K * n_tokens
    blk_rows = MOE_BLK * SUBLANE
    i = pl.program_id(0)
    n_used = nu_ref[0]
    slot = i % 2

    def gather(blk, s):
        base = rs_ref[blk]

        def body(r, carry):
            a = ord_ref[base + r]
            tok = jnp.where(a >= n_tokens, a - n_tokens, a)
            pltpu.make_async_copy(h_hbm.at[_token_tile(tok), :], xbuf.at[s, _token_tile(r), :], gsem.at[s]).start()
            return carry
        lax.fori_loop(0, MOE_BLK, body, 0, unroll=8)

    def scatter(blk, s):
        base = rs_ref[blk]
        cnt = nr_ref[blk]
        dump = n_assign + s * MOE_BLK

        def body(r, carry):
            dest = jnp.where(r < cnt, ord_ref[base + r], dump + r)
            pltpu.make_async_copy(ybuf.at[s, _token_tile(r), :], y_hbm.at[_token_tile(dest), :], ssem.at[s]).start()
            return carry
        lax.fori_loop(0, MOE_BLK, body, 0, unroll=8)

    def wait_gather(s):
        pltpu.make_async_copy(h_hbm.at[pl.ds(0, blk_rows), :], xbuf.at[s], gsem.at[s]).wait()

    def wait_scatter(s):
        pltpu.make_async_copy(ybuf.at[s], y_hbm.at[pl.ds(0, blk_rows), :], ssem.at[s]).wait()

    @pl.when(i == 0)
    def _():
        ybuf[0] = jnp.zeros((blk_rows, LANE), F32)
        for s in range(2):
            dump = pl.ds((n_assign + s * MOE_BLK) * SUBLANE, blk_rows)
            fill = pltpu.make_async_copy(ybuf.at[0], y_hbm.at[dump, :], ssem.at[0])
            fill.start()
            fill.wait()
        gather(0, 0)

    @pl.when(i + 1 < n_used)
    def _():
        gather(i + 1, 1 - slot)

    @pl.when(i < n_used)
    def _():
        @pl.when((i == 0) | (be_ref[i] != be_ref[jnp.maximum(i - 1, 0)]))
        def _():
            wb1[...] = w1_ref[...].astype(BF16)
            wb3[...] = w3_ref[...].astype(BF16)
            wb2[...] = w2_ref[...].astype(BF16)

        wait_gather(slot)

        @pl.when(i >= 2)
        def _():
            wait_scatter(slot)

        xn = _rms(_tiles_to_rows(xbuf.at[slot], MOE_BLK), g_ref[...]).astype(BF16)
        h1 = jnp.dot(xn, wb1[...], preferred_element_type=F32)
        h3 = jnp.dot(xn, wb3[...], preferred_element_type=F32)
        hid = (_silu(h1) * h3).astype(BF16)
        _rows_to_tiles(ybuf.at[slot], jnp.dot(hid, wb2[...], preferred_element_type=F32))
        scatter(i, slot)

        @pl.when(i == n_used - 1)
        def _():
            wait_scatter(slot)

            @pl.when(i >= 1)
            def _():
                wait_scatter(1 - slot)


def _experts(h_tiles, g, order, block_e, row_start, n_rows, n_used, w1, w3, w2, layer):
    t, d = h_tiles.shape[0] // SUBLANE, D_MODEL
    n_assign = t * TOP_K
    n_blocks = block_e.shape[0]
    de = w1.shape[3]
    wmap = lambda i, be, rs, nr, nu, od: (layer, be[i], 0, 0)
    grid_spec = pltpu.PrefetchScalarGridSpec(
        num_scalar_prefetch=5,
        grid=(n_blocks,),
        in_specs=[pl.BlockSpec(memory_space=pl.ANY),
                  pl.BlockSpec((1, d), lambda i, *_: (0, 0)),
                  pl.BlockSpec((None, None, d, de), wmap),
                  pl.BlockSpec((None, None, d, de), wmap),
                  pl.BlockSpec((None, None, de, d), wmap)],
        out_specs=pl.BlockSpec(memory_space=pl.ANY),
        scratch_shapes=[pltpu.VMEM((2, MOE_BLK * SUBLANE, LANE), F32), pltpu.VMEM((2, MOE_BLK * SUBLANE, LANE), F32),
                        pltpu.VMEM((d, de), BF16), pltpu.VMEM((d, de), BF16), pltpu.VMEM((de, d), BF16),
                        pltpu.SemaphoreType.DMA((2,)), pltpu.SemaphoreType.DMA((2,))],
    )
    return pl.pallas_call(
        functools.partial(_expert_kernel, n_tokens=t),
        grid_spec=grid_spec,
        out_shape=jax.ShapeDtypeStruct(((n_assign + 2 * MOE_BLK) * SUBLANE, LANE), F32),
        compiler_params=_cparams(("arbitrary",)),
        name="experts",
    )(block_e, row_start, n_rows, n_used, order, h_tiles, g.reshape(1, d), w1, w3, w2)


def _combine_kernel(h_ref, y0_ref, y1_ref, gt_ref, g_ref, o_ref, *, final):
    gt = gt_ref[...]
    tm = h_ref.shape[0]
    out = h_ref[...] + _tiles_to_rows(y0_ref, tm) * gt[:, 0:1] + _tiles_to_rows(y1_ref, tm) * gt[:, 1:2]
    if final:
        out = _rms(out, g_ref[...])
    o_ref[...] = out


def _combine(h2, y, gates, g_final, final):
    t, d = h2.shape
    tm = ROW_TILE
    nt = t // tm
    return pl.pallas_call(
        functools.partial(_combine_kernel, final=final),
        grid=(nt,),
        in_specs=[pl.BlockSpec((tm, d), lambda i: (i, 0)),
                  pl.BlockSpec((tm * SUBLANE, LANE), lambda i: (i, 0)),
                  pl.BlockSpec((tm * SUBLANE, LANE), lambda i: (nt + i, 0)),
                  pl.BlockSpec((tm, TOP_K), lambda i: (i, 0)),
                  pl.BlockSpec((1, d), lambda i: (0, 0))],
        out_specs=pl.BlockSpec((tm, d), lambda i: (i, 0)),
        out_shape=jax.ShapeDtypeStruct((t, d), F32),
        compiler_params=_cparams(("parallel",)),
        name="combine",
    )(h2, y, y, gates[:TOP_K].T, g_final.reshape(1, d))


def _moe(h2, h_tiles, ids, gates, g, w1, w3, w2, layer, g_final, final):
    t = h2.shape[0]
    order, block_e, row_start, n_rows, n_used = _dispatch_plan(ids, t)
    y = _experts(h_tiles, g, order, block_e, row_start, n_rows, n_used, w1, w3, w2, layer)
    return _combine(h2, y, gates, g_final, final)


def kernel(x, mem, g_mix, w_in, w_out, ssd_conv_w, ssd_conv_b, ssd_dt_bias, ssd_a_log, ssd_d, ssd_norm, lru_conv_w, lru_conv_b, lru_lambda, lru_wa, lru_ba, lru_wx, lru_bx, g_ca, g_mem, ca_wq, ca_wkv, ca_wo, g_moe, moe_wg, moe_bg, moe_we, moe_be, moe_w1, moe_w3, moe_w2, g_final):
    b, s, d = x.shape
    depth = g_mix.shape[0]
    h = x.reshape(b * s, d)
    mem2 = mem.reshape(b * mem.shape[1], d)
    ret_perm = _ret_perm()
    ssd_cols = _ssd_cols()[0]
    for l in range(depth):
        w_ret = w_in[l][:, ret_perm].astype(BF16)
        w_ssd = _take_cols(w_in[l], ssd_cols).astype(BF16)
        w_lru = w_in[l][:, _OFF_XR:].astype(BF16)
        p_ret, p_ssd, p_lru = _inproj(h, g_mix[l], w_ret, w_ssd, w_lru)
        o_ret = _retention(p_ret, b, s)
        o_ssd = _ssd(p_ssd, b, s, ssd_conv_w[l], ssd_conv_b[l], ssd_dt_bias[l], ssd_a_log[l], ssd_d[l], ssd_norm[l])
        o_lru = _lru(p_lru, b, s, lru_conv_w[l], lru_conv_b[l], lru_lambda[l], lru_wa[l], lru_ba[l], lru_wx[l], lru_bx[l])
        kv = _kv_proj(mem2, g_mem[l], ca_wkv[l].astype(BF16))
        h, h_tiles, ids, gates = _post_mixer(h, b, s, o_ret.reshape(b * s, -1), o_ssd.reshape(b * s, -1),
                                             o_lru.reshape(b * s, -1), *_outproj_weights(w_out[l]),
                                             g_ca[l], ca_wq[l].astype(BF16), kv, ca_wo[l].astype(BF16),
                                             g_moe[l], moe_wg[l], moe_bg[l], moe_we[l], moe_be[l])
        h = _moe(h, h_tiles, ids, gates, g_moe[l], moe_w1, moe_w3, moe_w2, l, g_final, l == depth - 1)
    return h.reshape(b, s, d)
```

```python
import functools
import math

import jax
import jax.numpy as jnp
import numpy as np
from jax import lax
from jax.experimental import pallas as pl
from jax.experimental.pallas import tpu as pltpu

D_MODEL = 1024
HEAD_DIM = 64
RET_WIDTH = 384
SSD_WIDTH = 384
LRU_WIDTH = 256
RET_HEADS = 6
SSD_HEADS = 6
SSD_GROUPS = 2
SSD_HPG = 3
SSD_STATE = 128
SSD_CONV_DIM = SSD_WIDTH + 2 * SSD_GROUPS * SSD_STATE
LRU_BLOCKS = 4
LRU_C = 8.0
CONV_WIDTH = 4
ROPE_BASE = 10000.0
CA_HEADS = 4
CA_HEAD_DIM = 256
N_GROUPS = 4
EXPERTS_PER_GROUP = 8
N_EXPERTS = 32
TOP_K = 2
D_EXPERT = 512
EPS = 1e-6

_OFF_Q, _OFF_K, _OFF_V, _OFF_G = 0, 384, 768, 1152
_OFF_Z = 1536
_OFF_XBC = 1920
_OFF_DT = 2816
_OFF_XR = 2822
_OFF_GATE = 3078
D_IN = 3334

LANE = 128
SUBLANE = 8
VMEM_LIMIT = 56 * 1024 * 1024

CHUNK = 128
ROW_TILE = 512
SSD_GW = 256

F32 = jnp.float32
BF16 = jnp.bfloat16


def _cparams(sem):
    return pltpu.CompilerParams(dimension_semantics=sem, vmem_limit_bytes=VMEM_LIMIT)


def _rms(x, g):
    return x * lax.rsqrt(jnp.mean(x * x, axis=-1, keepdims=True) + EPS) * g


def _inproj_kernel(x_ref, g_ref, wr_ref, ws_ref, wl_ref, pr_ref, ps_ref, pq_ref):
    xn = _rms(x_ref[...], g_ref[...]).astype(BF16)
    pr_ref[...] = jnp.dot(xn, wr_ref[...], preferred_element_type=F32)
    ps_ref[...] = jnp.dot(xn, ws_ref[...], preferred_element_type=F32)
    pq_ref[...] = jnp.dot(xn, wl_ref[...], preferred_element_type=F32)


def _inproj(x2, g, w_ret, w_ssd, w_lru):
    t, d = x2.shape
    tm = 256
    n_r, n_s, n_l = w_ret.shape[1], w_ssd.shape[1], w_lru.shape[1]
    full = lambda n: pl.BlockSpec((d, n), lambda i: (0, 0))
    return pl.pallas_call(
        _inproj_kernel,
        grid=(t // tm,),
        in_specs=[pl.BlockSpec((tm, d), lambda i: (i, 0)),
                  pl.BlockSpec((1, d), lambda i: (0, 0)),
                  full(n_r), full(n_s), full(n_l)],
        out_specs=[pl.BlockSpec((tm, n_r), lambda i: (i, 0)),
                   pl.BlockSpec((tm, n_s), lambda i: (i, 0)),
                   pl.BlockSpec((tm, n_l), lambda i: (i, 0))],
        out_shape=[jax.ShapeDtypeStruct((t, n_r), F32),
                   jax.ShapeDtypeStruct((t, n_s), F32),
                   jax.ShapeDtypeStruct((t, n_l), F32)],
        compiler_params=_cparams(("parallel",)),
        name="inproj",
    )(x2, g.reshape(1, d), w_ret, w_ssd, w_lru)


def _ret_kernel(q_ref, k_ref, v_ref, g_ref, cos_ref, sin_ref, d2_ref, dec_ref, gc_ref,
                o_ref, qs_ref, ks_ref, acc_ref):
    s = q_ref.shape[0]
    n = s // CHUNK
    lane = lax.broadcasted_iota(jnp.int32, (1, LANE), 1)
    mq0 = ((lane // 32) % 2 == 0).astype(F32)
    mq1 = 1.0 - mq0
    mv0 = (lane < HEAD_DIM).astype(F32)
    mv1 = 1.0 - mv0
    krow = lax.broadcasted_iota(jnp.int32, (LANE, LANE), 0)
    vcol = lax.broadcasted_iota(jnp.int32, (LANE, LANE), 1)
    bd = (((krow // 32) % 2) == (vcol // HEAD_DIM)).astype(F32)
    zeta_f = dec_ref[0]
    zeta_b = dec_ref[1]
    xi_f = dec_ref[2]
    xi_b = dec_ref[3]
    gc = gc_ref[...]
    d2 = d2_ref[...]

    def rot(x, c):
        rows = pl.ds(c * CHUNK, CHUNK)
        return x * cos_ref[rows, :] + pltpu.roll(x, 64, 1) * sin_ref[rows, :]

    def fwd(c, st):
        rows = pl.ds(pl.multiple_of(c * CHUNK, CHUNK), CHUNK)
        qc = rot(q_ref[rows, :], c)
        kc = rot(k_ref[rows, :], c) * (HEAD_DIM ** -0.5)
        vc = v_ref[rows, :]
        qs_ref[rows, :] = qc
        ks_ref[rows, :] = kc
        q2 = jnp.concatenate([qc * mq0, qc * mq1], axis=0)
        sc = lax.dot_general(q2, kc, (((1,), (1,)), ((), ())), preferred_element_type=F32) * d2
        o2 = jnp.dot(sc, vc, preferred_element_type=F32)
        o = o2[:CHUNK] * mv0 + o2[CHUNK:] * mv1
        o = o + jnp.dot(qc, st, preferred_element_type=F32) * xi_f
        acc_ref[rows, :] = o
        upd = jnp.dot(kc.T, vc * zeta_f, preferred_element_type=F32) * bd
        return st * gc + upd

    lax.fori_loop(0, n, fwd, jnp.zeros((LANE, LANE), F32), unroll=16)

    def bwd(i, st):
        c = n - 1 - i
        rows = pl.ds(pl.multiple_of(c * CHUNK, CHUNK), CHUNK)
        qc = qs_ref[rows, :]
        kc = ks_ref[rows, :]
        vc = v_ref[rows, :]
        o = acc_ref[rows, :] + jnp.dot(qc, st, preferred_element_type=F32) * xi_b
        oo = o * o
        ss0 = jnp.sum(oo * mv0, axis=1, keepdims=True)
        ss1 = jnp.sum(oo * mv1, axis=1, keepdims=True)
        scale = lax.rsqrt(ss0 * (1.0 / HEAD_DIM) + EPS) * mv0 + lax.rsqrt(ss1 * (1.0 / HEAD_DIM) + EPS) * mv1
        gate = g_ref[rows, :]
        gate = gate * jax.nn.sigmoid(gate)
        o_ref[rows, :] = (gate * (o * scale)).astype(o_ref.dtype)
        upd = jnp.dot(kc.T, vc * zeta_b, preferred_element_type=F32) * bd
        return st * gc + upd

    lax.fori_loop(0, n, bwd, jnp.zeros((LANE, LANE), F32), unroll=16)


def _ret_constants(s):
    half = HEAD_DIM // 2
    inv = ROPE_BASE ** (-jnp.arange(half, dtype=F32) / half)
    ang = jnp.arange(s, dtype=F32)[:, None] * inv[None, :]
    cos = jnp.cos(ang)
    sin = jnp.sin(ang)
    cos2 = jnp.concatenate([cos, cos, cos, cos], axis=1)
    sin2 = jnp.concatenate([-sin, -sin, sin, sin], axis=1)
    log_gamma = jnp.log1p(-jnp.exp2(-5.0 - jnp.arange(RET_HEADS, dtype=F32)))
    lg_pair = log_gamma.reshape(RET_HEADS // 2, 2)
    idx = jnp.arange(CHUNK, dtype=F32)
    adiff = jnp.abs(idx[:, None] - idx[None, :])
    d2 = jnp.exp(adiff[None, None] * lg_pair[:, :, None, None])
    d2 = d2.reshape(RET_HEADS // 2, 2 * CHUNK, CHUNK)
    lg_lane = jnp.repeat(lg_pair, HEAD_DIM, axis=1)
    pos = idx[None, :, None]
    lgl = lg_lane[:, None, :]
    dec = jnp.stack([jnp.exp((CHUNK - 1 - pos) * lgl), jnp.exp(pos * lgl),
                     jnp.exp((pos + 1) * lgl), jnp.exp((CHUNK - pos) * lgl)], axis=1)
    gc = jnp.exp(CHUNK * lg_lane)[:, None, :]
    return cos2, sin2, d2, dec, gc


def _retention(p_ret, b, s):
    p3 = p_ret.reshape(b, s, 4 * RET_WIDTH)
    cos2, sin2, d2, dec, gc = _ret_constants(s)
    npair = RET_HEADS // 2
    col = lambda off: pl.BlockSpec((None, s, LANE), lambda i, p: (i, 0, off + p))
    return pl.pallas_call(
        _ret_kernel,
        grid=(b, npair),
        in_specs=[col(0), col(npair), col(2 * npair), col(3 * npair),
                  pl.BlockSpec((s, LANE), lambda i, p: (0, 0)),
                  pl.BlockSpec((s, LANE), lambda i, p: (0, 0)),
                  pl.BlockSpec((None, 2 * CHUNK, CHUNK), lambda i, p: (p, 0, 0)),
                  pl.BlockSpec((None, 4, CHUNK, LANE), lambda i, p: (p, 0, 0, 0)),
                  pl.BlockSpec((None, 1, LANE), lambda i, p: (p, 0, 0))],
        out_specs=pl.BlockSpec((None, s, LANE), lambda i, p: (i, 0, p)),
        out_shape=jax.ShapeDtypeStruct((b, s, RET_WIDTH), BF16),
        scratch_shapes=[pltpu.VMEM((s, LANE), F32), pltpu.VMEM((s, LANE), F32), pltpu.VMEM((s, LANE), F32)],
        compiler_params=_cparams(("parallel", "parallel")),
        name="retention",
    )(p3, p3, p3, p3, cos2, sin2, d2, dec, gc)


def _ret_perm():
    pair = np.concatenate([np.arange(0, 32), np.arange(64, 96), np.arange(32, 64), np.arange(96, 128)])
    qperm = np.concatenate([128 * p + pair for p in range(RET_HEADS // 2)])
    return np.concatenate([_OFF_Q + qperm, _OFF_K + qperm, _OFF_V + np.arange(RET_WIDTH), _OFF_G + np.arange(RET_WIDTH)])


_PAD = 8


def _softplus(x):
    return jnp.maximum(x, 0.0) + jnp.log1p(jnp.exp(-jnp.abs(x)))


def _silu(x):
    return x * jax.nn.sigmoid(x)


def _expand_heads(cols, base, lane_head):
    acc = jnp.zeros((cols.shape[0], SSD_GW), F32)
    for r in range(SSD_HPG):
        acc = jnp.where(lane_head == r, cols[:, base + r:base + r + 1], acc)
    return acc


def _dwconv_chunk(xp_ref, cw_ref, cb_ref, c):
    rows = CHUNK + 2 * _PAD
    win = xp_ref[pl.ds(pl.multiple_of(c * CHUNK, CHUNK), rows), :]
    acc = cb_ref[...]
    for t in range(CONV_WIDTH):
        shift = (CONV_WIDTH // 2 - t) % rows
        tap = win if shift == 0 else pltpu.roll(win, shift, 0)
        acc = acc + tap[_PAD:_PAD + CHUNK, :] * cw_ref[t:t + 1, :]
    return acc


def _ssd_kernel(xbc_ref, z_ref, dt_ref, cw_ref, cb_ref, dtb_ref, alog_ref, dsk_ref, nw_ref, o_ref,
                xp_ref, xc_ref, y_ref, pc_ref, dts_ref):
    s = z_ref.shape[0]
    n = s // CHUNK
    w = xbc_ref.shape[1]
    xp_ref[0:_PAD, :] = jnp.zeros((_PAD, w), F32)
    xp_ref[_PAD + s:2 * _PAD + s, :] = jnp.zeros((_PAD, w), F32)

    def copy(c, carry):
        rows = pl.ds(pl.multiple_of(c * CHUNK, CHUNK), CHUNK)
        xp_ref[pl.ds(pl.multiple_of(c * CHUNK + _PAD, _PAD), CHUNK), :] = xbc_ref[rows, :]
        return carry

    lax.fori_loop(0, n, copy, 0)

    lane_head = lax.broadcasted_iota(jnp.int32, (1, SSD_GW), 1) // HEAD_DIM
    ii = lax.broadcasted_iota(jnp.int32, (CHUNK, CHUNK), 0)
    jj = lax.broadcasted_iota(jnp.int32, (CHUNK, CHUNK), 1)
    lower = ii >= jj
    tri = lower.astype(F32)
    a_neg = -jnp.exp(alog_ref[...])
    dt_bias = dtb_ref[...]

    def fwd(c, hf):
        rows = pl.ds(pl.multiple_of(c * CHUNK, CHUNK), CHUNK)
        xc = _silu(_dwconv_chunk(xp_ref, cw_ref, cb_ref, c))
        xc_ref[rows, :] = xc
        xs = xc[:, 0:SSD_GW]
        bm = xc[:, SSD_GW:SSD_GW + SSD_STATE]
        cm = xc[:, SSD_GW + SSD_STATE:]
        dt = _softplus(dt_ref[rows, :] + dt_bias)
        dts_ref[rows, :] = dt
        la = dt * a_neg
        p = jnp.dot(tri, la, precision=lax.Precision.HIGHEST, preferred_element_type=F32)
        pc_ref[rows, :] = p
        e = p - la
        pt = p.T
        et = e.T
        dtt = dt.T
        g = lax.dot_general(cm, bm, (((1,), (1,)), ((), ())), preferred_element_type=F32)
        y = jnp.zeros((CHUNK, SSD_GW), F32)
        for r in range(SSD_HPG):
            arg = jnp.where(lower, p[:, r:r + 1] - pt[r:r + 1, :], et[3 + r:4 + r, :] - e[:, 3 + r:4 + r])
            dtj = jnp.where(lower, dtt[r:r + 1, :], dtt[3 + r:4 + r, :])
            m = g * (jnp.exp(arg) * dtj)
            y = y + jnp.dot(m, jnp.where(lane_head == r, xs, 0.0), preferred_element_type=F32)
        pf = _expand_heads(p, 0, lane_head)
        pf_last = _expand_heads(p[CHUNK - 1:CHUNK, :], 0, lane_head)
        dtf = _expand_heads(dt, 0, lane_head)
        y = y + jnp.dot(cm, hf, preferred_element_type=F32) * jnp.exp(pf)
        y_ref[rows, :] = y
        upd = jnp.dot(bm.T, xs * (jnp.exp(pf_last - pf) * dtf), preferred_element_type=F32)
        return hf * jnp.exp(pf_last) + upd

    lax.fori_loop(0, n, fwd, jnp.zeros((SSD_STATE, SSD_GW), F32), unroll=2)

    dsk = dsk_ref[...]
    nw = nw_ref[...]

    def bwd(i, hb):
        c = n - 1 - i
        rows = pl.ds(pl.multiple_of(c * CHUNK, CHUNK), CHUNK)
        xs = xc_ref[rows, 0:SSD_GW]
        bm = xc_ref[rows, SSD_GW:SSD_GW + SSD_STATE]
        cm = xc_ref[rows, SSD_GW + SSD_STATE:]
        dt = dts_ref[rows, :]
        p = pc_ref[rows, :]
        e = p - dt * a_neg
        eb = _expand_heads(e, 3, lane_head)
        tb = _expand_heads(p[CHUNK - 1:CHUNK, :], 3, lane_head)
        dtb = _expand_heads(dt, 3, lane_head)
        y = y_ref[rows, :] + jnp.dot(cm, hb, preferred_element_type=F32) * jnp.exp(tb - eb)
        y = (y + dsk * xs) * _silu(z_ref[rows, :])
        ms = jnp.sum(y * y, axis=1, keepdims=True) * (1.0 / (SSD_HPG * HEAD_DIM))
        o_ref[rows, :] = (y * lax.rsqrt(ms + EPS) * nw).astype(o_ref.dtype)
        upd = jnp.dot(bm.T, xs * (jnp.exp(eb) * dtb), preferred_element_type=F32)
        return hb * jnp.exp(tb) + upd

    lax.fori_loop(0, n, bwd, jnp.zeros((SSD_STATE, SSD_GW), F32), unroll=2)


_SSD_XBC_W = SSD_GW + 2 * SSD_STATE
_SSD_SLAB = SSD_GROUPS * (_SSD_XBC_W + SSD_GW + LANE)


def _ssd_cols():
    gw = SSD_HPG * HEAD_DIM
    pad = lambda k: -np.ones(k, np.int64)
    xbc, zz, dtc, conv = [], [], [], []
    for g in range(SSD_GROUPS):
        xcols = np.arange(gw) + g * gw
        bcols = SSD_WIDTH + g * SSD_STATE + np.arange(SSD_STATE)
        ccols = SSD_WIDTH + SSD_GROUPS * SSD_STATE + g * SSD_STATE + np.arange(SSD_STATE)
        c_idx = np.concatenate([xcols, pad(SSD_GW - gw), bcols, ccols])
        conv.append(c_idx)
        xbc.append(np.where(c_idx >= 0, _OFF_XBC + c_idx, -1))
        zz.append(np.concatenate([_OFF_Z + xcols, pad(SSD_GW - gw)]))
        heads = _OFF_DT + g * SSD_HPG + np.arange(SSD_HPG)
        dtc.append(np.concatenate([heads, heads, pad(LANE - 2 * SSD_HPG)]))
    return np.concatenate(xbc + zz + dtc), np.concatenate(conv)


def _take_cols(a, idx):
    a = jnp.concatenate([a, jnp.zeros(a.shape[:-1] + (1,), a.dtype)], axis=-1)
    return a[..., np.where(idx >= 0, idx, a.shape[-1] - 1)]


def _ssd(p_ssd, b, s, conv_w, conv_b, dt_bias, a_log, d_skip, norm_w):
    p3 = p_ssd.reshape(b, s, _SSD_SLAB)
    _, conv_idx = _ssd_cols()
    cw = _take_cols(conv_w, conv_idx)
    cb = _take_cols(conv_b[None, :], conv_idx)
    gw = SSD_HPG * HEAD_DIM
    zpad = jnp.zeros((SSD_GROUPS, LANE - 2 * SSD_HPG), F32)
    grp = lambda v: jnp.concatenate([v[0].reshape(SSD_GROUPS, SSD_HPG), v[1].reshape(SSD_GROUPS, SSD_HPG), zpad], axis=1)
    dtb = grp(dt_bias)[:, None, :]
    alog = grp(a_log)[:, None, :]
    lanes = lambda v: jnp.pad(v.reshape(SSD_GROUPS, gw), ((0, 0), (0, SSD_GW - gw)))[:, None, :]
    dsk = lanes(jnp.repeat(d_skip, HEAD_DIM))
    nw = lanes(norm_w)
    nxb = _SSD_XBC_W // LANE
    grid_spec = dict(
        grid=(b, SSD_GROUPS),
        in_specs=[pl.BlockSpec((None, s, _SSD_XBC_W), lambda i, g: (i, 0, g)),
                  pl.BlockSpec((None, s, SSD_GW), lambda i, g: (i, 0, SSD_GROUPS * _SSD_XBC_W // SSD_GW + g)),
                  pl.BlockSpec((None, s, LANE), lambda i, g: (i, 0, SSD_GROUPS * (nxb + SSD_GW // LANE) + g)),
                  pl.BlockSpec((CONV_WIDTH, _SSD_XBC_W), lambda i, g: (0, g)),
                  pl.BlockSpec((1, _SSD_XBC_W), lambda i, g: (0, g)),
                  pl.BlockSpec((None, 1, LANE), lambda i, g: (g, 0, 0)),
                  pl.BlockSpec((None, 1, LANE), lambda i, g: (g, 0, 0)),
                  pl.BlockSpec((None, 1, SSD_GW), lambda i, g: (g, 0, 0)),
                  pl.BlockSpec((None, 1, SSD_GW), lambda i, g: (g, 0, 0))],
        out_specs=pl.BlockSpec((None, s, SSD_GW), lambda i, g: (i, 0, g)),
    )
    return pl.pallas_call(
        _ssd_kernel,
        out_shape=jax.ShapeDtypeStruct((b, s, SSD_GROUPS * SSD_GW), BF16),
        scratch_shapes=[pltpu.VMEM((s + 2 * _PAD, _SSD_XBC_W), F32), pltpu.VMEM((s, _SSD_XBC_W), F32),
                        pltpu.VMEM((s, SSD_GW), F32), pltpu.VMEM((s, LANE), F32), pltpu.VMEM((s, LANE), F32)],
        compiler_params=_cparams(("parallel", "parallel")),
        name="ssd",
        **grid_spec,
    )(p3, p3, p3, cw, cb, dtb, alog, dsk, nw)


def _scan_chunk(a, u, h, reverse):
    n_groups = CHUNK // SUBLANE
    width = a.shape[1]
    a = a.reshape(n_groups, SUBLANE, width)
    u = u.reshape(n_groups, SUBLANE, width)
    sub = lax.broadcasted_iota(jnp.int32, (1, SUBLANE, 1), 1)
    d = 1
    while d < SUBLANE:
        keep = (sub < SUBLANE - d) if reverse else (sub >= d)
        shift = SUBLANE - d if reverse else d
        a_sh = jnp.where(keep, pltpu.roll(a, shift, 1), 1.0)
        u_sh = jnp.where(keep, pltpu.roll(u, shift, 1), 0.0)
        u = a * u_sh + u
        a = a * a_sh
        d *= 2
    a = a.reshape(CHUNK, width)
    u = u.reshape(CHUNK, width)
    out = [None] * n_groups
    for g in (range(n_groups - 1, -1, -1) if reverse else range(n_groups)):
        rows = slice(g * SUBLANE, (g + 1) * SUBLANE)
        hg = u[rows] + a[rows] * h
        out[g] = hg
        h = hg[0:1] if reverse else hg[SUBLANE - 1:SUBLANE]
    return jnp.concatenate(out, axis=0), h


def _lru_kernel(x_ref, cw_ref, cb_ref, wg_ref, bg_ref, lam_ref, o_ref, xp_ref, hf_ref, ab_ref, ub_ref):
    s = x_ref.shape[0]
    n = s // CHUNK
    w = LRU_WIDTH
    xp_ref[0:_PAD, :] = jnp.zeros((_PAD, w), F32)
    xp_ref[_PAD + s:2 * _PAD + s, :] = jnp.zeros((_PAD, w), F32)

    def copy(c, carry):
        rows = pl.ds(pl.multiple_of(c * CHUNK, CHUNK), CHUNK)
        xp_ref[pl.ds(pl.multiple_of(c * CHUNK + _PAD, _PAD), CHUNK), :] = x_ref[rows, 0:w]
        return carry

    lax.fori_loop(0, n, copy, 0)
    nsp = -LRU_C * _softplus(-lam_ref[...])

    def gates(pre, xc, k):
        r = jax.nn.sigmoid(pre[:, 2 * k * w:(2 * k + 1) * w])
        i = jax.nn.sigmoid(pre[:, (2 * k + 1) * w:(2 * k + 2) * w])
        log_a = r * nsp[k:k + 1, :]
        a = jnp.exp(log_a)
        return a, jnp.sqrt(-jnp.tanh(log_a) * (a * a + 1.0)) * (i * xc)

    def fwd(c, h):
        rows = pl.ds(pl.multiple_of(c * CHUNK, CHUNK), CHUNK)
        xc = _dwconv_chunk(xp_ref, cw_ref, cb_ref, c)
        pre = jnp.dot(xc.astype(BF16), wg_ref[...], preferred_element_type=F32) + bg_ref[...]
        a_b, u_b = gates(pre, xc, 1)
        ab_ref[rows, :] = a_b
        ub_ref[rows, :] = u_b
        a_f, u_f = gates(pre, xc, 0)
        hc, h = _scan_chunk(a_f, u_f, h, False)
        hf_ref[rows, :] = hc
        return h

    lax.fori_loop(0, n, fwd, jnp.zeros((1, w), F32))

    def bwd(i, h):
        c = n - 1 - i
        rows = pl.ds(pl.multiple_of(c * CHUNK, CHUNK), CHUNK)
        hc, h = _scan_chunk(ab_ref[rows, :], ub_ref[rows, :], h, True)
        gate = x_ref[rows, w:2 * w]
        o_ref[rows, :] = ((hf_ref[rows, :] + hc) * jax.nn.gelu(gate)).astype(o_ref.dtype)
        return h

    lax.fori_loop(0, n, bwd, jnp.zeros((1, w), F32))


def _block_diag(wblk):
    eye = jnp.eye(LRU_BLOCKS, dtype=wblk.dtype)
    return jnp.einsum('hij,hk->hikj', wblk, eye).reshape(LRU_WIDTH, LRU_WIDTH)


def _lru(p_lru, b, s, conv_w, conv_b, lam, wa, ba, wx, bx):
    p3 = p_lru.reshape(b, s, 2 * LRU_WIDTH)
    wg = jnp.concatenate([_block_diag(wa[0]), _block_diag(wx[0]), _block_diag(wa[1]), _block_diag(wx[1])], axis=1).astype(BF16)
    bg = jnp.concatenate([ba[0], bx[0], ba[1], bx[1]])[None, :]
    w = LRU_WIDTH
    const = lambda shape: pl.BlockSpec(shape, lambda i: (0,) * len(shape))
    return pl.pallas_call(
        _lru_kernel,
        grid=(b,),
        in_specs=[pl.BlockSpec((None, s, 2 * w), lambda i: (i, 0, 0)),
                  const((CONV_WIDTH, w)), const((1, w)), const((w, 4 * w)), const((1, 4 * w)), const((2, w))],
        out_specs=pl.BlockSpec((None, s, w), lambda i: (i, 0, 0)),
        out_shape=jax.ShapeDtypeStruct((b, s, w), BF16),
        scratch_shapes=[pltpu.VMEM((s + 2 * _PAD, w), F32), pltpu.VMEM((s, w), F32),
                        pltpu.VMEM((s, w), F32), pltpu.VMEM((s, w), F32)],
        compiler_params=_cparams(("parallel",)),
        name="rglru",
    )(p3, conv_w, conv_b[None, :], wg, bg, lam)


def _outproj_weights(w_out):
    gw = SSD_HPG * HEAD_DIM
    idx = np.concatenate([np.concatenate([RET_WIDTH + g * gw + np.arange(gw), -np.ones(SSD_GW - gw, np.int64)])
                          for g in range(SSD_GROUPS)])
    w_ssd = _take_cols(w_out.T, idx).T
    return (w_out[:RET_WIDTH].astype(BF16), w_ssd.astype(BF16), w_out[RET_WIDTH + SSD_WIDTH:].astype(BF16))


def _kv_kernel(m_ref, g_ref, w_ref, o_ref):
    mn = _rms(m_ref[...], g_ref[...]).astype(BF16)
    o_ref[...] = jnp.dot(mn, w_ref[...], preferred_element_type=F32).astype(o_ref.dtype)


def _kv_proj(mem2, g, wkv):
    t, d = mem2.shape
    n = wkv.shape[1]
    tm = 256
    return pl.pallas_call(
        _kv_kernel,
        grid=(t // tm,),
        in_specs=[pl.BlockSpec((tm, d), lambda i: (i, 0)), pl.BlockSpec((1, d), lambda i: (0, 0)),
                  pl.BlockSpec((d, n), lambda i: (0, 0))],
        out_specs=pl.BlockSpec((tm, n), lambda i: (i, 0)),
        out_shape=jax.ShapeDtypeStruct((t, n), BF16),
        compiler_params=_cparams(("parallel",)),
        name="kvproj",
    )(mem2, g.reshape(1, d), wkv)


def _xattn_tile(h, g, wq_ref, k_ref, v_ref, wo_ref):
    xn = _rms(h, g).astype(BF16)
    q = jnp.dot(xn, wq_ref[...], preferred_element_type=F32).astype(BF16)
    outs = []
    for hd in range(CA_HEADS):
        cols = slice(hd * CA_HEAD_DIM, (hd + 1) * CA_HEAD_DIM)
        sc = lax.dot_general(q[:, cols], k_ref[:, cols], (((1,), (1,)), ((), ())), preferred_element_type=F32)
        sc = sc * (CA_HEAD_DIM ** -0.5)
        e = jnp.exp(sc - jnp.max(sc, axis=-1, keepdims=True))
        p = e / jnp.sum(e, axis=-1, keepdims=True)
        outs.append(jnp.dot(p.astype(BF16), v_ref[:, cols], preferred_element_type=F32).astype(BF16))
    o = jnp.concatenate(outs, axis=-1)
    return h + jnp.dot(o, wo_ref[...], preferred_element_type=F32)


_ROUTER_ROWS = SUBLANE + N_EXPERTS


def _first_argmax(v, row):
    m = jnp.max(v, axis=0, keepdims=True)
    return m, jnp.min(jnp.where(v == m, row, SUBLANE), axis=0, keepdims=True)


def _router_tile(h, g, w_ref, b_ref):
    xn = _rms(h, g).astype(BF16)
    lg = lax.dot_general(w_ref[...], xn, (((1,), (1,)), ((), ())), preferred_element_type=F32) + b_ref[...]
    tm = lg.shape[1]
    row = lax.broadcasted_iota(jnp.int32, (SUBLANE, tm), 0)
    gl = jnp.where(row < N_GROUPS, lg[0:SUBLANE], -jnp.inf)
    ge = jnp.exp(gl - jnp.max(gl, axis=0, keepdims=True))
    gp_all = ge / jnp.sum(ge, axis=0, keepdims=True)
    gp, gi = _first_argmax(gp_all, row)
    el = lg[SUBLANE:2 * SUBLANE]
    for g in range(1, N_GROUPS):
        el = jnp.where(gi == g, lg[SUBLANE * (g + 1):SUBLANE * (g + 2)], el)
    ee = jnp.exp(el - jnp.max(el, axis=0, keepdims=True))
    ep = ee / jnp.sum(ee, axis=0, keepdims=True)
    v1, i1 = _first_argmax(ep, row)
    v2, i2 = _first_argmax(jnp.where(row == i1, -1.0, ep), row)
    den = v1 + v2
    ids = jnp.where(row == 0, gi * EXPERTS_PER_GROUP + i1, jnp.where(row == 1, gi * EXPERTS_PER_GROUP + i2, 0))
    gates = jnp.where(row == 0, gp * v1 / den, jnp.where(row == 1, gp * v2 / den, 0.0))
    return ids, gates


assert D_MODEL == SUBLANE * LANE


def _token_tile(r):
    return pl.ds(pl.multiple_of(r * SUBLANE, SUBLANE), SUBLANE)


def _rows_to_tiles(ref, x):
    for j in range(SUBLANE):
        ref[pl.ds(j, x.shape[0], stride=SUBLANE), :] = x[:, j * LANE:(j + 1) * LANE]


def _tiles_to_rows(ref, n):
    return jnp.concatenate([ref[pl.ds(j, n, stride=SUBLANE), :] for j in range(SUBLANE)], axis=1)


def _post_kernel(h_ref, a_ref, b_ref, c_ref, wa_ref, wb_ref, wc_ref, gca_ref, wq_ref, k_ref, v_ref, wo_ref,
                 gmoe_ref, wr_ref, br_ref, o_ref, ot_ref, ids_ref, gates_ref):
    h = h_ref[...]
    h = h + jnp.dot(a_ref[...], wa_ref[...], preferred_element_type=F32)
    h = h + jnp.dot(b_ref[...], wb_ref[...], preferred_element_type=F32)
    h = h + jnp.dot(c_ref[...], wc_ref[...], preferred_element_type=F32)
    h = _xattn_tile(h, gca_ref[...], wq_ref, k_ref, v_ref, wo_ref)
    o_ref[...] = h
    _rows_to_tiles(ot_ref, h)
    ids, gates = _router_tile(h, gmoe_ref[...], wr_ref, br_ref)
    ids_ref[...] = ids
    gates_ref[...] = gates


def _post_mixer(h2, b, s, o_ret, o_ssd, o_lru, w_ret, w_ssd, w_lru, g_ca, wq, kv, wo, g_moe, wg, bg, we, be):
    t, d = h2.shape
    m = kv.shape[0] // b
    ts = ROW_TILE
    nts = s // ts
    zrow = jnp.zeros((SUBLANE - N_GROUPS, d), F32)
    wr = jnp.concatenate([wg.T, zrow, we.T], axis=0).astype(BF16)
    br = jnp.concatenate([bg, jnp.zeros((SUBLANE - N_GROUPS,), F32), be])[:, None]
    row = lambda n: pl.BlockSpec((ts, n), lambda i, j: (i * nts + j, 0))
    const = lambda r, c: pl.BlockSpec((r, c), lambda i, j: (0, 0))
    ka, kb, kc = o_ret.shape[1], o_ssd.shape[1], o_lru.shape[1]
    lanes = pl.BlockSpec((SUBLANE, ts), lambda i, j: (0, i * nts + j))
    return pl.pallas_call(
        _post_kernel,
        grid=(b, nts),
        in_specs=[row(d), row(ka), row(kb), row(kc), const(ka, d), const(kb, d), const(kc, d),
                  const(1, d), const(d, d),
                  pl.BlockSpec((m, d), lambda i, j: (i, 0)),
                  pl.BlockSpec((m, d), lambda i, j: (i, 1)),
                  const(d, d), const(1, d), const(_ROUTER_ROWS, d), const(_ROUTER_ROWS, 1)],
        out_specs=[row(d), pl.BlockSpec((ts * SUBLANE, LANE), lambda i, j: (i * nts + j, 0)), lanes, lanes],
        out_shape=[jax.ShapeDtypeStruct((t, d), F32), jax.ShapeDtypeStruct((t * SUBLANE, LANE), F32),
                   jax.ShapeDtypeStruct((SUBLANE, t), jnp.int32), jax.ShapeDtypeStruct((SUBLANE, t), F32)],
        compiler_params=_cparams(("parallel", "parallel")),
        name="post_mixer",
    )(h2, o_ret, o_ssd, o_lru, w_ret, w_ssd, w_lru, g_ca.reshape(1, d), wq, kv, kv, wo,
      g_moe.reshape(1, d), wr, br)


MOE_BLK = 256


_ASSIGN_BITS = 17


def _dispatch_plan(ids, t):
    n_assign = t * TOP_K
    assert n_assign <= 1 << _ASSIGN_BITS
    e = ids[:TOP_K].reshape(n_assign)
    key =jnp.left_shift(e, _ASSIGN_BITS) | jnp.arange(n_assign, dtype=jnp.int32)
    order = jnp.sort(key) & ((1 << _ASSIGN_BITS) - 1)
    order = jnp.concatenate([order, jnp.zeros((MOE_BLK,), jnp.int32)])
    experts = jnp.arange(N_EXPERTS, dtype=jnp.int32)
    counts = jnp.sum((e[None, :] == experts[:, None]).astype(jnp.int32), axis=1)
    start = jnp.cumsum(counts) - counts
    nb = (counts + MOE_BLK - 1) // MOE_BLK
    blk_end = jnp.cumsum(nb)
    n_blocks = n_assign // MOE_BLK + N_EXPERTS
    j = jnp.arange(n_blocks, dtype=jnp.int32)
    be = jnp.minimum(jnp.sum((blk_end[None, :] <= j[:, None]).astype(jnp.int32), axis=1), N_EXPERTS - 1)
    onehot = (be[:, None] == experts[None, :]).astype(jnp.int32)
    pick = lambda v: jnp.sum(onehot * v[None, :], axis=1)
    r = j - pick(blk_end - nb)
    row_start = jnp.clip(pick(start) + r * MOE_BLK, 0, n_assign)
    n_rows = jnp.clip(pick(counts) - r * MOE_BLK, 0, MOE_BLK)
    tok = jnp.where(order >= t, order - t, order)
    rr = jnp.arange(MOE_BLK, dtype=jnp.int32)[None, :]
    dump = n_assign + (j % 2)[:, None] * MOE_BLK + rr
    dest = jnp.where(rr < n_rows[:, None], order[row_start[:, None] + rr], dump).reshape(n_blocks * MOE_BLK)
    return tok, dest, be, row_start, blk_end[-1:].astype(jnp.int32)


def _expert_kernel(be_ref, rs_ref, nu_ref, tok_ref, dst_ref, h_hbm, g_ref, w1_ref, w3_ref, w2_ref, y_hbm,
                   xbuf, ybuf, wb1, wb3, wb2, gsem, ssem, *, n_tokens):
    n_assign = TOP_K * n_tokens
    blk_rows = MOE_BLK * SUBLANE
    i = pl.program_id(0)
    n_used = nu_ref[0]
    slot = i % 2

    def gather(blk, s):
        base = rs_ref[blk]

        def body(r, carry):
            tok = tok_ref[base + r]
            pltpu.make_async_copy(h_hbm.at[_token_tile(tok), :], xbuf.at[s, _token_tile(r), :], gsem.at[s]).start()
            return carry
        lax.fori_loop(0, MOE_BLK, body, 0, unroll=8)

    def scatter(blk, s):
        base = blk * MOE_BLK

        def body(r, carry):
            dest = dst_ref[base + r]
            pltpu.make_async_copy(ybuf.at[s, _token_tile(r), :], y_hbm.at[_token_tile(dest), :], ssem.at[s]).start()
            return carry
        lax.fori_loop(0, MOE_BLK, body, 0, unroll=8)

    def wait_gather(s):
        pltpu.make_async_copy(h_hbm.at[pl.ds(0, blk_rows), :], xbuf.at[s], gsem.at[s]).wait()

    def wait_scatter(s):
        pltpu.make_async_copy(ybuf.at[s], y_hbm.at[pl.ds(0, blk_rows), :], ssem.at[s]).wait()

    @pl.when(i == 0)
    def _():
        ybuf[0] = jnp.zeros((blk_rows, LANE), F32)
        for s in range(2):
            dump = pl.ds((n_assign + s * MOE_BLK) * SUBLANE, blk_rows)
            fill = pltpu.make_async_copy(ybuf.at[0], y_hbm.at[dump, :], ssem.at[0])
            fill.start()
            fill.wait()
        gather(0, 0)

    @pl.when(i + 1 < n_used)
    def _():
        gather(i + 1, 1 - slot)

    @pl.when(i < n_used)
    def _():
        @pl.when((i == 0) | (be_ref[i] != be_ref[jnp.maximum(i - 1, 0)]))
        def _():
            wb1[...] = w1_ref[...].astype(BF16)
            wb3[...] = w3_ref[...].astype(BF16)
            wb2[...] = w2_ref[...].astype(BF16)

        wait_gather(slot)

        @pl.when(i >= 2)
        def _():
            wait_scatter(slot)

        xn = _rms(_tiles_to_rows(xbuf.at[slot], MOE_BLK), g_ref[...]).astype(BF16)
        h1 = jnp.dot(xn, wb1[...], preferred_element_type=F32)
        h3 = jnp.dot(xn, wb3[...], preferred_element_type=F32)
        hid = (_silu(h1) * h3).astype(BF16)
        _rows_to_tiles(ybuf.at[slot], jnp.dot(hid, wb2[...], preferred_element_type=F32))
        scatter(i, slot)

        @pl.when(i == n_used - 1)
        def _():
            wait_scatter(slot)

            @pl.when(i >= 1)
            def _():
                wait_scatter(1 - slot)


def _experts(h_tiles, g, tok, dest, block_e, row_start, n_used, w1, w3, w2, layer):
    t, d = h_tiles.shape[0] // SUBLANE, D_MODEL
    n_assign = t * TOP_K
    n_blocks = block_e.shape[0]
    de = w1.shape[3]
    wmap = lambda i, be, *_: (layer, be[i], 0, 0)
    grid_spec = pltpu.PrefetchScalarGridSpec(
        num_scalar_prefetch=5,
        grid=(n_blocks,),
        in_specs=[pl.BlockSpec(memory_space=pl.ANY),
                  pl.BlockSpec((1, d), lambda i, *_: (0, 0)),
                  pl.BlockSpec((None, None, d, de), wmap),
                  pl.BlockSpec((None, None, d, de), wmap),
                  pl.BlockSpec((None, None, de, d), wmap)],
        out_specs=pl.BlockSpec(memory_space=pl.ANY),
        scratch_shapes=[pltpu.VMEM((2, MOE_BLK * SUBLANE, LANE), F32), pltpu.VMEM((2, MOE_BLK * SUBLANE, LANE), F32),
                        pltpu.VMEM((d, de), BF16), pltpu.VMEM((d, de), BF16), pltpu.VMEM((de, d), BF16),
                        pltpu.SemaphoreType.DMA((2,)), pltpu.SemaphoreType.DMA((2,))],
    )
    return pl.pallas_call(
        functools.partial(_expert_kernel, n_tokens=t),
        grid_spec=grid_spec,
        out_shape=jax.ShapeDtypeStruct(((n_assign + 2 * MOE_BLK) * SUBLANE, LANE), F32),
        compiler_params=_cparams(("arbitrary",)),
        name="experts",
    )(block_e, row_start, n_used, tok, dest, h_tiles, g.reshape(1, d), w1, w3, w2)


def _combine_kernel(h_ref, y0_ref, y1_ref, gt_ref, g_ref, o_ref, *, final):
    gt = gt_ref[...]
    tm = h_ref.shape[0]
    out = h_ref[...] + _tiles_to_rows(y0_ref, tm) * gt[:, 0:1] + _tiles_to_rows(y1_ref, tm) * gt[:, 1:2]
    if final:
        out = _rms(out, g_ref[...])
    o_ref[...] = out


def _combine(h2, y, gates, g_final, final):
    t, d = h2.shape
    tm = ROW_TILE
    nt = t // tm
    return pl.pallas_call(
        functools.partial(_combine_kernel, final=final),
        grid=(nt,),
        in_specs=[pl.BlockSpec((tm, d), lambda i: (i, 0)),
                  pl.BlockSpec((tm * SUBLANE, LANE), lambda i: (i, 0)),
                  pl.BlockSpec((tm * SUBLANE, LANE), lambda i: (nt + i, 0)),
                  pl.BlockSpec((tm, TOP_K), lambda i: (i, 0)),
                  pl.BlockSpec((1, d), lambda i: (0, 0))],
        out_specs=pl.BlockSpec((tm, d), lambda i: (i, 0)),
        out_shape=jax.ShapeDtypeStruct((t, d), F32),
        compiler_params=_cparams(("parallel",)),
        name="combine",
    )(h2, y, y, gates[:TOP_K].T, g_final.reshape(1, d))


def _moe(h2, h_tiles, ids, gates, g, w1, w3, w2, layer, g_final, final):
    t = h2.shape[0]
    tok, dest, block_e, row_start, n_used = _dispatch_plan(ids, t)
    y = _experts(h_tiles, g, tok, dest, block_e, row_start, n_used, w1, w3, w2, layer)
    return _combine(h2, y, gates, g_final, final)


def kernel(x, mem, g_mix, w_in, w_out, ssd_conv_w, ssd_conv_b, ssd_dt_bias, ssd_a_log, ssd_d, ssd_norm, lru_conv_w, lru_conv_b, lru_lambda, lru_wa, lru_ba, lru_wx, lru_bx, g_ca, g_mem, ca_wq, ca_wkv, ca_wo, g_moe, moe_wg, moe_bg, moe_we, moe_be, moe_w1, moe_w3, moe_w2, g_final):
    b, s, d = x.shape
    depth = g_mix.shape[0]
    h = x.reshape(b * s, d)
    mem2 = mem.reshape(b * mem.shape[1], d)
    ret_perm = _ret_perm()
    ssd_cols = _ssd_cols()[0]
    for l in range(depth):
        w_ret = w_in[l][:, ret_perm].astype(BF16)
        w_ssd = _take_cols(w_in[l], ssd_cols).astype(BF16)
        w_lru = w_in[l][:, _OFF_XR:].astype(BF16)
        p_ret, p_ssd, p_lru = _inproj(h, g_mix[l], w_ret, w_ssd, w_lru)
        o_ret = _retention(p_ret, b, s)
        o_ssd = _ssd(p_ssd, b, s, ssd_conv_w[l], ssd_conv_b[l], ssd_dt_bias[l], ssd_a_log[l], ssd_d[l], ssd_norm[l])
        o_lru = _lru(p_lru, b, s, lru_conv_w[l], lru_conv_b[l], lru_lambda[l], lru_wa[l], lru_ba[l], lru_wx[l], lru_bx[l])
        kv = _kv_proj(mem2, g_mem[l], ca_wkv[l].astype(BF16))
        h, h_tiles, ids, gates = _post_mixer(h, b, s, o_ret.reshape(b * s, -1), o_ssd.reshape(b * s, -1),
                                             o_lru.reshape(b * s, -1), *_outproj_weights(w_out[l]),
                                             g_ca[l], ca_wq[l].astype(BF16), kv, ca_wo[l].astype(BF16),
                                             g_moe[l], moe_wg[l], moe_bg[l], moe_we[l], moe_be[l])
        h = _moe(h, h_tiles, ids, gates, g_moe[l], moe_w1, moe_w3, moe_w2, l, g_final, l == depth - 1)
    return h.reshape(b, s, d)
```

```python
import functools
import math

import jax
import jax.numpy as jnp
import numpy as np
from jax import lax
from jax.experimental import pallas as pl
from jax.experimental.pallas import tpu as pltpu

D_MODEL = 1024
HEAD_DIM = 64
RET_WIDTH = 384
SSD_WIDTH = 384
LRU_WIDTH = 256
RET_HEADS = 6
SSD_HEADS = 6
SSD_GROUPS = 2
SSD_HPG = 3
SSD_STATE = 128
SSD_CONV_DIM = SSD_WIDTH + 2 * SSD_GROUPS * SSD_STATE
LRU_BLOCKS = 4
LRU_C = 8.0
CONV_WIDTH = 4
ROPE_BASE = 10000.0
CA_HEADS = 4
CA_HEAD_DIM = 256
N_GROUPS = 4
EXPERTS_PER_GROUP = 8
N_EXPERTS = 32
TOP_K = 2
D_EXPERT = 512
EPS = 1e-6

_OFF_Q, _OFF_K, _OFF_V, _OFF_G = 0, 384, 768, 1152
_OFF_Z = 1536
_OFF_XBC = 1920
_OFF_DT = 2816
_OFF_XR = 2822
_OFF_GATE = 3078
D_IN = 3334

LANE = 128
SUBLANE = 8
VMEM_LIMIT = 56 * 1024 * 1024

CHUNK = 128
ROW_TILE = 512
SSD_GW = 256

F32 = jnp.float32
BF16 = jnp.bfloat16


def _cparams(sem):
    return pltpu.CompilerParams(dimension_semantics=sem, vmem_limit_bytes=VMEM_LIMIT)


def _rms(x, g):
    return x * lax.rsqrt(jnp.mean(x * x, axis=-1, keepdims=True) + EPS) * g


def _project(h, g_ref, wr_ref, ws_ref, wl_ref, pr_ref, ps_ref, pq_ref):
    xn = _rms(h, g_ref[...]).astype(BF16)
    pr_ref[...] = jnp.dot(xn, wr_ref[...], preferred_element_type=F32)
    ps_ref[...] = jnp.dot(xn, ws_ref[...], preferred_element_type=F32)
    pq_ref[...] = jnp.dot(xn, wl_ref[...], preferred_element_type=F32)


def _inproj_kernel(x_ref, g_ref, wr_ref, ws_ref, wl_ref, pr_ref, ps_ref, pq_ref):
    _project(x_ref[...], g_ref, wr_ref, ws_ref, wl_ref, pr_ref, ps_ref, pq_ref)


def _combine_inproj_kernel(h_ref, y0_ref, y1_ref, gt_ref, g_ref, wr_ref, ws_ref, wl_ref,
                           ho_ref, pr_ref, ps_ref, pq_ref):
    h = _moe_combine(h_ref, y0_ref, y1_ref, gt_ref)
    ho_ref[...] = h
    _project(h, g_ref, wr_ref, ws_ref, wl_ref, pr_ref, ps_ref, pq_ref)


def _inproj(x2, g, w_ret, w_ssd, w_lru, moe=None):
    t, d = x2.shape
    tm = 256
    nt = t // tm
    n_r, n_s, n_l = w_ret.shape[1], w_ssd.shape[1], w_lru.shape[1]
    full = lambda n: pl.BlockSpec((d, n), lambda i: (0, 0))
    row = lambda n: pl.BlockSpec((tm, n), lambda i: (i, 0))
    weights = [pl.BlockSpec((1, d), lambda i: (0, 0)), full(n_r), full(n_s), full(n_l)]
    outs = [row(n_r), row(n_s), row(n_l)]
    shapes = [jax.ShapeDtypeStruct((t, n), F32) for n in (n_r, n_s, n_l)]
    if moe is None:
        return pl.pallas_call(
            _inproj_kernel, grid=(nt,), in_specs=[row(d)] + weights, out_specs=outs, out_shape=shapes,
            compiler_params=_cparams(("parallel",)), name="inproj",
        )(x2, g.reshape(1, d), w_ret, w_ssd, w_lru)
    y, gates = moe
    tiles = lambda off: pl.BlockSpec((tm * SUBLANE, LANE), lambda i: (off + i, 0))
    return pl.pallas_call(
        _combine_inproj_kernel, grid=(nt,),
        in_specs=[row(d), tiles(0), tiles(nt), row(TOP_K)] + weights,
        out_specs=[row(d)] + outs, out_shape=[jax.ShapeDtypeStruct((t, d), F32)] + shapes,
        compiler_params=_cparams(("parallel",)), name="combine_inproj",
    )(x2, y, y, gates[:TOP_K].T, g.reshape(1, d), w_ret, w_ssd, w_lru)


def _ret_kernel(q_ref, k_ref, v_ref, g_ref, cos_ref, sin_ref, d2_ref, dec_ref, gc_ref,
                o_ref, qs_ref, ks_ref, acc_ref):
    s = q_ref.shape[0]
    n = s // CHUNK
    lane = lax.broadcasted_iota(jnp.int32, (1, LANE), 1)
    mq0 = ((lane // 32) % 2 == 0).astype(F32)
    mq1 = 1.0 - mq0
    mv0 = (lane < HEAD_DIM).astype(F32)
    mv1 = 1.0 - mv0
    krow = lax.broadcasted_iota(jnp.int32, (LANE, LANE), 0)
    vcol = lax.broadcasted_iota(jnp.int32, (LANE, LANE), 1)
    bd = (((krow // 32) % 2) == (vcol // HEAD_DIM)).astype(F32)
    zeta_f = dec_ref[0]
    zeta_b = dec_ref[1]
    xi_f = dec_ref[2]
    xi_b = dec_ref[3]
    gc = gc_ref[...]
    d2 = d2_ref[...]

    def rot(x, c):
        rows = pl.ds(c * CHUNK, CHUNK)
        return x * cos_ref[rows, :] + pltpu.roll(x, 64, 1) * sin_ref[rows, :]

    def fwd(c, st):
        rows = pl.ds(pl.multiple_of(c * CHUNK, CHUNK), CHUNK)
        qc = rot(q_ref[rows, :], c)
        kc = rot(k_ref[rows, :], c) * (HEAD_DIM ** -0.5)
        vc = v_ref[rows, :]
        qs_ref[rows, :] = qc
        ks_ref[rows, :] = kc
        q2 = jnp.concatenate([qc * mq0, qc * mq1], axis=0)
        sc = lax.dot_general(q2, kc, (((1,), (1,)), ((), ())), preferred_element_type=F32) * d2
        o2 = jnp.dot(sc, vc, preferred_element_type=F32)
        o = o2[:CHUNK] * mv0 + o2[CHUNK:] * mv1
        o = o + jnp.dot(qc, st, preferred_element_type=F32) * xi_f
        acc_ref[rows, :] = o
        upd = jnp.dot(kc.T, vc * zeta_f, preferred_element_type=F32) * bd
        return st * gc + upd

    lax.fori_loop(0, n, fwd, jnp.zeros((LANE, LANE), F32), unroll=16)

    def bwd(i, st):
        c = n - 1 - i
        rows = pl.ds(pl.multiple_of(c * CHUNK, CHUNK), CHUNK)
        qc = qs_ref[rows, :]
        kc = ks_ref[rows, :]
        vc = v_ref[rows, :]
        o = acc_ref[rows, :] + jnp.dot(qc, st, preferred_element_type=F32) * xi_b
        oo = o * o
        ss0 = jnp.sum(oo * mv0, axis=1, keepdims=True)
        ss1 = jnp.sum(oo * mv1, axis=1, keepdims=True)
        scale = lax.rsqrt(ss0 * (1.0 / HEAD_DIM) + EPS) * mv0 + lax.rsqrt(ss1 * (1.0 / HEAD_DIM) + EPS) * mv1
        gate = g_ref[rows, :]
        gate = gate * jax.nn.sigmoid(gate)
        o_ref[rows, :] = (gate * (o * scale)).astype(o_ref.dtype)
        upd = jnp.dot(kc.T, vc * zeta_b, preferred_element_type=F32) * bd
        return st * gc + upd

    lax.fori_loop(0, n, bwd, jnp.zeros((LANE, LANE), F32), unroll=16)


def _ret_constants(s):
    half = HEAD_DIM // 2
    inv = ROPE_BASE ** (-jnp.arange(half, dtype=F32) / half)
    ang = jnp.arange(s, dtype=F32)[:, None] * inv[None, :]
    cos = jnp.cos(ang)
    sin = jnp.sin(ang)
    cos2 = jnp.concatenate([cos, cos, cos, cos], axis=1)
    sin2 = jnp.concatenate([-sin, -sin, sin, sin], axis=1)
    log_gamma = jnp.log1p(-jnp.exp2(-5.0 - jnp.arange(RET_HEADS, dtype=F32)))
    lg_pair = log_gamma.reshape(RET_HEADS // 2, 2)
    idx = jnp.arange(CHUNK, dtype=F32)
    adiff = jnp.abs(idx[:, None] - idx[None, :])
    d2 = jnp.exp(adiff[None, None] * lg_pair[:, :, None, None])
    d2 = d2.reshape(RET_HEADS // 2, 2 * CHUNK, CHUNK)
    lg_lane = jnp.repeat(lg_pair, HEAD_DIM, axis=1)
    pos = idx[None, :, None]
    lgl = lg_lane[:, None, :]
    dec = jnp.stack([jnp.exp((CHUNK - 1 - pos) * lgl), jnp.exp(pos * lgl),
                     jnp.exp((pos + 1) * lgl), jnp.exp((CHUNK - pos) * lgl)], axis=1)
    gc = jnp.exp(CHUNK * lg_lane)[:, None, :]
    return cos2, sin2, d2, dec, gc


def _retention(p_ret, b, s):
    p3 = p_ret.reshape(b, s, 4 * RET_WIDTH)
    cos2, sin2, d2, dec, gc = _ret_constants(s)
    npair = RET_HEADS // 2
    col = lambda off: pl.BlockSpec((None, s, LANE), lambda i, p: (i, 0, off + p))
    return pl.pallas_call(
        _ret_kernel,
        grid=(b, npair),
        in_specs=[col(0), col(npair), col(2 * npair), col(3 * npair),
                  pl.BlockSpec((s, LANE), lambda i, p: (0, 0)),
                  pl.BlockSpec((s, LANE), lambda i, p: (0, 0)),
                  pl.BlockSpec((None, 2 * CHUNK, CHUNK), lambda i, p: (p, 0, 0)),
                  pl.BlockSpec((None, 4, CHUNK, LANE), lambda i, p: (p, 0, 0, 0)),
                  pl.BlockSpec((None, 1, LANE), lambda i, p: (p, 0, 0))],
        out_specs=pl.BlockSpec((None, s, LANE), lambda i, p: (i, 0, p)),
        out_shape=jax.ShapeDtypeStruct((b, s, RET_WIDTH), BF16),
        scratch_shapes=[pltpu.VMEM((s, LANE), F32), pltpu.VMEM((s, LANE), F32), pltpu.VMEM((s, LANE), F32)],
        compiler_params=_cparams(("parallel", "parallel")),
        name="retention",
    )(p3, p3, p3, p3, cos2, sin2, d2, dec, gc)


_PAD = 8


def _softplus(x):
    return jnp.maximum(x, 0.0) + jnp.log1p(jnp.exp(-jnp.abs(x)))


def _silu(x):
    return x * jax.nn.sigmoid(x)


def _expand_heads(cols, base, lane_head):
    acc = jnp.zeros((cols.shape[0], SSD_GW), F32)
    for r in range(SSD_HPG):
        acc = jnp.where(lane_head == r, cols[:, base + r:base + r + 1], acc)
    return acc


def _dwconv_chunk(xp_ref, cw_ref, cb_ref, c):
    rows = CHUNK + 2 * _PAD
    win = xp_ref[pl.ds(pl.multiple_of(c * CHUNK, CHUNK), rows), :]
    acc = cb_ref[...]
    for t in range(CONV_WIDTH):
        shift = (CONV_WIDTH // 2 - t) % rows
        tap = win if shift == 0 else pltpu.roll(win, shift, 0)
        acc = acc + tap[_PAD:_PAD + CHUNK, :] * cw_ref[t:t + 1, :]
    return acc


def _ssd_kernel(xbc_ref, z_ref, dt_ref, cw_ref, cb_ref, dtb_ref, alog_ref, dsk_ref, nw_ref, o_ref,
                xp_ref, xc_ref, y_ref, pc_ref, dts_ref):
    s = z_ref.shape[0]
    n = s // CHUNK
    w = xbc_ref.shape[1]
    xp_ref[0:_PAD, :] = jnp.zeros((_PAD, w), F32)
    xp_ref[_PAD + s:2 * _PAD + s, :] = jnp.zeros((_PAD, w), F32)

    def copy(c, carry):
        rows = pl.ds(pl.multiple_of(c * CHUNK, CHUNK), CHUNK)
        xp_ref[pl.ds(pl.multiple_of(c * CHUNK + _PAD, _PAD), CHUNK), :] = xbc_ref[rows, :]
        return carry

    lax.fori_loop(0, n, copy, 0)

    lane_head = lax.broadcasted_iota(jnp.int32, (1, SSD_GW), 1) // HEAD_DIM
    ii = lax.broadcasted_iota(jnp.int32, (CHUNK, CHUNK), 0)
    jj = lax.broadcasted_iota(jnp.int32, (CHUNK, CHUNK), 1)
    lower = ii >= jj
    tri = lower.astype(F32)
    a_neg = -jnp.exp(alog_ref[...])
    dt_bias = dtb_ref[...]

    def fwd(c, hf):
        rows = pl.ds(pl.multiple_of(c * CHUNK, CHUNK), CHUNK)
        xc = _silu(_dwconv_chunk(xp_ref, cw_ref, cb_ref, c))
        xc_ref[rows, :] = xc
        xs = xc[:, 0:SSD_GW]
        bm = xc[:, SSD_GW:SSD_GW + SSD_STATE]
        cm = xc[:, SSD_GW + SSD_STATE:]
        dt = _softplus(dt_ref[rows, :] + dt_bias)
        dts_ref[rows, :] = dt
        la = dt * a_neg
        p = jnp.dot(tri, la, precision=lax.Precision.HIGHEST, preferred_element_type=F32)
        pc_ref[rows, :] = p
        e = p - la
        pt = p.T
        et = e.T
        dtt = dt.T
        g = lax.dot_general(cm, bm, (((1,), (1,)), ((), ())), preferred_element_type=F32)
        y = jnp.zeros((CHUNK, SSD_GW), F32)
        for r in range(SSD_HPG):
            arg = jnp.where(lower, p[:, r:r + 1] - pt[r:r + 1, :], et[3 + r:4 + r, :] - e[:, 3 + r:4 + r])
            dtj = jnp.where(lower, dtt[r:r + 1, :], dtt[3 + r:4 + r, :])
            m = g * (jnp.exp(arg) * dtj)
            y = y + jnp.dot(m, jnp.where(lane_head == r, xs, 0.0), preferred_element_type=F32)
        pf = _expand_heads(p, 0, lane_head)
        pf_last = _expand_heads(p[CHUNK - 1:CHUNK, :], 0, lane_head)
        dtf = _expand_heads(dt, 0, lane_head)
        y = y + jnp.dot(cm, hf, preferred_element_type=F32) * jnp.exp(pf)
        y_ref[rows, :] = y
        upd = jnp.dot(bm.T, xs * (jnp.exp(pf_last - pf) * dtf), preferred_element_type=F32)
        return hf * jnp.exp(pf_last) + upd

    lax.fori_loop(0, n, fwd, jnp.zeros((SSD_STATE, SSD_GW), F32), unroll=2)

    dsk = dsk_ref[...]
    nw = nw_ref[...]

    def bwd(i, hb):
        c = n - 1 - i
        rows = pl.ds(pl.multiple_of(c * CHUNK, CHUNK), CHUNK)
        xs = xc_ref[rows, 0:SSD_GW]
        bm = xc_ref[rows, SSD_GW:SSD_GW + SSD_STATE]
        cm = xc_ref[rows, SSD_GW + SSD_STATE:]
        dt = dts_ref[rows, :]
        p = pc_ref[rows, :]
        e = p - dt * a_neg
        eb = _expand_heads(e, 3, lane_head)
        tb = _expand_heads(p[CHUNK - 1:CHUNK, :], 3, lane_head)
        dtb = _expand_heads(dt, 3, lane_head)
        y = y_ref[rows, :] + jnp.dot(cm, hb, preferred_element_type=F32) * jnp.exp(tb - eb)
        y = (y + dsk * xs) * _silu(z_ref[rows, :])
        ms = jnp.sum(y * y, axis=1, keepdims=True) * (1.0 / (SSD_HPG * HEAD_DIM))
        o_ref[rows, :] = (y * lax.rsqrt(ms + EPS) * nw).astype(o_ref.dtype)
        upd = jnp.dot(bm.T, xs * (jnp.exp(eb) * dtb), preferred_element_type=F32)
        return hb * jnp.exp(tb) + upd

    lax.fori_loop(0, n, bwd, jnp.zeros((SSD_STATE, SSD_GW), F32), unroll=2)


_SSD_XBC_W = SSD_GW + 2 * SSD_STATE
_SSD_SLAB = SSD_GROUPS * (_SSD_XBC_W + SSD_GW + LANE)


def _ssd_xbc_cols(a, off):
    gw = SSD_HPG * HEAD_DIM
    zero = jnp.zeros(a.shape[:-1] + (SSD_GW - gw,), a.dtype)
    out = []
    for g in range(SSD_GROUPS):
        b0 = off + SSD_WIDTH + g * SSD_STATE
        c0 = b0 + SSD_GROUPS * SSD_STATE
        out += [a[..., off + g * gw:off + (g + 1) * gw], zero, a[..., b0:b0 + SSD_STATE], a[..., c0:c0 + SSD_STATE]]
    return jnp.concatenate(out, axis=-1)


def _ssd_slab_cols(w):
    gw = SSD_HPG * HEAD_DIM
    zero = lambda k: jnp.zeros((w.shape[0], k), w.dtype)
    z = [p for g in range(SSD_GROUPS) for p in (w[:, _OFF_Z + g * gw:_OFF_Z + (g + 1) * gw], zero(SSD_GW - gw))]
    dt = []
    for g in range(SSD_GROUPS):
        heads = w[:, _OFF_DT + g * SSD_HPG:_OFF_DT + (g + 1) * SSD_HPG]
        dt += [heads, heads, zero(LANE - 2 * SSD_HPG)]
    return jnp.concatenate([_ssd_xbc_cols(w, _OFF_XBC)] + z + dt, axis=1)


def _ssd(p_ssd, b, s, conv_w, conv_b, dt_bias, a_log, d_skip, norm_w):
    p3 = p_ssd.reshape(b, s, _SSD_SLAB)
    cw = _ssd_xbc_cols(conv_w, 0)
    cb = _ssd_xbc_cols(conv_b[None, :], 0)
    gw = SSD_HPG * HEAD_DIM
    zpad = jnp.zeros((SSD_GROUPS, LANE - 2 * SSD_HPG), F32)
    grp = lambda v: jnp.concatenate([v[0].reshape(SSD_GROUPS, SSD_HPG), v[1].reshape(SSD_GROUPS, SSD_HPG), zpad], axis=1)
    dtb = grp(dt_bias)[:, None, :]
    alog = grp(a_log)[:, None, :]
    lanes = lambda v: jnp.pad(v.reshape(SSD_GROUPS, gw), ((0, 0), (0, SSD_GW - gw)))[:, None, :]
    dsk = lanes(jnp.repeat(d_skip, HEAD_DIM))
    nw = lanes(norm_w)
    nxb = _SSD_XBC_W // LANE
    grid_spec = dict(
        grid=(b, SSD_GROUPS),
        in_specs=[pl.BlockSpec((None, s, _SSD_XBC_W), lambda i, g: (i, 0, g)),
                  pl.BlockSpec((None, s, SSD_GW), lambda i, g: (i, 0, SSD_GROUPS * _SSD_XBC_W // SSD_GW + g)),
                  pl.BlockSpec((None, s, LANE), lambda i, g: (i, 0, SSD_GROUPS * (nxb + SSD_GW // LANE) + g)),
                  pl.BlockSpec((CONV_WIDTH, _SSD_XBC_W), lambda i, g: (0, g)),
                  pl.BlockSpec((1, _SSD_XBC_W), lambda i, g: (0, g)),
                  pl.BlockSpec((None, 1, LANE), lambda i, g: (g, 0, 0)),
                  pl.BlockSpec((None, 1, LANE), lambda i, g: (g, 0, 0)),
                  pl.BlockSpec((None, 1, SSD_GW), lambda i, g: (g, 0, 0)),
                  pl.BlockSpec((None, 1, SSD_GW), lambda i, g: (g, 0, 0))],
        out_specs=pl.BlockSpec((None, s, SSD_GW), lambda i, g: (i, 0, g)),
    )
    return pl.pallas_call(
        _ssd_kernel,
        out_shape=jax.ShapeDtypeStruct((b, s, SSD_GROUPS * SSD_GW), BF16),
        scratch_shapes=[pltpu.VMEM((s + 2 * _PAD, _SSD_XBC_W), F32), pltpu.VMEM((s, _SSD_XBC_W), F32),
                        pltpu.VMEM((s, SSD_GW), F32), pltpu.VMEM((s, LANE), F32), pltpu.VMEM((s, LANE), F32)],
        compiler_params=_cparams(("parallel", "parallel")),
        name="ssd",
        **grid_spec,
    )(p3, p3, p3, cw, cb, dtb, alog, dsk, nw)


def _scan_chunk(a, u, h, reverse):
    n_groups = CHUNK // SUBLANE
    width = a.shape[1]
    a = a.reshape(n_groups, SUBLANE, width)
    u = u.reshape(n_groups, SUBLANE, width)
    sub = lax.broadcasted_iota(jnp.int32, (1, SUBLANE, 1), 1)
    d = 1
    while d < SUBLANE:
        keep = (sub < SUBLANE - d) if reverse else (sub >= d)
        shift = SUBLANE - d if reverse else d
        a_sh = jnp.where(keep, pltpu.roll(a, shift, 1), 1.0)
        u_sh = jnp.where(keep, pltpu.roll(u, shift, 1), 0.0)
        u = a * u_sh + u
        a = a * a_sh
        d *= 2
    a = a.reshape(CHUNK, width)
    u = u.reshape(CHUNK, width)
    out = [None] * n_groups
    for g in (range(n_groups - 1, -1, -1) if reverse else range(n_groups)):
        rows = slice(g * SUBLANE, (g + 1) * SUBLANE)
        hg = u[rows] + a[rows] * h
        out[g] = hg
        h = hg[0:1] if reverse else hg[SUBLANE - 1:SUBLANE]
    return jnp.concatenate(out, axis=0), h


def _lru_kernel(x_ref, cw_ref, cb_ref, wg_ref, bg_ref, lam_ref, o_ref, xp_ref, hf_ref, ab_ref, ub_ref):
    s = x_ref.shape[0]
    n = s // CHUNK
    w = LRU_WIDTH
    xp_ref[0:_PAD, :] = jnp.zeros((_PAD, w), F32)
    xp_ref[_PAD + s:2 * _PAD + s, :] = jnp.zeros((_PAD, w), F32)

    def copy(c, carry):
        rows = pl.ds(pl.multiple_of(c * CHUNK, CHUNK), CHUNK)
        xp_ref[pl.ds(pl.multiple_of(c * CHUNK + _PAD, _PAD), CHUNK), :] = x_ref[rows, 0:w]
        return carry

    lax.fori_loop(0, n, copy, 0)
    nsp = -LRU_C * _softplus(-lam_ref[...])

    def gates(pre, xc, k):
        r = jax.nn.sigmoid(pre[:, 2 * k * w:(2 * k + 1) * w])
        i = jax.nn.sigmoid(pre[:, (2 * k + 1) * w:(2 * k + 2) * w])
        log_a = r * nsp[k:k + 1, :]
        a = jnp.exp(log_a)
        return a, jnp.sqrt(-jnp.tanh(log_a) * (a * a + 1.0)) * (i * xc)

    def fwd(c, h):
        rows = pl.ds(pl.multiple_of(c * CHUNK, CHUNK), CHUNK)
        xc = _dwconv_chunk(xp_ref, cw_ref, cb_ref, c)
        pre = jnp.dot(xc.astype(BF16), wg_ref[...], preferred_element_type=F32) + bg_ref[...]
        a_b, u_b = gates(pre, xc, 1)
        ab_ref[rows, :] = a_b
        ub_ref[rows, :] = u_b
        a_f, u_f = gates(pre, xc, 0)
        hc, h = _scan_chunk(a_f, u_f, h, False)
        hf_ref[rows, :] = hc
        return h

    lax.fori_loop(0, n, fwd, jnp.zeros((1, w), F32))

    def bwd(i, h):
        c = n - 1 - i
        rows = pl.ds(pl.multiple_of(c * CHUNK, CHUNK), CHUNK)
        hc, h = _scan_chunk(ab_ref[rows, :], ub_ref[rows, :], h, True)
        gate = x_ref[rows, w:2 * w]
        o_ref[rows, :] = ((hf_ref[rows, :] + hc) * jax.nn.gelu(gate)).astype(o_ref.dtype)
        return h

    lax.fori_loop(0, n, bwd, jnp.zeros((1, w), F32))


def _block_diag(wblk):
    eye = jnp.eye(LRU_BLOCKS, dtype=wblk.dtype)
    return jnp.einsum('hij,hk->hikj', wblk, eye).reshape(LRU_WIDTH, LRU_WIDTH)


def _lru(p_lru, b, s, conv_w, conv_b, lam, wa, ba, wx, bx):
    p3 = p_lru.reshape(b, s, 2 * LRU_WIDTH)
    wg = jnp.concatenate([_block_diag(wa[0]), _block_diag(wx[0]), _block_diag(wa[1]), _block_diag(wx[1])], axis=1).astype(BF16)
    bg = jnp.concatenate([ba[0], bx[0], ba[1], bx[1]])[None, :]
    w = LRU_WIDTH
    const = lambda shape: pl.BlockSpec(shape, lambda i: (0,) * len(shape))
    return pl.pallas_call(
        _lru_kernel,
        grid=(b,),
        in_specs=[pl.BlockSpec((None, s, 2 * w), lambda i: (i, 0, 0)),
                  const((CONV_WIDTH, w)), const((1, w)), const((w, 4 * w)), const((1, 4 * w)), const((2, w))],
        out_specs=pl.BlockSpec((None, s, w), lambda i: (i, 0, 0)),
        out_shape=jax.ShapeDtypeStruct((b, s, w), BF16),
        scratch_shapes=[pltpu.VMEM((s + 2 * _PAD, w), F32), pltpu.VMEM((s, w), F32),
                        pltpu.VMEM((s, w), F32), pltpu.VMEM((s, w), F32)],
        compiler_params=_cparams(("parallel",)),
        name="rglru",
    )(p3, conv_w, conv_b[None, :], wg, bg, lam)


def _outproj_weights(w_out):
    gw = SSD_HPG * HEAD_DIM
    zero = jnp.zeros((SSD_GW - gw, w_out.shape[1]), w_out.dtype)
    rows = [p for g in range(SSD_GROUPS) for p in (w_out[RET_WIDTH + g * gw:RET_WIDTH + (g + 1) * gw], zero)]
    w_ssd = jnp.concatenate(rows, axis=0)
    return (w_out[:RET_WIDTH].astype(BF16), w_ssd.astype(BF16), w_out[RET_WIDTH + SSD_WIDTH:].astype(BF16))


def _kv_kernel(m_ref, g_ref, w_ref, o_ref):
    mn = _rms(m_ref[...], g_ref[...]).astype(BF16)
    o_ref[...] = jnp.dot(mn, w_ref[...], preferred_element_type=F32).astype(o_ref.dtype)


def _kv_proj(mem2, g, wkv):
    t, d = mem2.shape
    n = wkv.shape[1]
    tm = 256
    return pl.pallas_call(
        _kv_kernel,
        grid=(t // tm,),
        in_specs=[pl.BlockSpec((tm, d), lambda i: (i, 0)), pl.BlockSpec((1, d), lambda i: (0, 0)),
                  pl.BlockSpec((d, n), lambda i: (0, 0))],
        out_specs=pl.BlockSpec((tm, n), lambda i: (i, 0)),
        out_shape=jax.ShapeDtypeStruct((t, n), BF16),
        compiler_params=_cparams(("parallel",)),
        name="kvproj",
    )(mem2, g.reshape(1, d), wkv)


def _xattn_tile(h, g, wq_ref, k_ref, v_ref, wo_ref):
    xn = _rms(h, g).astype(BF16)
    q = jnp.dot(xn, wq_ref[...], preferred_element_type=F32).astype(BF16)
    outs = []
    for hd in range(CA_HEADS):
        cols = slice(hd * CA_HEAD_DIM, (hd + 1) * CA_HEAD_DIM)
        sc = lax.dot_general(q[:, cols], k_ref[:, cols], (((1,), (1,)), ((), ())), preferred_element_type=F32)
        sc = sc * (CA_HEAD_DIM ** -0.5)
        e = jnp.exp(sc - jnp.max(sc, axis=-1, keepdims=True))
        p = e / jnp.sum(e, axis=-1, keepdims=True)
        outs.append(jnp.dot(p.astype(BF16), v_ref[:, cols], preferred_element_type=F32).astype(BF16))
    o = jnp.concatenate(outs, axis=-1)
    return h + jnp.dot(o, wo_ref[...], preferred_element_type=F32)


_ROUTER_ROWS = SUBLANE + N_EXPERTS


def _first_argmax(v, row):
    m = jnp.max(v, axis=0, keepdims=True)
    return m, jnp.min(jnp.where(v == m, row, SUBLANE), axis=0, keepdims=True)


def _router_tile(h, g, w_ref, b_ref):
    xn = _rms(h, g).astype(BF16)
    lg = lax.dot_general(w_ref[...], xn, (((1,), (1,)), ((), ())), preferred_element_type=F32) + b_ref[...]
    tm = lg.shape[1]
    row = lax.broadcasted_iota(jnp.int32, (SUBLANE, tm), 0)
    gl = jnp.where(row < N_GROUPS, lg[0:SUBLANE], -jnp.inf)
    ge = jnp.exp(gl - jnp.max(gl, axis=0, keepdims=True))
    gp_all = ge / jnp.sum(ge, axis=0, keepdims=True)
    gp, gi = _first_argmax(gp_all, row)
    el = lg[SUBLANE:2 * SUBLANE]
    for g in range(1, N_GROUPS):
        el = jnp.where(gi == g, lg[SUBLANE * (g + 1):SUBLANE * (g + 2)], el)
    ee = jnp.exp(el - jnp.max(el, axis=0, keepdims=True))
    ep = ee / jnp.sum(ee, axis=0, keepdims=True)
    v1, i1 = _first_argmax(ep, row)
    v2, i2 = _first_argmax(jnp.where(row == i1, -1.0, ep), row)
    den = v1 + v2
    ids = jnp.where(row == 0, gi * EXPERTS_PER_GROUP + i1, jnp.where(row == 1, gi * EXPERTS_PER_GROUP + i2, 0))
    gates = jnp.where(row == 0, gp * v1 / den, jnp.where(row == 1, gp * v2 / den, 0.0))
    return ids, gates


assert D_MODEL == SUBLANE * LANE


def _token_tile(r):
    return pl.ds(pl.multiple_of(r * SUBLANE, SUBLANE), SUBLANE)


def _rows_to_tiles(ref, x):
    for j in range(SUBLANE):
        ref[pl.ds(j, x.shape[0], stride=SUBLANE), :] = x[:, j * LANE:(j + 1) * LANE]


def _tiles_to_rows(ref, n):
    return jnp.concatenate([ref[pl.ds(j, n, stride=SUBLANE), :] for j in range(SUBLANE)], axis=1)


def _post_kernel(h_ref, a_ref, b_ref, c_ref, wa_ref, wb_ref, wc_ref, gca_ref, wq_ref, k_ref, v_ref, wo_ref,
                 gmoe_ref, wr_ref, br_ref, o_ref, ot_ref, ids_ref, gates_ref):
    h = h_ref[...]
    h = h + jnp.dot(a_ref[...], wa_ref[...], preferred_element_type=F32)
    h = h + jnp.dot(b_ref[...], wb_ref[...], preferred_element_type=F32)
    h = h + jnp.dot(c_ref[...], wc_ref[...], preferred_element_type=F32)
    h = _xattn_tile(h, gca_ref[...], wq_ref, k_ref, v_ref, wo_ref)
    o_ref[...] = h
    _rows_to_tiles(ot_ref, h)
    ids, gates = _router_tile(h, gmoe_ref[...], wr_ref, br_ref)
    ids_ref[...] = ids
    gates_ref[...] = gates


def _post_mixer(h2, b, s, o_ret, o_ssd, o_lru, w_ret, w_ssd, w_lru, g_ca, wq, kv, wo, g_moe, wg, bg, we, be):
    t, d = h2.shape
    m = kv.shape[0] // b
    ts = ROW_TILE
    nts = s // ts
    zrow = jnp.zeros((SUBLANE - N_GROUPS, d), F32)
    wr = jnp.concatenate([wg.T, zrow, we.T], axis=0).astype(BF16)
    br = jnp.concatenate([bg, jnp.zeros((SUBLANE - N_GROUPS,), F32), be])[:, None]
    row = lambda n: pl.BlockSpec((ts, n), lambda i, j: (i * nts + j, 0))
    const = lambda r, c: pl.BlockSpec((r, c), lambda i, j: (0, 0))
    ka, kb, kc = o_ret.shape[1], o_ssd.shape[1], o_lru.shape[1]
    lanes = pl.BlockSpec((SUBLANE, ts), lambda i, j: (0, i * nts + j))
    return pl.pallas_call(
        _post_kernel,
        grid=(b, nts),
        in_specs=[row(d), row(ka), row(kb), row(kc), const(ka, d), const(kb, d), const(kc, d),
                  const(1, d), const(d, d),
                  pl.BlockSpec((m, d), lambda i, j: (i, 0)),
                  pl.BlockSpec((m, d), lambda i, j: (i, 1)),
                  const(d, d), const(1, d), const(_ROUTER_ROWS, d), const(_ROUTER_ROWS, 1)],
        out_specs=[row(d), pl.BlockSpec((ts * SUBLANE, LANE), lambda i, j: (i * nts + j, 0)), lanes, lanes],
        out_shape=[jax.ShapeDtypeStruct((t, d), F32), jax.ShapeDtypeStruct((t * SUBLANE, LANE), F32),
                   jax.ShapeDtypeStruct((SUBLANE, t), jnp.int32), jax.ShapeDtypeStruct((SUBLANE, t), F32)],
        compiler_params=_cparams(("parallel", "parallel")),
        name="post_mixer",
    )(h2, o_ret, o_ssd, o_lru, w_ret, w_ssd, w_lru, g_ca.reshape(1, d), wq, kv, kv, wo,
      g_moe.reshape(1, d), wr, br)


MOE_BLK = 256


_ASSIGN_BITS = 17


def _dispatch_plan(ids, t):
    n_assign = t * TOP_K
    assert n_assign <= 1 << _ASSIGN_BITS
    e = ids[:TOP_K].reshape(n_assign)
    key =jnp.left_shift(e, _ASSIGN_BITS) | jnp.arange(n_assign, dtype=jnp.int32)
    order = jnp.sort(key) & ((1 << _ASSIGN_BITS) - 1)
    order = jnp.concatenate([order, jnp.zeros((MOE_BLK,), jnp.int32)])
    experts = jnp.arange(N_EXPERTS, dtype=jnp.int32)
    counts = jnp.sum((e[None, :] == experts[:, None]).astype(jnp.int32), axis=1)
    start = jnp.cumsum(counts) - counts
    nb = (counts + MOE_BLK - 1) // MOE_BLK
    blk_end = jnp.cumsum(nb)
    n_blocks = n_assign // MOE_BLK + N_EXPERTS
    j = jnp.arange(n_blocks, dtype=jnp.int32)
    be = jnp.minimum(jnp.sum((blk_end[None, :] <= j[:, None]).astype(jnp.int32), axis=1), N_EXPERTS - 1)
    onehot = (be[:, None] == experts[None, :]).astype(jnp.int32)
    pick = lambda v: jnp.sum(onehot * v[None, :], axis=1)
    r = j - pick(blk_end - nb)
    row_start = jnp.clip(pick(start) + r * MOE_BLK, 0, n_assign)
    n_rows = jnp.clip(pick(counts) - r * MOE_BLK, 0, MOE_BLK)
    tok = jnp.where(order >= t, order - t, order)
    rr = jnp.arange(MOE_BLK, dtype=jnp.int32)[None, :]
    dump = n_assign + (j % 2)[:, None] * MOE_BLK + rr
    dest = jnp.where(rr < n_rows[:, None], order[row_start[:, None] + rr], dump).reshape(n_blocks * MOE_BLK)
    return tok, dest, be, row_start, blk_end[-1:].astype(jnp.int32)


def _expert_kernel(be_ref, rs_ref, nu_ref, tok_ref, dst_ref, h_hbm, g_ref, w1_ref, w3_ref, w2_ref, y_hbm,
                   xbuf, ybuf, wb1, wb3, wb2, gsem, ssem, *, n_tokens):
    n_assign = TOP_K * n_tokens
    blk_rows = MOE_BLK * SUBLANE
    i = pl.program_id(0)
    n_used = nu_ref[0]
    slot = i % 2

    def gather(blk, s):
        base = rs_ref[blk]

        def body(r, carry):
            tok = tok_ref[base + r]
            pltpu.make_async_copy(h_hbm.at[_token_tile(tok), :], xbuf.at[s, _token_tile(r), :], gsem.at[s]).start()
            return carry
        lax.fori_loop(0, MOE_BLK, body, 0, unroll=8)

    def scatter(blk, s):
        base = blk * MOE_BLK

        def body(r, carry):
            dest = dst_ref[base + r]
            pltpu.make_async_copy(ybuf.at[s, _token_tile(r), :], y_hbm.at[_token_tile(dest), :], ssem.at[s]).start()
            return carry
        lax.fori_loop(0, MOE_BLK, body, 0, unroll=8)

    def wait_gather(s):
        pltpu.make_async_copy(h_hbm.at[pl.ds(0, blk_rows), :], xbuf.at[s], gsem.at[s]).wait()

    def wait_scatter(s):
        pltpu.make_async_copy(ybuf.at[s], y_hbm.at[pl.ds(0, blk_rows), :], ssem.at[s]).wait()

    @pl.when(i == 0)
    def _():
        ybuf[0] = jnp.zeros((blk_rows, LANE), F32)
        for s in range(2):
            dump = pl.ds((n_assign + s * MOE_BLK) * SUBLANE, blk_rows)
            fill = pltpu.make_async_copy(ybuf.at[0], y_hbm.at[dump, :], ssem.at[0])
            fill.start()
            fill.wait()
        gather(0, 0)

    @pl.when(i + 1 < n_used)
    def _():
        gather(i + 1, 1 - slot)

    @pl.when(i < n_used)
    def _():
        @pl.when((i == 0) | (be_ref[i] != be_ref[jnp.maximum(i - 1, 0)]))
        def _():
            wb1[...] = w1_ref[...].astype(BF16)
            wb3[...] = w3_ref[...].astype(BF16)
            wb2[...] = w2_ref[...].astype(BF16)

        wait_gather(slot)

        @pl.when(i >= 2)
        def _():
            wait_scatter(slot)

        xn = _rms(_tiles_to_rows(xbuf.at[slot], MOE_BLK), g_ref[...]).astype(BF16)
        h1 = jnp.dot(xn, wb1[...], preferred_element_type=F32)
        h3 = jnp.dot(xn, wb3[...], preferred_element_type=F32)
        hid = (_silu(h1) * h3).astype(BF16)
        _rows_to_tiles(ybuf.at[slot], jnp.dot(hid, wb2[...], preferred_element_type=F32))
        scatter(i, slot)

        @pl.when(i == n_used - 1)
        def _():
            wait_scatter(slot)

            @pl.when(i >= 1)
            def _():
                wait_scatter(1 - slot)


def _experts(h_tiles, g, tok, dest, block_e, row_start, n_used, w1, w3, w2, layer):
    t, d = h_tiles.shape[0] // SUBLANE, D_MODEL
    n_assign = t * TOP_K
    n_blocks = block_e.shape[0]
    de = w1.shape[3]
    wmap = lambda i, be, *_: (layer, be[i], 0, 0)
    grid_spec = pltpu.PrefetchScalarGridSpec(
        num_scalar_prefetch=5,
        grid=(n_blocks,),
        in_specs=[pl.BlockSpec(memory_space=pl.ANY),
                  pl.BlockSpec((1, d), lambda i, *_: (0, 0)),
                  pl.BlockSpec((None, None, d, de), wmap),
                  pl.BlockSpec((None, None, d, de), wmap),
                  pl.BlockSpec((None, None, de, d), wmap)],
        out_specs=pl.BlockSpec(memory_space=pl.ANY),
        scratch_shapes=[pltpu.VMEM((2, MOE_BLK * SUBLANE, LANE), F32), pltpu.VMEM((2, MOE_BLK * SUBLANE, LANE), F32),
                        pltpu.VMEM((d, de), BF16), pltpu.VMEM((d, de), BF16), pltpu.VMEM((de, d), BF16),
                        pltpu.SemaphoreType.DMA((2,)), pltpu.SemaphoreType.DMA((2,))],
    )
    return pl.pallas_call(
        functools.partial(_expert_kernel, n_tokens=t),
        grid_spec=grid_spec,
        out_shape=jax.ShapeDtypeStruct(((n_assign + 2 * MOE_BLK) * SUBLANE, LANE), F32),
        compiler_params=_cparams(("arbitrary",)),
        name="experts",
    )(block_e, row_start, n_used, tok, dest, h_tiles, g.reshape(1, d), w1, w3, w2)


def _moe_combine(h_ref, y0_ref, y1_ref, gt_ref):
    gt = gt_ref[...]
    tm = h_ref.shape[0]
    return h_ref[...] + _tiles_to_rows(y0_ref, tm) * gt[:, 0:1] + _tiles_to_rows(y1_ref, tm) * gt[:, 1:2]


def _combine_kernel(h_ref, y0_ref, y1_ref, gt_ref, g_ref, o_ref):
    o_ref[...] = _rms(_moe_combine(h_ref, y0_ref, y1_ref, gt_ref), g_ref[...])


def _combine_final(h2, y, gates, g_final):
    t, d = h2.shape
    tm = ROW_TILE
    nt = t // tm
    return pl.pallas_call(
        _combine_kernel,
        grid=(nt,),
        in_specs=[pl.BlockSpec((tm, d), lambda i: (i, 0)),
                  pl.BlockSpec((tm * SUBLANE, LANE), lambda i: (i, 0)),
                  pl.BlockSpec((tm * SUBLANE, LANE), lambda i: (nt + i, 0)),
                  pl.BlockSpec((tm, TOP_K), lambda i: (i, 0)),
                  pl.BlockSpec((1, d), lambda i: (0, 0))],
        out_specs=pl.BlockSpec((tm, d), lambda i: (i, 0)),
        out_shape=jax.ShapeDtypeStruct((t, d), F32),
        compiler_params=_cparams(("parallel",)),
        name="combine",
    )(h2, y, y, gates[:TOP_K].T, g_final.reshape(1, d))


def _moe_experts(h_tiles, ids, g, w1, w3, w2, layer):
    t = h_tiles.shape[0] // SUBLANE
    tok, dest, block_e, row_start, n_used = _dispatch_plan(ids, t)
    return _experts(h_tiles, g, tok, dest, block_e, row_start, n_used, w1, w3, w2, layer)


def _ret_cols(w):
    d = w.shape[0]
    pairs = lambda c: c.reshape(d, RET_HEADS // 2, 2, 2, HEAD_DIM // 2).transpose(0, 1, 3, 2, 4).reshape(d, RET_WIDTH)
    return jnp.concatenate([pairs(w[:, _OFF_Q:_OFF_K]), pairs(w[:, _OFF_K:_OFF_V]), w[:, _OFF_V:_OFF_Z]], axis=1)


def kernel(x, mem, g_mix, w_in, w_out, ssd_conv_w, ssd_conv_b, ssd_dt_bias, ssd_a_log, ssd_d, ssd_norm, lru_conv_w, lru_conv_b, lru_lambda, lru_wa, lru_ba, lru_wx, lru_bx, g_ca, g_mem, ca_wq, ca_wkv, ca_wo, g_moe, moe_wg, moe_bg, moe_we, moe_be, moe_w1, moe_w3, moe_w2, g_final):
    b, s, d = x.shape
    depth = g_mix.shape[0]
    h = x.reshape(b * s, d)
    mem2 = mem.reshape(b * mem.shape[1], d)
    moe = None
    for l in range(depth):
        w_ret = _ret_cols(w_in[l]).astype(BF16)
        w_ssd = _ssd_slab_cols(w_in[l]).astype(BF16)
        w_lru = w_in[l][:, _OFF_XR:].astype(BF16)
        if moe is None:
            p_ret, p_ssd, p_lru = _inproj(h, g_mix[l], w_ret, w_ssd, w_lru)
        else:
            h, p_ret, p_ssd, p_lru = _inproj(h, g_mix[l], w_ret, w_ssd, w_lru, moe)
        o_ret = _retention(p_ret, b, s)
        o_ssd = _ssd(p_ssd, b, s, ssd_conv_w[l], ssd_conv_b[l], ssd_dt_bias[l], ssd_a_log[l], ssd_d[l], ssd_norm[l])
        o_lru = _lru(p_lru, b, s, lru_conv_w[l], lru_conv_b[l], lru_lambda[l], lru_wa[l], lru_ba[l], lru_wx[l], lru_bx[l])
        kv = _kv_proj(mem2, g_mem[l], ca_wkv[l].astype(BF16))
        h, h_tiles, ids, gates = _post_mixer(h, b, s, o_ret.reshape(b * s, -1), o_ssd.reshape(b * s, -1),
                                             o_lru.reshape(b * s, -1), *_outproj_weights(w_out[l]),
                                             g_ca[l], ca_wq[l].astype(BF16), kv, ca_wo[l].astype(BF16),
                                             g_moe[l], moe_wg[l], moe_bg[l], moe_we[l], moe_be[l])
        moe = (_moe_experts(h_tiles, ids, g_moe[l], moe_w1, moe_w3, moe_w2, l), gates)
    return _combine_final(h, *moe, g_final).reshape(b, s, d)
```

```python
import functools
import math

import jax
import jax.numpy as jnp
import numpy as np
from jax import lax
from jax.experimental import pallas as pl
from jax.experimental.pallas import tpu as pltpu

D_MODEL = 1024
HEAD_DIM = 64
RET_WIDTH = 384
SSD_WIDTH = 384
LRU_WIDTH = 256
RET_HEADS = 6
SSD_HEADS = 6
SSD_GROUPS = 2
SSD_HPG = 3
SSD_STATE = 128
SSD_CONV_DIM = SSD_WIDTH + 2 * SSD_GROUPS * SSD_STATE
LRU_BLOCKS = 4
LRU_C = 8.0
CONV_WIDTH = 4
ROPE_BASE = 10000.0
CA_HEADS = 4
CA_HEAD_DIM = 256
N_GROUPS = 4
EXPERTS_PER_GROUP = 8
N_EXPERTS = 32
TOP_K = 2
D_EXPERT = 512
EPS = 1e-6

_OFF_Q, _OFF_K, _OFF_V, _OFF_G = 0, 384, 768, 1152
_OFF_Z = 1536
_OFF_XBC = 1920
_OFF_DT = 2816
_OFF_XR = 2822
_OFF_GATE = 3078
D_IN = 3334

LANE = 128
SUBLANE = 8
VMEM_LIMIT = 56 * 1024 * 1024

CHUNK = 128
ROW_TILE = 512
SSD_GW = 256

F32 = jnp.float32
BF16 = jnp.bfloat16


def _cparams(sem):
    return pltpu.CompilerParams(dimension_semantics=sem, vmem_limit_bytes=VMEM_LIMIT)


def _rms(x, g):
    return x * lax.rsqrt(jnp.mean(x * x, axis=-1, keepdims=True) + EPS) * g


def _project(h, g_ref, wr_ref, ws_ref, wl_ref, pr_ref, ps_ref, pq_ref):
    xn = _rms(h, g_ref[...]).astype(BF16)
    pr_ref[...] = jnp.dot(xn, wr_ref[...], preferred_element_type=F32)
    ps_ref[...] = jnp.dot(xn, ws_ref[...], preferred_element_type=F32)
    pq_ref[...] = jnp.dot(xn, wl_ref[...], preferred_element_type=F32)


def _inproj_kernel(x_ref, g_ref, wr_ref, ws_ref, wl_ref, pr_ref, ps_ref, pq_ref):
    _project(x_ref[...], g_ref, wr_ref, ws_ref, wl_ref, pr_ref, ps_ref, pq_ref)


def _combine_inproj_kernel(h_ref, y0_ref, y1_ref, gt_ref, g_ref, wr_ref, ws_ref, wl_ref,
                           ho_ref, pr_ref, ps_ref, pq_ref):
    h = _moe_combine(h_ref, y0_ref, y1_ref, gt_ref)
    ho_ref[...] = h
    _project(h, g_ref, wr_ref, ws_ref, wl_ref, pr_ref, ps_ref, pq_ref)


def _inproj(x2, g, w_ret, w_ssd, w_lru, moe=None):
    t, d = x2.shape
    tm = ROW_TILE
    nt = t // tm
    n_r, n_s, n_l = w_ret.shape[1], w_ssd.shape[1], w_lru.shape[1]
    full = lambda n: pl.BlockSpec((d, n), lambda i: (0, 0))
    row = lambda n: pl.BlockSpec((tm, n), lambda i: (i, 0))
    weights = [pl.BlockSpec((1, d), lambda i: (0, 0)), full(n_r), full(n_s), full(n_l)]
    outs = [row(n_r), row(n_s), row(n_l)]
    shapes = [jax.ShapeDtypeStruct((t, n), F32) for n in (n_r, n_s, n_l)]
    if moe is None:
        return pl.pallas_call(
            _inproj_kernel, grid=(nt,), in_specs=[row(d)] + weights, out_specs=outs, out_shape=shapes,
            compiler_params=_cparams(("parallel",)), name="inproj",
        )(x2, g.reshape(1, d), w_ret, w_ssd, w_lru)
    y, gates = moe
    tiles = lambda off: pl.BlockSpec((tm * SUBLANE, LANE), lambda i: (off + i, 0))
    return pl.pallas_call(
        _combine_inproj_kernel, grid=(nt,),
        in_specs=[row(d), tiles(0), tiles(nt), row(TOP_K)] + weights,
        out_specs=[row(d)] + outs, out_shape=[jax.ShapeDtypeStruct((t, d), F32)] + shapes,
        compiler_params=_cparams(("parallel",)), name="combine_inproj",
    )(x2, y, y, gates[:TOP_K].T, g.reshape(1, d), w_ret, w_ssd, w_lru)


def _ret_kernel(q_ref, k_ref, v_ref, g_ref, cos_ref, sin_ref, d2_ref, dec_ref, gc_ref,
                o_ref, qs_ref, ks_ref, acc_ref):
    s = q_ref.shape[0]
    n = s // CHUNK
    lane = lax.broadcasted_iota(jnp.int32, (1, LANE), 1)
    mq0 = ((lane // 32) % 2 == 0).astype(F32)
    mq1 = 1.0 - mq0
    mv0 = (lane < HEAD_DIM).astype(F32)
    mv1 = 1.0 - mv0
    krow = lax.broadcasted_iota(jnp.int32, (LANE, LANE), 0)
    vcol = lax.broadcasted_iota(jnp.int32, (LANE, LANE), 1)
    bd = (((krow // 32) % 2) == (vcol // HEAD_DIM)).astype(F32)
    zeta_f = dec_ref[0]
    zeta_b = dec_ref[1]
    xi_f = dec_ref[2]
    xi_b = dec_ref[3]
    gc = gc_ref[...]
    d2 = d2_ref[...]

    def rot(x, c):
        rows = pl.ds(c * CHUNK, CHUNK)
        return x * cos_ref[rows, :] + pltpu.roll(x, 64, 1) * sin_ref[rows, :]

    def fwd(c, st):
        rows = pl.ds(pl.multiple_of(c * CHUNK, CHUNK), CHUNK)
        qc = rot(q_ref[rows, :], c)
        kc = rot(k_ref[rows, :], c) * (HEAD_DIM ** -0.5)
        vc = v_ref[rows, :]
        qs_ref[rows, :] = qc
        ks_ref[rows, :] = kc
        q2 = jnp.concatenate([qc * mq0, qc * mq1], axis=0)
        sc = lax.dot_general(q2, kc, (((1,), (1,)), ((), ())), preferred_element_type=F32) * d2
        o2 = jnp.dot(sc, vc, preferred_element_type=F32)
        o = o2[:CHUNK] * mv0 + o2[CHUNK:] * mv1
        o = o + jnp.dot(qc, st, preferred_element_type=F32) * xi_f
        acc_ref[rows, :] = o
        upd = jnp.dot(kc.T, vc * zeta_f, preferred_element_type=F32) * bd
        return st * gc + upd

    lax.fori_loop(0, n, fwd, jnp.zeros((LANE, LANE), F32), unroll=16)

    def bwd(i, st):
        c = n - 1 - i
        rows = pl.ds(pl.multiple_of(c * CHUNK, CHUNK), CHUNK)
        qc = qs_ref[rows, :]
        kc = ks_ref[rows, :]
        vc = v_ref[rows, :]
        o = acc_ref[rows, :] + jnp.dot(qc, st, preferred_element_type=F32) * xi_b
        oo = o * o
        ss0 = jnp.sum(oo * mv0, axis=1, keepdims=True)
        ss1 = jnp.sum(oo * mv1, axis=1, keepdims=True)
        scale = lax.rsqrt(ss0 * (1.0 / HEAD_DIM) + EPS) * mv0 + lax.rsqrt(ss1 * (1.0 / HEAD_DIM) + EPS) * mv1
        gate = g_ref[rows, :]
        gate = gate * jax.nn.sigmoid(gate)
        o_ref[rows, :] = (gate * (o * scale)).astype(o_ref.dtype)
        upd = jnp.dot(kc.T, vc * zeta_b, preferred_element_type=F32) * bd
        return st * gc + upd

    lax.fori_loop(0, n, bwd, jnp.zeros((LANE, LANE), F32), unroll=16)


def _ret_constants(s):
    half = HEAD_DIM // 2
    inv = ROPE_BASE ** (-jnp.arange(half, dtype=F32) / half)
    ang = jnp.arange(s, dtype=F32)[:, None] * inv[None, :]
    cos = jnp.cos(ang)
    sin = jnp.sin(ang)
    cos2 = jnp.concatenate([cos, cos, cos, cos], axis=1)
    sin2 = jnp.concatenate([-sin, -sin, sin, sin], axis=1)
    log_gamma = jnp.log1p(-jnp.exp2(-5.0 - jnp.arange(RET_HEADS, dtype=F32)))
    lg_pair = log_gamma.reshape(RET_HEADS // 2, 2)
    idx = jnp.arange(CHUNK, dtype=F32)
    adiff = jnp.abs(idx[:, None] - idx[None, :])
    d2 = jnp.exp(adiff[None, None] * lg_pair[:, :, None, None])
    d2 = d2.reshape(RET_HEADS // 2, 2 * CHUNK, CHUNK)
    lg_lane = jnp.repeat(lg_pair, HEAD_DIM, axis=1)
    pos = idx[None, :, None]
    lgl = lg_lane[:, None, :]
    dec = jnp.stack([jnp.exp((CHUNK - 1 - pos) * lgl), jnp.exp(pos * lgl),
                     jnp.exp((pos + 1) * lgl), jnp.exp((CHUNK - pos) * lgl)], axis=1)
    gc = jnp.exp(CHUNK * lg_lane)[:, None, :]
    return cos2, sin2, d2, dec, gc


def _retention(p_ret, b, s):
    p3 = p_ret.reshape(b, s, 4 * RET_WIDTH)
    cos2, sin2, d2, dec, gc = _ret_constants(s)
    npair = RET_HEADS // 2
    col = lambda off: pl.BlockSpec((None, s, LANE), lambda i, p: (i, 0, off + p))
    return pl.pallas_call(
        _ret_kernel,
        grid=(b, npair),
        in_specs=[col(0), col(npair), col(2 * npair), col(3 * npair),
                  pl.BlockSpec((s, LANE), lambda i, p: (0, 0)),
                  pl.BlockSpec((s, LANE), lambda i, p: (0, 0)),
                  pl.BlockSpec((None, 2 * CHUNK, CHUNK), lambda i, p: (p, 0, 0)),
                  pl.BlockSpec((None, 4, CHUNK, LANE), lambda i, p: (p, 0, 0, 0)),
                  pl.BlockSpec((None, 1, LANE), lambda i, p: (p, 0, 0))],
        out_specs=pl.BlockSpec((None, s, LANE), lambda i, p: (i, 0, p)),
        out_shape=jax.ShapeDtypeStruct((b, s, RET_WIDTH), BF16),
        scratch_shapes=[pltpu.VMEM((s, LANE), F32), pltpu.VMEM((s, LANE), F32), pltpu.VMEM((s, LANE), F32)],
        compiler_params=_cparams(("parallel", "parallel")),
        name="retention",
    )(p3, p3, p3, p3, cos2, sin2, d2, dec, gc)


_PAD = 8


def _softplus(x):
    return jnp.maximum(x, 0.0) + jnp.log1p(jnp.exp(-jnp.abs(x)))


def _silu(x):
    return x * jax.nn.sigmoid(x)


def _expand_heads(cols, base, lane_head):
    acc = jnp.zeros((cols.shape[0], SSD_GW), F32)
    for r in range(SSD_HPG):
        acc = jnp.where(lane_head == r, cols[:, base + r:base + r + 1], acc)
    return acc


def _dwconv_chunk(xp_ref, cw_ref, cb_ref, c):
    rows = CHUNK + 2 * _PAD
    win = xp_ref[pl.ds(pl.multiple_of(c * CHUNK, CHUNK), rows), :]
    acc = cb_ref[...]
    for t in range(CONV_WIDTH):
        shift = (CONV_WIDTH // 2 - t) % rows
        tap = win if shift == 0 else pltpu.roll(win, shift, 0)
        acc = acc + tap[_PAD:_PAD + CHUNK, :] * cw_ref[t:t + 1, :]
    return acc


def _ssd_kernel(xbc_ref, z_ref, dt_ref, cw_ref, cb_ref, dtb_ref, alog_ref, dsk_ref, nw_ref, o_ref,
                xp_ref, xc_ref, y_ref, pc_ref, dts_ref):
    s = z_ref.shape[0]
    n = s // CHUNK
    w = xbc_ref.shape[1]
    xp_ref[0:_PAD, :] = jnp.zeros((_PAD, w), F32)
    xp_ref[_PAD + s:2 * _PAD + s, :] = jnp.zeros((_PAD, w), F32)

    def copy(c, carry):
        rows = pl.ds(pl.multiple_of(c * CHUNK, CHUNK), CHUNK)
        xp_ref[pl.ds(pl.multiple_of(c * CHUNK + _PAD, _PAD), CHUNK), :] = xbc_ref[rows, :]
        return carry

    lax.fori_loop(0, n, copy, 0)

    lane_head = lax.broadcasted_iota(jnp.int32, (1, SSD_GW), 1) // HEAD_DIM
    ii = lax.broadcasted_iota(jnp.int32, (CHUNK, CHUNK), 0)
    jj = lax.broadcasted_iota(jnp.int32, (CHUNK, CHUNK), 1)
    lower = ii >= jj
    tri = lower.astype(F32)
    a_neg = -jnp.exp(alog_ref[...])
    dt_bias = dtb_ref[...]

    def fwd(c, hf):
        rows = pl.ds(pl.multiple_of(c * CHUNK, CHUNK), CHUNK)
        xc = _silu(_dwconv_chunk(xp_ref, cw_ref, cb_ref, c))
        xc_ref[rows, :] = xc
        xs = xc[:, 0:SSD_GW]
        bm = xc[:, SSD_GW:SSD_GW + SSD_STATE]
        cm = xc[:, SSD_GW + SSD_STATE:]
        dt = _softplus(dt_ref[rows, :] + dt_bias)
        dts_ref[rows, :] = dt
        la = dt * a_neg
        p = jnp.dot(tri, la, precision=lax.Precision.HIGHEST, preferred_element_type=F32)
        pc_ref[rows, :] = p
        e = p - la
        pt = p.T
        et = e.T
        dtt = dt.T
        g = lax.dot_general(cm, bm, (((1,), (1,)), ((), ())), preferred_element_type=F32)
        y = jnp.zeros((CHUNK, SSD_GW), F32)
        for r in range(SSD_HPG):
            arg = jnp.where(lower, p[:, r:r + 1] - pt[r:r + 1, :], et[3 + r:4 + r, :] - e[:, 3 + r:4 + r])
            dtj = jnp.where(lower, dtt[r:r + 1, :], dtt[3 + r:4 + r, :])
            m = g * (jnp.exp(arg) * dtj)
            y = y + jnp.dot(m, jnp.where(lane_head == r, xs, 0.0), preferred_element_type=F32)
        pf = _expand_heads(p, 0, lane_head)
        pf_last = _expand_heads(p[CHUNK - 1:CHUNK, :], 0, lane_head)
        dtf = _expand_heads(dt, 0, lane_head)
        y = y + jnp.dot(cm, hf, preferred_element_type=F32) * jnp.exp(pf)
        y_ref[rows, :] = y
        upd = jnp.dot(bm.T, xs * (jnp.exp(pf_last - pf) * dtf), preferred_element_type=F32)
        return hf * jnp.exp(pf_last) + upd

    lax.fori_loop(0, n, fwd, jnp.zeros((SSD_STATE, SSD_GW), F32), unroll=2)

    dsk = dsk_ref[...]
    nw = nw_ref[...]

    def bwd(i, hb):
        c = n - 1 - i
        rows = pl.ds(pl.multiple_of(c * CHUNK, CHUNK), CHUNK)
        xs = xc_ref[rows, 0:SSD_GW]
        bm = xc_ref[rows, SSD_GW:SSD_GW + SSD_STATE]
        cm = xc_ref[rows, SSD_GW + SSD_STATE:]
        dt = dts_ref[rows, :]
        p = pc_ref[rows, :]
        e = p - dt * a_neg
        eb = _expand_heads(e, 3, lane_head)
        tb = _expand_heads(p[CHUNK - 1:CHUNK, :], 3, lane_head)
        dtb = _expand_heads(dt, 3, lane_head)
        y = y_ref[rows, :] + jnp.dot(cm, hb, preferred_element_type=F32) * jnp.exp(tb - eb)
        y = (y + dsk * xs) * _silu(z_ref[rows, :])
        ms = jnp.sum(y * y, axis=1, keepdims=True) * (1.0 / (SSD_HPG * HEAD_DIM))
        o_ref[rows, :] = (y * lax.rsqrt(ms + EPS) * nw).astype(o_ref.dtype)
        upd = jnp.dot(bm.T, xs * (jnp.exp(eb) * dtb), preferred_element_type=F32)
        return hb * jnp.exp(tb) + upd

    lax.fori_loop(0, n, bwd, jnp.zeros((SSD_STATE, SSD_GW), F32), unroll=2)


_SSD_XBC_W = SSD_GW + 2 * SSD_STATE
_SSD_SLAB = SSD_GROUPS * (_SSD_XBC_W + SSD_GW + LANE)


def _ssd_xbc_cols(a, off):
    gw = SSD_HPG * HEAD_DIM
    zero = jnp.zeros(a.shape[:-1] + (SSD_GW - gw,), a.dtype)
    out = []
    for g in range(SSD_GROUPS):
        b0 = off + SSD_WIDTH + g * SSD_STATE
        c0 = b0 + SSD_GROUPS * SSD_STATE
        out += [a[..., off + g * gw:off + (g + 1) * gw], zero, a[..., b0:b0 + SSD_STATE], a[..., c0:c0 + SSD_STATE]]
    return jnp.concatenate(out, axis=-1)


def _ssd_slab_cols(w):
    gw = SSD_HPG * HEAD_DIM
    zero = w.shape[1]
    pad = lambda k: np.full(k, zero, np.int64)
    xbc, zz, dtc = [], [], []
    for g in range(SSD_GROUPS):
        b0 = _OFF_XBC + SSD_WIDTH + g * SSD_STATE
        c0 = b0 + SSD_GROUPS * SSD_STATE
        xbc += [_OFF_XBC + g * gw + np.arange(gw), pad(SSD_GW - gw), b0 + np.arange(SSD_STATE), c0 + np.arange(SSD_STATE)]
        zz += [_OFF_Z + g * gw + np.arange(gw), pad(SSD_GW - gw)]
        heads = _OFF_DT + g * SSD_HPG + np.arange(SSD_HPG)
        dtc += [heads, heads, pad(LANE - 2 * SSD_HPG)]
    w0 = jnp.concatenate([w, jnp.zeros((w.shape[0], 1), w.dtype)], axis=1)
    return w0[:, np.concatenate(xbc + zz + dtc)]


def _ssd(p_ssd, b, s, conv_w, conv_b, dt_bias, a_log, d_skip, norm_w):
    p3 = p_ssd.reshape(b, s, _SSD_SLAB)
    cw = _ssd_xbc_cols(conv_w, 0)
    cb = _ssd_xbc_cols(conv_b[None, :], 0)
    gw = SSD_HPG * HEAD_DIM
    zpad = jnp.zeros((SSD_GROUPS, LANE - 2 * SSD_HPG), F32)
    grp = lambda v: jnp.concatenate([v[0].reshape(SSD_GROUPS, SSD_HPG), v[1].reshape(SSD_GROUPS, SSD_HPG), zpad], axis=1)
    dtb = grp(dt_bias)[:, None, :]
    alog = grp(a_log)[:, None, :]
    lanes = lambda v: jnp.pad(v.reshape(SSD_GROUPS, gw), ((0, 0), (0, SSD_GW - gw)))[:, None, :]
    dsk = lanes(jnp.repeat(d_skip, HEAD_DIM))
    nw = lanes(norm_w)
    nxb = _SSD_XBC_W // LANE
    grid_spec = dict(
        grid=(b, SSD_GROUPS),
        in_specs=[pl.BlockSpec((None, s, _SSD_XBC_W), lambda i, g: (i, 0, g)),
                  pl.BlockSpec((None, s, SSD_GW), lambda i, g: (i, 0, SSD_GROUPS * _SSD_XBC_W // SSD_GW + g)),
                  pl.BlockSpec((None, s, LANE), lambda i, g: (i, 0, SSD_GROUPS * (nxb + SSD_GW // LANE) + g)),
                  pl.BlockSpec((CONV_WIDTH, _SSD_XBC_W), lambda i, g: (0, g)),
                  pl.BlockSpec((1, _SSD_XBC_W), lambda i, g: (0, g)),
                  pl.BlockSpec((None, 1, LANE), lambda i, g: (g, 0, 0)),
                  pl.BlockSpec((None, 1, LANE), lambda i, g: (g, 0, 0)),
                  pl.BlockSpec((None, 1, SSD_GW), lambda i, g: (g, 0, 0)),
                  pl.BlockSpec((None, 1, SSD_GW), lambda i, g: (g, 0, 0))],
        out_specs=pl.BlockSpec((None, s, SSD_GW), lambda i, g: (i, 0, g)),
    )
    return pl.pallas_call(
        _ssd_kernel,
        out_shape=jax.ShapeDtypeStruct((b, s, SSD_GROUPS * SSD_GW), BF16),
        scratch_shapes=[pltpu.VMEM((s + 2 * _PAD, _SSD_XBC_W), F32), pltpu.VMEM((s, _SSD_XBC_W), F32),
                        pltpu.VMEM((s, SSD_GW), F32), pltpu.VMEM((s, LANE), F32), pltpu.VMEM((s, LANE), F32)],
        compiler_params=_cparams(("parallel", "parallel")),
        name="ssd",
        **grid_spec,
    )(p3, p3, p3, cw, cb, dtb, alog, dsk, nw)


def _scan_chunk(a, u, h, reverse):
    n_groups = CHUNK // SUBLANE
    width = a.shape[1]
    a = a.reshape(n_groups, SUBLANE, width)
    u = u.reshape(n_groups, SUBLANE, width)
    sub = lax.broadcasted_iota(jnp.int32, (1, SUBLANE, 1), 1)
    d = 1
    while d < SUBLANE:
        keep = (sub < SUBLANE - d) if reverse else (sub >= d)
        shift = SUBLANE - d if reverse else d
        a_sh = jnp.where(keep, pltpu.roll(a, shift, 1), 1.0)
        u_sh = jnp.where(keep, pltpu.roll(u, shift, 1), 0.0)
        u = a * u_sh + u
        a = a * a_sh
        d *= 2
    a = a.reshape(CHUNK, width)
    u = u.reshape(CHUNK, width)
    out = [None] * n_groups
    for g in (range(n_groups - 1, -1, -1) if reverse else range(n_groups)):
        rows = slice(g * SUBLANE, (g + 1) * SUBLANE)
        hg = u[rows] + a[rows] * h
        out[g] = hg
        h = hg[0:1] if reverse else hg[SUBLANE - 1:SUBLANE]
    return jnp.concatenate(out, axis=0), h


def _lru_kernel(x_ref, cw_ref, cb_ref, wg_ref, bg_ref, lam_ref, o_ref, xp_ref, hf_ref, ab_ref, ub_ref):
    s = x_ref.shape[0]
    n = s // CHUNK
    w = LRU_WIDTH
    xp_ref[0:_PAD, :] = jnp.zeros((_PAD, w), F32)
    xp_ref[_PAD + s:2 * _PAD + s, :] = jnp.zeros((_PAD, w), F32)

    def copy(c, carry):
        rows = pl.ds(pl.multiple_of(c * CHUNK, CHUNK), CHUNK)
        xp_ref[pl.ds(pl.multiple_of(c * CHUNK + _PAD, _PAD), CHUNK), :] = x_ref[rows, 0:w]
        return carry

    lax.fori_loop(0, n, copy, 0)
    nsp = -LRU_C * _softplus(-lam_ref[...])

    def gates(pre, xc, k):
        r = jax.nn.sigmoid(pre[:, 2 * k * w:(2 * k + 1) * w])
        i = jax.nn.sigmoid(pre[:, (2 * k + 1) * w:(2 * k + 2) * w])
        log_a = r * nsp[k:k + 1, :]
        a = jnp.exp(log_a)
        return a, jnp.sqrt(-jnp.tanh(log_a) * (a * a + 1.0)) * (i * xc)

    def fwd(c, h):
        rows = pl.ds(pl.multiple_of(c * CHUNK, CHUNK), CHUNK)
        xc = _dwconv_chunk(xp_ref, cw_ref, cb_ref, c)
        pre = jnp.dot(xc.astype(BF16), wg_ref[...], preferred_element_type=F32) + bg_ref[...]
        a_b, u_b = gates(pre, xc, 1)
        ab_ref[rows, :] = a_b
        ub_ref[rows, :] = u_b
        a_f, u_f = gates(pre, xc, 0)
        hc, h = _scan_chunk(a_f, u_f, h, False)
        hf_ref[rows, :] = hc
        return h

    lax.fori_loop(0, n, fwd, jnp.zeros((1, w), F32))

    def bwd(i, h):
        c = n - 1 - i
        rows = pl.ds(pl.multiple_of(c * CHUNK, CHUNK), CHUNK)
        hc, h = _scan_chunk(ab_ref[rows, :], ub_ref[rows, :], h, True)
        gate = x_ref[rows, w:2 * w]
        o_ref[rows, :] = ((hf_ref[rows, :] + hc) * jax.nn.gelu(gate)).astype(o_ref.dtype)
        return h

    lax.fori_loop(0, n, bwd, jnp.zeros((1, w), F32))


def _block_diag(wblk):
    eye = jnp.eye(LRU_BLOCKS, dtype=wblk.dtype)
    return jnp.einsum('hij,hk->hikj', wblk, eye).reshape(LRU_WIDTH, LRU_WIDTH)


def _lru(p_lru, b, s, conv_w, conv_b, lam, wa, ba, wx, bx):
    p3 = p_lru.reshape(b, s, 2 * LRU_WIDTH)
    wg = jnp.concatenate([_block_diag(wa[0]), _block_diag(wx[0]), _block_diag(wa[1]), _block_diag(wx[1])], axis=1).astype(BF16)
    bg = jnp.concatenate([ba[0], bx[0], ba[1], bx[1]])[None, :]
    w = LRU_WIDTH
    const = lambda shape: pl.BlockSpec(shape, lambda i: (0,) * len(shape))
    return pl.pallas_call(
        _lru_kernel,
        grid=(b,),
        in_specs=[pl.BlockSpec((None, s, 2 * w), lambda i: (i, 0, 0)),
                  const((CONV_WIDTH, w)), const((1, w)), const((w, 4 * w)), const((1, 4 * w)), const((2, w))],
        out_specs=pl.BlockSpec((None, s, w), lambda i: (i, 0, 0)),
        out_shape=jax.ShapeDtypeStruct((b, s, w), BF16),
        scratch_shapes=[pltpu.VMEM((s + 2 * _PAD, w), F32), pltpu.VMEM((s, w), F32),
                        pltpu.VMEM((s, w), F32), pltpu.VMEM((s, w), F32)],
        compiler_params=_cparams(("parallel",)),
        name="rglru",
    )(p3, conv_w, conv_b[None, :], wg, bg, lam)


def _outproj_weights(w_out):
    gw = SSD_HPG * HEAD_DIM
    zero = jnp.zeros((SSD_GW - gw, w_out.shape[1]), w_out.dtype)
    rows = [p for g in range(SSD_GROUPS) for p in (w_out[RET_WIDTH + g * gw:RET_WIDTH + (g + 1) * gw], zero)]
    w_ssd = jnp.concatenate(rows, axis=0)
    return (w_out[:RET_WIDTH].astype(BF16), w_ssd.astype(BF16), w_out[RET_WIDTH + SSD_WIDTH:].astype(BF16))


def _kv_kernel(m_ref, g_ref, w_ref, o_ref):
    mn = _rms(m_ref[...], g_ref[...]).astype(BF16)
    o_ref[...] = jnp.dot(mn, w_ref[...], preferred_element_type=F32).astype(o_ref.dtype)


def _kv_proj(mem2, g, wkv):
    t, d = mem2.shape
    n = wkv.shape[1]
    tm = 256
    return pl.pallas_call(
        _kv_kernel,
        grid=(t // tm,),
        in_specs=[pl.BlockSpec((tm, d), lambda i: (i, 0)), pl.BlockSpec((1, d), lambda i: (0, 0)),
                  pl.BlockSpec((d, n), lambda i: (0, 0))],
        out_specs=pl.BlockSpec((tm, n), lambda i: (i, 0)),
        out_shape=jax.ShapeDtypeStruct((t, n), BF16),
        compiler_params=_cparams(("parallel",)),
        name="kvproj",
    )(mem2, g.reshape(1, d), wkv)


def _xattn_tile(h, g, wq_ref, k_ref, v_ref, wo_ref):
    xn = _rms(h, g).astype(BF16)
    q = jnp.dot(xn, wq_ref[...], preferred_element_type=F32).astype(BF16)
    outs = []
    for hd in range(CA_HEADS):
        cols = slice(hd * CA_HEAD_DIM, (hd + 1) * CA_HEAD_DIM)
        sc = lax.dot_general(q[:, cols], k_ref[:, cols], (((1,), (1,)), ((), ())), preferred_element_type=F32)
        sc = sc * (CA_HEAD_DIM ** -0.5)
        e = jnp.exp(sc - jnp.max(sc, axis=-1, keepdims=True))
        p = e / jnp.sum(e, axis=-1, keepdims=True)
        outs.append(jnp.dot(p.astype(BF16), v_ref[:, cols], preferred_element_type=F32).astype(BF16))
    o = jnp.concatenate(outs, axis=-1)
    return h + jnp.dot(o, wo_ref[...], preferred_element_type=F32)


_ROUTER_ROWS = SUBLANE + N_EXPERTS


def _first_argmax(v, row):
    m = jnp.max(v, axis=0, keepdims=True)
    return m, jnp.min(jnp.where(v == m, row, SUBLANE), axis=0, keepdims=True)


def _router_tile(h, g, w_ref, b_ref):
    xn = _rms(h, g).astype(BF16)
    lg = lax.dot_general(w_ref[...], xn, (((1,), (1,)), ((), ())), preferred_element_type=F32) + b_ref[...]
    tm = lg.shape[1]
    row = lax.broadcasted_iota(jnp.int32, (SUBLANE, tm), 0)
    gl = jnp.where(row < N_GROUPS, lg[0:SUBLANE], -jnp.inf)
    ge = jnp.exp(gl - jnp.max(gl, axis=0, keepdims=True))
    gp_all = ge / jnp.sum(ge, axis=0, keepdims=True)
    gp, gi = _first_argmax(gp_all, row)
    el = lg[SUBLANE:2 * SUBLANE]
    for g in range(1, N_GROUPS):
        el = jnp.where(gi == g, lg[SUBLANE * (g + 1):SUBLANE * (g + 2)], el)
    ee = jnp.exp(el - jnp.max(el, axis=0, keepdims=True))
    ep = ee / jnp.sum(ee, axis=0, keepdims=True)
    v1, i1 = _first_argmax(ep, row)
    v2, i2 = _first_argmax(jnp.where(row == i1, -1.0, ep), row)
    den = v1 + v2
    ids = jnp.where(row == 0, gi * EXPERTS_PER_GROUP + i1, jnp.where(row == 1, gi * EXPERTS_PER_GROUP + i2, 0))
    gates = jnp.where(row == 0, gp * v1 / den, jnp.where(row == 1, gp * v2 / den, 0.0))
    return ids, gates


assert D_MODEL == SUBLANE * LANE


def _token_tile(r):
    return pl.ds(pl.multiple_of(r * SUBLANE, SUBLANE), SUBLANE)


def _rows_to_tiles(ref, x):
    for j in range(SUBLANE):
        ref[pl.ds(j, x.shape[0], stride=SUBLANE), :] = x[:, j * LANE:(j + 1) * LANE]


def _tiles_to_rows(ref, n):
    return jnp.concatenate([ref[pl.ds(j, n, stride=SUBLANE), :] for j in range(SUBLANE)], axis=1)


def _post_kernel(h_ref, a_ref, b_ref, c_ref, wa_ref, wb_ref, wc_ref, gca_ref, wq_ref, k_ref, v_ref, wo_ref,
                 gmoe_ref, wr_ref, br_ref, o_ref, ot_ref, ids_ref, gates_ref):
    h = h_ref[...]
    h = h + jnp.dot(a_ref[...], wa_ref[...], preferred_element_type=F32)
    h = h + jnp.dot(b_ref[...], wb_ref[...], preferred_element_type=F32)
    h = h + jnp.dot(c_ref[...], wc_ref[...], preferred_element_type=F32)
    h = _xattn_tile(h, gca_ref[...], wq_ref, k_ref, v_ref, wo_ref)
    o_ref[...] = h
    _rows_to_tiles(ot_ref, h)
    ids, gates = _router_tile(h, gmoe_ref[...], wr_ref, br_ref)
    ids_ref[...] = ids
    gates_ref[...] = gates


def _post_mixer(h2, b, s, o_ret, o_ssd, o_lru, w_ret, w_ssd, w_lru, g_ca, wq, kv, wo, g_moe, wg, bg, we, be):
    t, d = h2.shape
    m = kv.shape[0] // b
    ts = ROW_TILE
    nts = s // ts
    zrow = jnp.zeros((SUBLANE - N_GROUPS, d), F32)
    wr = jnp.concatenate([wg.T, zrow, we.T], axis=0).astype(BF16)
    br = jnp.concatenate([bg, jnp.zeros((SUBLANE - N_GROUPS,), F32), be])[:, None]
    row = lambda n: pl.BlockSpec((ts, n), lambda i, j: (i * nts + j, 0))
    const = lambda r, c: pl.BlockSpec((r, c), lambda i, j: (0, 0))
    ka, kb, kc = o_ret.shape[1], o_ssd.shape[1], o_lru.shape[1]
    lanes = pl.BlockSpec((SUBLANE, ts), lambda i, j: (0, i * nts + j))
    return pl.pallas_call(
        _post_kernel,
        grid=(b, nts),
        in_specs=[row(d), row(ka), row(kb), row(kc), const(ka, d), const(kb, d), const(kc, d),
                  const(1, d), const(d, d),
                  pl.BlockSpec((m, d), lambda i, j: (i, 0)),
                  pl.BlockSpec((m, d), lambda i, j: (i, 1)),
                  const(d, d), const(1, d), const(_ROUTER_ROWS, d), const(_ROUTER_ROWS, 1)],
        out_specs=[row(d), pl.BlockSpec((ts * SUBLANE, LANE), lambda i, j: (i * nts + j, 0)), lanes, lanes],
        out_shape=[jax.ShapeDtypeStruct((t, d), F32), jax.ShapeDtypeStruct((t * SUBLANE, LANE), F32),
                   jax.ShapeDtypeStruct((SUBLANE, t), jnp.int32), jax.ShapeDtypeStruct((SUBLANE, t), F32)],
        compiler_params=_cparams(("parallel", "parallel")),
        name="post_mixer",
    )(h2, o_ret, o_ssd, o_lru, w_ret, w_ssd, w_lru, g_ca.reshape(1, d), wq, kv, kv, wo,
      g_moe.reshape(1, d), wr, br)


MOE_BLK = 256


_ASSIGN_BITS = 17


def _dispatch_plan(ids, t):
    n_assign = t * TOP_K
    assert n_assign <= 1 << _ASSIGN_BITS
    e = ids[:TOP_K].reshape(n_assign)
    key =jnp.left_shift(e, _ASSIGN_BITS) | jnp.arange(n_assign, dtype=jnp.int32)
    order = jnp.sort(key) & ((1 << _ASSIGN_BITS) - 1)
    order = jnp.concatenate([order, jnp.zeros((MOE_BLK,), jnp.int32)])
    experts = jnp.arange(N_EXPERTS, dtype=jnp.int32)
    counts = jnp.sum((e[None, :] == experts[:, None]).astype(jnp.int32), axis=1)
    start = jnp.cumsum(counts) - counts
    nb = (counts + MOE_BLK - 1) // MOE_BLK
    blk_end = jnp.cumsum(nb)
    n_blocks = n_assign // MOE_BLK + N_EXPERTS
    j = jnp.arange(n_blocks, dtype=jnp.int32)
    be = jnp.minimum(jnp.sum((blk_end[None, :] <= j[:, None]).astype(jnp.int32), axis=1), N_EXPERTS - 1)
    onehot = (be[:, None] == experts[None, :]).astype(jnp.int32)
    pick = lambda v: jnp.sum(onehot * v[None, :], axis=1)
    r = j - pick(blk_end - nb)
    row_start = jnp.clip(pick(start) + r * MOE_BLK, 0, n_assign)
    n_rows = jnp.clip(pick(counts) - r * MOE_BLK, 0, MOE_BLK)
    tok = jnp.where(order >= t, order - t, order)
    rr = jnp.arange(MOE_BLK, dtype=jnp.int32)[None, :]
    dump = n_assign + (j % 2)[:, None] * MOE_BLK + rr
    dest = jnp.where(rr < n_rows[:, None], order[row_start[:, None] + rr], dump).reshape(n_blocks * MOE_BLK)
    return tok, dest, be, row_start, blk_end[-1:].astype(jnp.int32)


def _expert_kernel(be_ref, rs_ref, nu_ref, tok_ref, dst_ref, h_hbm, g_ref, w1_ref, w3_ref, w2_ref, y_hbm,
                   xbuf, ybuf, wb1, wb3, wb2, gsem, ssem, *, n_tokens):
    n_assign = TOP_K * n_tokens
    blk_rows = MOE_BLK * SUBLANE
    i = pl.program_id(0)
    n_used = nu_ref[0]
    slot = i % 2

    def gather(blk, s):
        base = rs_ref[blk]

        def body(r, carry):
            tok = tok_ref[base + r]
            pltpu.make_async_copy(h_hbm.at[_token_tile(tok), :], xbuf.at[s, _token_tile(r), :], gsem.at[s]).start()
            return carry
        lax.fori_loop(0, MOE_BLK, body, 0, unroll=8)

    def scatter(blk, s):
        base = blk * MOE_BLK

        def body(r, carry):
            dest = dst_ref[base + r]
            pltpu.make_async_copy(ybuf.at[s, _token_tile(r), :], y_hbm.at[_token_tile(dest), :], ssem.at[s]).start()
            return carry
        lax.fori_loop(0, MOE_BLK, body, 0, unroll=8)

    def wait_gather(s):
        pltpu.make_async_copy(h_hbm.at[pl.ds(0, blk_rows), :], xbuf.at[s], gsem.at[s]).wait()

    def wait_scatter(s):
        pltpu.make_async_copy(ybuf.at[s], y_hbm.at[pl.ds(0, blk_rows), :], ssem.at[s]).wait()

    @pl.when(i == 0)
    def _():
        ybuf[0] = jnp.zeros((blk_rows, LANE), F32)
        for s in range(2):
            dump = pl.ds((n_assign + s * MOE_BLK) * SUBLANE, blk_rows)
            fill = pltpu.make_async_copy(ybuf.at[0], y_hbm.at[dump, :], ssem.at[0])
            fill.start()
            fill.wait()
        gather(0, 0)

    @pl.when(i + 1 < n_used)
    def _():
        gather(i + 1, 1 - slot)

    @pl.when(i < n_used)
    def _():
        @pl.when((i == 0) | (be_ref[i] != be_ref[jnp.maximum(i - 1, 0)]))
        def _():
            wb1[...] = w1_ref[...].astype(BF16)
            wb3[...] = w3_ref[...].astype(BF16)
            wb2[...] = w2_ref[...].astype(BF16)

        wait_gather(slot)

        @pl.when(i >= 2)
        def _():
            wait_scatter(slot)

        xn = _rms(_tiles_to_rows(xbuf.at[slot], MOE_BLK), g_ref[...]).astype(BF16)
        h1 = jnp.dot(xn, wb1[...], preferred_element_type=F32)
        h3 = jnp.dot(xn, wb3[...], preferred_element_type=F32)
        hid = (_silu(h1) * h3).astype(BF16)
        _rows_to_tiles(ybuf.at[slot], jnp.dot(hid, wb2[...], preferred_element_type=F32))
        scatter(i, slot)

        @pl.when(i == n_used - 1)
        def _():
            wait_scatter(slot)

            @pl.when(i >= 1)
            def _():
                wait_scatter(1 - slot)


def _experts(h_tiles, g, tok, dest, block_e, row_start, n_used, w1, w3, w2, layer):
    t, d = h_tiles.shape[0] // SUBLANE, D_MODEL
    n_assign = t * TOP_K
    n_blocks = block_e.shape[0]
    de = w1.shape[3]
    wmap = lambda i, be, *_: (layer, be[i], 0, 0)
    grid_spec = pltpu.PrefetchScalarGridSpec(
        num_scalar_prefetch=5,
        grid=(n_blocks,),
        in_specs=[pl.BlockSpec(memory_space=pl.ANY),
                  pl.BlockSpec((1, d), lambda i, *_: (0, 0)),
                  pl.BlockSpec((None, None, d, de), wmap),
                  pl.BlockSpec((None, None, d, de), wmap),
                  pl.BlockSpec((None, None, de, d), wmap)],
        out_specs=pl.BlockSpec(memory_space=pl.ANY),
        scratch_shapes=[pltpu.VMEM((2, MOE_BLK * SUBLANE, LANE), F32), pltpu.VMEM((2, MOE_BLK * SUBLANE, LANE), F32),
                        pltpu.VMEM((d, de), BF16), pltpu.VMEM((d, de), BF16), pltpu.VMEM((de, d), BF16),
                        pltpu.SemaphoreType.DMA((2,)), pltpu.SemaphoreType.DMA((2,))],
    )
    return pl.pallas_call(
        functools.partial(_expert_kernel, n_tokens=t),
        grid_spec=grid_spec,
        out_shape=jax.ShapeDtypeStruct(((n_assign + 2 * MOE_BLK) * SUBLANE, LANE), F32),
        compiler_params=_cparams(("arbitrary",)),
        name="experts",
    )(block_e, row_start, n_used, tok, dest, h_tiles, g.reshape(1, d), w1, w3, w2)


def _moe_combine(h_ref, y0_ref, y1_ref, gt_ref):
    gt = gt_ref[...]
    tm = h_ref.shape[0]
    return h_ref[...] + _tiles_to_rows(y0_ref, tm) * gt[:, 0:1] + _tiles_to_rows(y1_ref, tm) * gt[:, 1:2]


def _combine_kernel(h_ref, y0_ref, y1_ref, gt_ref, g_ref, o_ref):
    o_ref[...] = _rms(_moe_combine(h_ref, y0_ref, y1_ref, gt_ref), g_ref[...])


def _combine_final(h2, y, gates, g_final):
    t, d = h2.shape
    tm = ROW_TILE
    nt = t // tm
    return pl.pallas_call(
        _combine_kernel,
        grid=(nt,),
        in_specs=[pl.BlockSpec((tm, d), lambda i: (i, 0)),
                  pl.BlockSpec((tm * SUBLANE, LANE), lambda i: (i, 0)),
                  pl.BlockSpec((tm * SUBLANE, LANE), lambda i: (nt + i, 0)),
                  pl.BlockSpec((tm, TOP_K), lambda i: (i, 0)),
                  pl.BlockSpec((1, d), lambda i: (0, 0))],
        out_specs=pl.BlockSpec((tm, d), lambda i: (i, 0)),
        out_shape=jax.ShapeDtypeStruct((t, d), F32),
        compiler_params=_cparams(("parallel",)),
        name="combine",
    )(h2, y, y, gates[:TOP_K].T, g_final.reshape(1, d))


def _moe_experts(h_tiles, ids, g, w1, w3, w2, layer):
    t = h_tiles.shape[0] // SUBLANE
    tok, dest, block_e, row_start, n_used = _dispatch_plan(ids, t)
    return _experts(h_tiles, g, tok, dest, block_e, row_start, n_used, w1, w3, w2, layer)


def _ret_cols(w):
    half = HEAD_DIM // 2
    pair = np.concatenate([np.arange(0, half), np.arange(HEAD_DIM, HEAD_DIM + half),
                           np.arange(half, HEAD_DIM), np.arange(HEAD_DIM + half, 2 * HEAD_DIM)])
    qperm = np.concatenate([2 * HEAD_DIM * p + pair for p in range(RET_HEADS // 2)])
    return w[:, np.concatenate([_OFF_Q + qperm, _OFF_K + qperm, np.arange(_OFF_V, _OFF_Z)])]


def kernel(x, mem, g_mix, w_in, w_out, ssd_conv_w, ssd_conv_b, ssd_dt_bias, ssd_a_log, ssd_d, ssd_norm, lru_conv_w, lru_conv_b, lru_lambda, lru_wa, lru_ba, lru_wx, lru_bx, g_ca, g_mem, ca_wq, ca_wkv, ca_wo, g_moe, moe_wg, moe_bg, moe_we, moe_be, moe_w1, moe_w3, moe_w2, g_final):
    b, s, d = x.shape
    depth = g_mix.shape[0]
    h = x.reshape(b * s, d)
    mem2 = mem.reshape(b * mem.shape[1], d)
    moe = None
    for l in range(depth):
        w_ret = _ret_cols(w_in[l]).astype(BF16)
        w_ssd = _ssd_slab_cols(w_in[l]).astype(BF16)
        w_lru = w_in[l][:, _OFF_XR:].astype(BF16)
        if moe is None:
            p_ret, p_ssd, p_lru = _inproj(h, g_mix[l], w_ret, w_ssd, w_lru)
        else:
            h, p_ret, p_ssd, p_lru = _inproj(h, g_mix[l], w_ret, w_ssd, w_lru, moe)
        o_ret = _retention(p_ret, b, s)
        o_ssd = _ssd(p_ssd, b, s, ssd_conv_w[l], ssd_conv_b[l], ssd_dt_bias[l], ssd_a_log[l], ssd_d[l], ssd_norm[l])
        o_lru = _lru(p_lru, b, s, lru_conv_w[l], lru_conv_b[l], lru_lambda[l], lru_wa[l], lru_ba[l], lru_wx[l], lru_bx[l])
        kv = _kv_proj(mem2, g_mem[l], ca_wkv[l].astype(BF16))
        h, h_tiles, ids, gates = _post_mixer(h, b, s, o_ret.reshape(b * s, -1), o_ssd.reshape(b * s, -1),
                                             o_lru.reshape(b * s, -1), *_outproj_weights(w_out[l]),
                                             g_ca[l], ca_wq[l].astype(BF16), kv, ca_wo[l].astype(BF16),
                                             g_moe[l], moe_wg[l], moe_bg[l], moe_we[l], moe_be[l])
        moe = (_moe_experts(h_tiles, ids, g_moe[l], moe_w1, moe_w3, moe_w2, l), gates)
    return _combine_final(h, *moe, g_final).reshape(b, s, d)
```

```python
import functools

import jax
import jax.numpy as jnp
import numpy as np
from jax import lax
from jax.experimental import pallas as pl
from jax.experimental.pallas import tpu as pltpu

D_MODEL = 1024
HEAD_DIM = 64
RET_WIDTH = 384
SSD_WIDTH = 384
LRU_WIDTH = 256
RET_HEADS = 6
SSD_HEADS = 6
SSD_GROUPS = 2
SSD_HPG = 3
SSD_STATE = 128
SSD_CONV_DIM = SSD_WIDTH + 2 * SSD_GROUPS * SSD_STATE
LRU_BLOCKS = 4
LRU_C = 8.0
CONV_WIDTH = 4
ROPE_BASE = 10000.0
CA_HEADS = 4
CA_HEAD_DIM = 256
N_GROUPS = 4
EXPERTS_PER_GROUP = 8
N_EXPERTS = 32
TOP_K = 2
D_EXPERT = 512
EPS = 1e-6

_OFF_Q, _OFF_K, _OFF_V, _OFF_G = 0, 384, 768, 1152
_OFF_Z = 1536
_OFF_XBC = 1920
_OFF_DT = 2816
_OFF_XR = 2822
_OFF_GATE = 3078
D_IN = 3334

LANE = 128
SUBLANE = 8
VMEM_LIMIT = 56 * 1024 * 1024

CHUNK = 128
ROW_TILE = 512
SSD_GW = 256

F32 = jnp.float32
BF16 = jnp.bfloat16


def _cparams(sem):
    return pltpu.CompilerParams(dimension_semantics=sem, vmem_limit_bytes=VMEM_LIMIT)


def _rms(x, g):
    return x * lax.rsqrt(jnp.mean(x * x, axis=-1, keepdims=True) + EPS) * g


def _project(h, g_ref, wr_ref, ws_ref, wl_ref, pr_ref, ps_ref, pq_ref):
    xn = _rms(h, g_ref[...]).astype(BF16)
    pr_ref[...] = jnp.dot(xn, wr_ref[...], preferred_element_type=F32)
    ps_ref[...] = jnp.dot(xn, ws_ref[...], preferred_element_type=F32)
    pq_ref[...] = jnp.dot(xn, wl_ref[...], preferred_element_type=F32)


def _inproj_kernel(x_ref, g_ref, wr_ref, ws_ref, wl_ref, pr_ref, ps_ref, pq_ref):
    _project(x_ref[...], g_ref, wr_ref, ws_ref, wl_ref, pr_ref, ps_ref, pq_ref)


def _combine_inproj_kernel(h_ref, y0_ref, y1_ref, gt_ref, g_ref, wr_ref, ws_ref, wl_ref,
                           ho_ref, pr_ref, ps_ref, pq_ref):
    h = _moe_combine(h_ref, y0_ref, y1_ref, gt_ref)
    ho_ref[...] = h
    _project(h, g_ref, wr_ref, ws_ref, wl_ref, pr_ref, ps_ref, pq_ref)


def _inproj(x2, g, w_ret, w_ssd, w_lru, moe=None):
    t, d = x2.shape
    tm = ROW_TILE
    nt = t // tm
    n_r, n_s, n_l = w_ret.shape[1], w_ssd.shape[1], w_lru.shape[1]
    full = lambda n: pl.BlockSpec((d, n), lambda i: (0, 0))
    row = lambda n: pl.BlockSpec((tm, n), lambda i: (i, 0))
    weights = [pl.BlockSpec((1, d), lambda i: (0, 0)), full(n_r), full(n_s), full(n_l)]
    outs = [row(n_r), row(n_s), row(n_l)]
    shapes = [jax.ShapeDtypeStruct((t, n), F32) for n in (n_r, n_s, n_l)]
    if moe is None:
        return pl.pallas_call(
            _inproj_kernel, grid=(nt,), in_specs=[row(d)] + weights, out_specs=outs, out_shape=shapes,
            compiler_params=_cparams(("parallel",)), name="inproj",
        )(x2, g.reshape(1, d), w_ret, w_ssd, w_lru)
    y, gates = moe
    tiles = lambda off: pl.BlockSpec((tm * SUBLANE, LANE), lambda i: (off + i, 0))
    return pl.pallas_call(
        _combine_inproj_kernel, grid=(nt,),
        in_specs=[row(d), tiles(0), tiles(nt), row(TOP_K)] + weights,
        out_specs=[row(d)] + outs, out_shape=[jax.ShapeDtypeStruct((t, d), F32)] + shapes,
        compiler_params=_cparams(("parallel",)), name="combine_inproj",
    )(x2, y, y, gates[:TOP_K].T, g.reshape(1, d), w_ret, w_ssd, w_lru)


def _ret_kernel(q_ref, k_ref, v_ref, g_ref, cos_ref, sin_ref, d2_ref, dec_ref, gc_ref,
                o_ref, qs_ref, ks_ref, acc_ref):
    s = q_ref.shape[0]
    n = s // CHUNK
    lane = lax.broadcasted_iota(jnp.int32, (1, LANE), 1)
    mq0 = ((lane // 32) % 2 == 0).astype(F32)
    mq1 = 1.0 - mq0
    mv0 = (lane < HEAD_DIM).astype(F32)
    mv1 = 1.0 - mv0
    krow = lax.broadcasted_iota(jnp.int32, (LANE, LANE), 0)
    vcol = lax.broadcasted_iota(jnp.int32, (LANE, LANE), 1)
    bd = (((krow // 32) % 2) == (vcol // HEAD_DIM)).astype(F32)
    zeta_f = dec_ref[0]
    zeta_b = dec_ref[1]
    xi_f = dec_ref[2]
    xi_b = dec_ref[3]
    gc = gc_ref[...]
    d2 = d2_ref[...]

    def rot(x, c):
        rows = pl.ds(c * CHUNK, CHUNK)
        return x * cos_ref[rows, :] + pltpu.roll(x, 64, 1) * sin_ref[rows, :]

    def fwd(c, st):
        rows = pl.ds(pl.multiple_of(c * CHUNK, CHUNK), CHUNK)
        qc = rot(q_ref[rows, :], c)
        kc = rot(k_ref[rows, :], c) * (HEAD_DIM ** -0.5)
        vc = v_ref[rows, :]
        qs_ref[rows, :] = qc
        ks_ref[rows, :] = kc
        q2 = jnp.concatenate([qc * mq0, qc * mq1], axis=0)
        sc = lax.dot_general(q2, kc, (((1,), (1,)), ((), ())), preferred_element_type=F32) * d2
        o2 = jnp.dot(sc, vc, preferred_element_type=F32)
        o = o2[:CHUNK] * mv0 + o2[CHUNK:] * mv1
        o = o + jnp.dot(qc, st, preferred_element_type=F32) * xi_f
        acc_ref[rows, :] = o
        upd = jnp.dot(kc.T, vc * zeta_f, preferred_element_type=F32) * bd
        return st * gc + upd

    lax.fori_loop(0, n, fwd, jnp.zeros((LANE, LANE), F32), unroll=16)

    def bwd(i, st):
        c = n - 1 - i
        rows = pl.ds(pl.multiple_of(c * CHUNK, CHUNK), CHUNK)
        qc = qs_ref[rows, :]
        kc = ks_ref[rows, :]
        vc = v_ref[rows, :]
        o = acc_ref[rows, :] + jnp.dot(qc, st, preferred_element_type=F32) * xi_b
        oo = o * o
        ss0 = jnp.sum(oo * mv0, axis=1, keepdims=True)
        ss1 = jnp.sum(oo * mv1, axis=1, keepdims=True)
        scale = lax.rsqrt(ss0 * (1.0 / HEAD_DIM) + EPS) * mv0 + lax.rsqrt(ss1 * (1.0 / HEAD_DIM) + EPS) * mv1
        gate = g_ref[rows, :]
        gate = gate * jax.nn.sigmoid(gate)
        o_ref[rows, :] = (gate * (o * scale)).astype(o_ref.dtype)
        upd = jnp.dot(kc.T, vc * zeta_b, preferred_element_type=F32) * bd
        return st * gc + upd

    lax.fori_loop(0, n, bwd, jnp.zeros((LANE, LANE), F32), unroll=16)


def _ret_constants(s):
    half = HEAD_DIM // 2
    inv = ROPE_BASE ** (-jnp.arange(half, dtype=F32) / half)
    ang = jnp.arange(s, dtype=F32)[:, None] * inv[None, :]
    cos = jnp.cos(ang)
    sin = jnp.sin(ang)
    cos2 = jnp.concatenate([cos, cos, cos, cos], axis=1)
    sin2 = jnp.concatenate([-sin, -sin, sin, sin], axis=1)
    log_gamma = jnp.log1p(-jnp.exp2(-5.0 - jnp.arange(RET_HEADS, dtype=F32)))
    lg_pair = log_gamma.reshape(RET_HEADS // 2, 2)
    idx = jnp.arange(CHUNK, dtype=F32)
    adiff = jnp.abs(idx[:, None] - idx[None, :])
    d2 = jnp.exp(adiff[None, None] * lg_pair[:, :, None, None])
    d2 = d2.reshape(RET_HEADS // 2, 2 * CHUNK, CHUNK)
    lg_lane = jnp.repeat(lg_pair, HEAD_DIM, axis=1)
    pos = idx[None, :, None]
    lgl = lg_lane[:, None, :]
    dec = jnp.stack([jnp.exp((CHUNK - 1 - pos) * lgl), jnp.exp(pos * lgl),
                     jnp.exp((pos + 1) * lgl), jnp.exp((CHUNK - pos) * lgl)], axis=1)
    gc = jnp.exp(CHUNK * lg_lane)[:, None, :]
    return cos2, sin2, d2, dec, gc


def _retention(p_ret, b, s):
    p3 = p_ret.reshape(b, s, 4 * RET_WIDTH)
    cos2, sin2, d2, dec, gc = _ret_constants(s)
    npair = RET_HEADS // 2
    col = lambda off: pl.BlockSpec((None, s, LANE), lambda i, p: (i, 0, off + p))
    return pl.pallas_call(
        _ret_kernel,
        grid=(b, npair),
        in_specs=[col(0), col(npair), col(2 * npair), col(3 * npair),
                  pl.BlockSpec((s, LANE), lambda i, p: (0, 0)),
                  pl.BlockSpec((s, LANE), lambda i, p: (0, 0)),
                  pl.BlockSpec((None, 2 * CHUNK, CHUNK), lambda i, p: (p, 0, 0)),
                  pl.BlockSpec((None, 4, CHUNK, LANE), lambda i, p: (p, 0, 0, 0)),
                  pl.BlockSpec((None, 1, LANE), lambda i, p: (p, 0, 0))],
        out_specs=pl.BlockSpec((None, s, LANE), lambda i, p: (i, 0, p)),
        out_shape=jax.ShapeDtypeStruct((b, s, RET_WIDTH), BF16),
        scratch_shapes=[pltpu.VMEM((s, LANE), F32), pltpu.VMEM((s, LANE), F32), pltpu.VMEM((s, LANE), F32)],
        compiler_params=_cparams(("parallel", "parallel")),
        name="retention",
    )(p3, p3, p3, p3, cos2, sin2, d2, dec, gc)


_PAD = 8


def _softplus(x):
    return jnp.maximum(x, 0.0) + jnp.log1p(jnp.exp(-jnp.abs(x)))


def _silu(x):
    return x * jax.nn.sigmoid(x)


def _expand_heads(cols, base, lane_head):
    acc = jnp.zeros((cols.shape[0], SSD_GW), F32)
    for r in range(SSD_HPG):
        acc = jnp.where(lane_head == r, cols[:, base + r:base + r + 1], acc)
    return acc


def _dwconv_chunk(xp_ref, cw_ref, cb_ref, c):
    rows = CHUNK + 2 * _PAD
    win = xp_ref[pl.ds(pl.multiple_of(c * CHUNK, CHUNK), rows), :]
    acc = cb_ref[...]
    for t in range(CONV_WIDTH):
        shift = (CONV_WIDTH // 2 - t) % rows
        tap = win if shift == 0 else pltpu.roll(win, shift, 0)
        acc = acc + tap[_PAD:_PAD + CHUNK, :] * cw_ref[t:t + 1, :]
    return acc


def _ssd_kernel(xbc_ref, z_ref, dt_ref, cw_ref, cb_ref, dtb_ref, alog_ref, dsk_ref, nw_ref, o_ref,
                xp_ref, xc_ref, y_ref, pc_ref, dts_ref):
    s = z_ref.shape[0]
    n = s // CHUNK
    w = xbc_ref.shape[1]
    xp_ref[0:_PAD, :] = jnp.zeros((_PAD, w), F32)
    xp_ref[_PAD + s:2 * _PAD + s, :] = jnp.zeros((_PAD, w), F32)

    def copy(c, carry):
        rows = pl.ds(pl.multiple_of(c * CHUNK, CHUNK), CHUNK)
        xp_ref[pl.ds(pl.multiple_of(c * CHUNK + _PAD, _PAD), CHUNK), :] = xbc_ref[rows, :]
        return carry

    lax.fori_loop(0, n, copy, 0)

    lane_head = lax.broadcasted_iota(jnp.int32, (1, SSD_GW), 1) // HEAD_DIM
    ii = lax.broadcasted_iota(jnp.int32, (CHUNK, CHUNK), 0)
    jj = lax.broadcasted_iota(jnp.int32, (CHUNK, CHUNK), 1)
    lower = ii >= jj
    tri = lower.astype(F32)
    a_neg = -jnp.exp(alog_ref[...])
    dt_bias = dtb_ref[...]

    def fwd(c, hf):
        rows = pl.ds(pl.multiple_of(c * CHUNK, CHUNK), CHUNK)
        xc = _silu(_dwconv_chunk(xp_ref, cw_ref, cb_ref, c))
        xc_ref[rows, :] = xc
        xs = xc[:, 0:SSD_GW]
        bm = xc[:, SSD_GW:SSD_GW + SSD_STATE]
        cm = xc[:, SSD_GW + SSD_STATE:]
        dt = _softplus(dt_ref[rows, :] + dt_bias)
        dts_ref[rows, :] = dt
        la = dt * a_neg
        p = jnp.dot(tri, la, precision=lax.Precision.HIGHEST, preferred_element_type=F32)
        pc_ref[rows, :] = p
        e = p - la
        pt = p.T
        et = e.T
        dtt = dt.T
        g = lax.dot_general(cm, bm, (((1,), (1,)), ((), ())), preferred_element_type=F32)
        y = jnp.zeros((CHUNK, SSD_GW), F32)
        for r in range(SSD_HPG):
            arg = jnp.where(lower, p[:, r:r + 1] - pt[r:r + 1, :], et[3 + r:4 + r, :] - e[:, 3 + r:4 + r])
            dtj = jnp.where(lower, dtt[r:r + 1, :], dtt[3 + r:4 + r, :])
            m = g * (jnp.exp(arg) * dtj)
            y = y + jnp.dot(m, jnp.where(lane_head == r, xs, 0.0), preferred_element_type=F32)
        pf = _expand_heads(p, 0, lane_head)
        pf_last = _expand_heads(p[CHUNK - 1:CHUNK, :], 0, lane_head)
        dtf = _expand_heads(dt, 0, lane_head)
        y = y + jnp.dot(cm, hf, preferred_element_type=F32) * jnp.exp(pf)
        y_ref[rows, :] = y
        upd = jnp.dot(bm.T, xs * (jnp.exp(pf_last - pf) * dtf), preferred_element_type=F32)
        return hf * jnp.exp(pf_last) + upd

    lax.fori_loop(0, n, fwd, jnp.zeros((SSD_STATE, SSD_GW), F32), unroll=2)

    dsk = dsk_ref[...]
    nw = nw_ref[...]

    def bwd(i, hb):
        c = n - 1 - i
        rows = pl.ds(pl.multiple_of(c * CHUNK, CHUNK), CHUNK)
        xs = xc_ref[rows, 0:SSD_GW]
        bm = xc_ref[rows, SSD_GW:SSD_GW + SSD_STATE]
        cm = xc_ref[rows, SSD_GW + SSD_STATE:]
        dt = dts_ref[rows, :]
        p = pc_ref[rows, :]
        e = p - dt * a_neg
        eb = _expand_heads(e, 3, lane_head)
        tb = _expand_heads(p[CHUNK - 1:CHUNK, :], 3, lane_head)
        dtb = _expand_heads(dt, 3, lane_head)
        y = y_ref[rows, :] + jnp.dot(cm, hb, preferred_element_type=F32) * jnp.exp(tb - eb)
        y = (y + dsk * xs) * _silu(z_ref[rows, :])
        ms = jnp.sum(y * y, axis=1, keepdims=True) * (1.0 / (SSD_HPG * HEAD_DIM))
        o_ref[rows, :] = (y * lax.rsqrt(ms + EPS) * nw).astype(o_ref.dtype)
        upd = jnp.dot(bm.T, xs * (jnp.exp(eb) * dtb), preferred_element_type=F32)
        return hb * jnp.exp(tb) + upd

    lax.fori_loop(0, n, bwd, jnp.zeros((SSD_STATE, SSD_GW), F32), unroll=2)


_SSD_XBC_W = SSD_GW + 2 * SSD_STATE
_SSD_SLAB = SSD_GROUPS * (_SSD_XBC_W + SSD_GW + LANE)


def _ssd_xbc_cols(a, off):
    gw = SSD_HPG * HEAD_DIM
    zero = jnp.zeros(a.shape[:-1] + (SSD_GW - gw,), a.dtype)
    out = []
    for g in range(SSD_GROUPS):
        b0 = off + SSD_WIDTH + g * SSD_STATE
        c0 = b0 + SSD_GROUPS * SSD_STATE
        out += [a[..., off + g * gw:off + (g + 1) * gw], zero, a[..., b0:b0 + SSD_STATE], a[..., c0:c0 + SSD_STATE]]
    return jnp.concatenate(out, axis=-1)


def _ssd_slab_cols(w):
    gw = SSD_HPG * HEAD_DIM
    zero = w.shape[1]
    pad = lambda k: np.full(k, zero, np.int64)
    xbc, zz, dtc = [], [], []
    for g in range(SSD_GROUPS):
        b0 = _OFF_XBC + SSD_WIDTH + g * SSD_STATE
        c0 = b0 + SSD_GROUPS * SSD_STATE
        xbc += [_OFF_XBC + g * gw + np.arange(gw), pad(SSD_GW - gw), b0 + np.arange(SSD_STATE), c0 + np.arange(SSD_STATE)]
        zz += [_OFF_Z + g * gw + np.arange(gw), pad(SSD_GW - gw)]
        heads = _OFF_DT + g * SSD_HPG + np.arange(SSD_HPG)
        dtc += [heads, heads, pad(LANE - 2 * SSD_HPG)]
    w0 = jnp.concatenate([w, jnp.zeros((w.shape[0], 1), w.dtype)], axis=1)
    return w0[:, np.concatenate(xbc + zz + dtc)]


def _ssd(p_ssd, b, s, conv_w, conv_b, dt_bias, a_log, d_skip, norm_w):
    p3 = p_ssd.reshape(b, s, _SSD_SLAB)
    cw = _ssd_xbc_cols(conv_w, 0)
    cb = _ssd_xbc_cols(conv_b[None, :], 0)
    gw = SSD_HPG * HEAD_DIM
    zpad = jnp.zeros((SSD_GROUPS, LANE - 2 * SSD_HPG), F32)
    grp = lambda v: jnp.concatenate([v[0].reshape(SSD_GROUPS, SSD_HPG), v[1].reshape(SSD_GROUPS, SSD_HPG), zpad], axis=1)
    dtb = grp(dt_bias)[:, None, :]
    alog = grp(a_log)[:, None, :]
    lanes = lambda v: jnp.pad(v.reshape(SSD_GROUPS, gw), ((0, 0), (0, SSD_GW - gw)))[:, None, :]
    dsk = lanes(jnp.repeat(d_skip, HEAD_DIM))
    nw = lanes(norm_w)
    nxb = _SSD_XBC_W // LANE
    grid_spec = dict(
        grid=(b, SSD_GROUPS),
        in_specs=[pl.BlockSpec((None, s, _SSD_XBC_W), lambda i, g: (i, 0, g)),
                  pl.BlockSpec((None, s, SSD_GW), lambda i, g: (i, 0, SSD_GROUPS * _SSD_XBC_W // SSD_GW + g)),
                  pl.BlockSpec((None, s, LANE), lambda i, g: (i, 0, SSD_GROUPS * (nxb + SSD_GW // LANE) + g)),
                  pl.BlockSpec((CONV_WIDTH, _SSD_XBC_W), lambda i, g: (0, g)),
                  pl.BlockSpec((1, _SSD_XBC_W), lambda i, g: (0, g)),
                  pl.BlockSpec((None, 1, LANE), lambda i, g: (g, 0, 0)),
                  pl.BlockSpec((None, 1, LANE), lambda i, g: (g, 0, 0)),
                  pl.BlockSpec((None, 1, SSD_GW), lambda i, g: (g, 0, 0)),
                  pl.BlockSpec((None, 1, SSD_GW), lambda i, g: (g, 0, 0))],
        out_specs=pl.BlockSpec((None, s, SSD_GW), lambda i, g: (i, 0, g)),
    )
    return pl.pallas_call(
        _ssd_kernel,
        out_shape=jax.ShapeDtypeStruct((b, s, SSD_GROUPS * SSD_GW), BF16),
        scratch_shapes=[pltpu.VMEM((s + 2 * _PAD, _SSD_XBC_W), F32), pltpu.VMEM((s, _SSD_XBC_W), F32),
                        pltpu.VMEM((s, SSD_GW), F32), pltpu.VMEM((s, LANE), F32), pltpu.VMEM((s, LANE), F32)],
        compiler_params=_cparams(("parallel", "parallel")),
        name="ssd",
        **grid_spec,
    )(p3, p3, p3, cw, cb, dtb, alog, dsk, nw)


def _scan_chunk(a, u, h, reverse):
    n_groups = CHUNK // SUBLANE
    width = a.shape[1]
    a = a.reshape(n_groups, SUBLANE, width)
    u = u.reshape(n_groups, SUBLANE, width)
    sub = lax.broadcasted_iota(jnp.int32, (1, SUBLANE, 1), 1)
    d = 1
    while d < SUBLANE:
        keep = (sub < SUBLANE - d) if reverse else (sub >= d)
        shift = SUBLANE - d if reverse else d
        a_sh = jnp.where(keep, pltpu.roll(a, shift, 1), 1.0)
        u_sh = jnp.where(keep, pltpu.roll(u, shift, 1), 0.0)
        u = a * u_sh + u
        a = a * a_sh
        d *= 2
    a = a.reshape(CHUNK, width)
    u = u.reshape(CHUNK, width)
    out = [None] * n_groups
    for g in (range(n_groups - 1, -1, -1) if reverse else range(n_groups)):
        rows = slice(g * SUBLANE, (g + 1) * SUBLANE)
        hg = u[rows] + a[rows] * h
        out[g] = hg
        h = hg[0:1] if reverse else hg[SUBLANE - 1:SUBLANE]
    return jnp.concatenate(out, axis=0), h


def _lru_kernel(x_ref, cw_ref, cb_ref, wg_ref, bg_ref, lam_ref, o_ref, xp_ref, hf_ref, ab_ref, ub_ref):
    s = x_ref.shape[0]
    n = s // CHUNK
    w = LRU_WIDTH
    xp_ref[0:_PAD, :] = jnp.zeros((_PAD, w), F32)
    xp_ref[_PAD + s:2 * _PAD + s, :] = jnp.zeros((_PAD, w), F32)

    def copy(c, carry):
        rows = pl.ds(pl.multiple_of(c * CHUNK, CHUNK), CHUNK)
        xp_ref[pl.ds(pl.multiple_of(c * CHUNK + _PAD, _PAD), CHUNK), :] = x_ref[rows, 0:w]
        return carry

    lax.fori_loop(0, n, copy, 0)
    nsp = -LRU_C * _softplus(-lam_ref[...])

    def gates(pre, xc, k):
        r = jax.nn.sigmoid(pre[:, 2 * k * w:(2 * k + 1) * w])
        i = jax.nn.sigmoid(pre[:, (2 * k + 1) * w:(2 * k + 2) * w])
        log_a = r * nsp[k:k + 1, :]
        a = jnp.exp(log_a)
        return a, jnp.sqrt(-jnp.tanh(log_a) * (a * a + 1.0)) * (i * xc)

    def fwd(c, h):
        rows = pl.ds(pl.multiple_of(c * CHUNK, CHUNK), CHUNK)
        xc = _dwconv_chunk(xp_ref, cw_ref, cb_ref, c)
        pre = jnp.dot(xc.astype(BF16), wg_ref[...], preferred_element_type=F32) + bg_ref[...]
        a_b, u_b = gates(pre, xc, 1)
        ab_ref[rows, :] = a_b
        ub_ref[rows, :] = u_b
        a_f, u_f = gates(pre, xc, 0)
        hc, h = _scan_chunk(a_f, u_f, h, False)
        hf_ref[rows, :] = hc
        return h

    lax.fori_loop(0, n, fwd, jnp.zeros((1, w), F32))

    def bwd(i, h):
        c = n - 1 - i
        rows = pl.ds(pl.multiple_of(c * CHUNK, CHUNK), CHUNK)
        hc, h = _scan_chunk(ab_ref[rows, :], ub_ref[rows, :], h, True)
        gate = x_ref[rows, w:2 * w]
        o_ref[rows, :] = ((hf_ref[rows, :] + hc) * jax.nn.gelu(gate)).astype(o_ref.dtype)
        return h

    lax.fori_loop(0, n, bwd, jnp.zeros((1, w), F32))


def _block_diag(wblk):
    eye = jnp.eye(LRU_BLOCKS, dtype=wblk.dtype)
    return jnp.einsum('hij,hk->hikj', wblk, eye).reshape(LRU_WIDTH, LRU_WIDTH)


def _lru(p_lru, b, s, conv_w, conv_b, lam, wa, ba, wx, bx):
    p3 = p_lru.reshape(b, s, 2 * LRU_WIDTH)
    wg = jnp.concatenate([_block_diag(wa[0]), _block_diag(wx[0]), _block_diag(wa[1]), _block_diag(wx[1])], axis=1).astype(BF16)
    bg = jnp.concatenate([ba[0], bx[0], ba[1], bx[1]])[None, :]
    w = LRU_WIDTH
    const = lambda shape: pl.BlockSpec(shape, lambda i: (0,) * len(shape))
    return pl.pallas_call(
        _lru_kernel,
        grid=(b,),
        in_specs=[pl.BlockSpec((None, s, 2 * w), lambda i: (i, 0, 0)),
                  const((CONV_WIDTH, w)), const((1, w)), const((w, 4 * w)), const((1, 4 * w)), const((2, w))],
        out_specs=pl.BlockSpec((None, s, w), lambda i: (i, 0, 0)),
        out_shape=jax.ShapeDtypeStruct((b, s, w), BF16),
        scratch_shapes=[pltpu.VMEM((s + 2 * _PAD, w), F32), pltpu.VMEM((s, w), F32),
                        pltpu.VMEM((s, w), F32), pltpu.VMEM((s, w), F32)],
        compiler_params=_cparams(("parallel",)),
        name="rglru",
    )(p3, conv_w, conv_b[None, :], wg, bg, lam)


def _outproj_weights(w_out):
    gw = SSD_HPG * HEAD_DIM
    zero = jnp.zeros((SSD_GW - gw, w_out.shape[1]), w_out.dtype)
    rows = [p for g in range(SSD_GROUPS) for p in (w_out[RET_WIDTH + g * gw:RET_WIDTH + (g + 1) * gw], zero)]
    w_ssd = jnp.concatenate(rows, axis=0)
    return (w_out[:RET_WIDTH].astype(BF16), w_ssd.astype(BF16), w_out[RET_WIDTH + SSD_WIDTH:].astype(BF16))


def _kv_kernel(m_ref, g_ref, w_ref, o_ref):
    mn = _rms(m_ref[...], g_ref[...]).astype(BF16)
    o_ref[...] = jnp.dot(mn, w_ref[...], preferred_element_type=F32).astype(o_ref.dtype)


def _kv_proj(mem2, g, wkv):
    t, d = mem2.shape
    n = wkv.shape[1]
    tm = 256
    return pl.pallas_call(
        _kv_kernel,
        grid=(t // tm,),
        in_specs=[pl.BlockSpec((tm, d), lambda i: (i, 0)), pl.BlockSpec((1, d), lambda i: (0, 0)),
                  pl.BlockSpec((d, n), lambda i: (0, 0))],
        out_specs=pl.BlockSpec((tm, n), lambda i: (i, 0)),
        out_shape=jax.ShapeDtypeStruct((t, n), BF16),
        compiler_params=_cparams(("parallel",)),
        name="kvproj",
    )(mem2, g.reshape(1, d), wkv)


def _xattn_tile(h, g, wq_ref, k_ref, v_ref, wo_ref):
    xn = _rms(h, g).astype(BF16)
    q = jnp.dot(xn, wq_ref[...], preferred_element_type=F32).astype(BF16)
    outs = []
    for hd in range(CA_HEADS):
        cols = slice(hd * CA_HEAD_DIM, (hd + 1) * CA_HEAD_DIM)
        sc = lax.dot_general(q[:, cols], k_ref[:, cols], (((1,), (1,)), ((), ())), preferred_element_type=F32)
        sc = sc * (CA_HEAD_DIM ** -0.5)
        e = jnp.exp(sc - jnp.max(sc, axis=-1, keepdims=True))
        p = e / jnp.sum(e, axis=-1, keepdims=True)
        outs.append(jnp.dot(p.astype(BF16), v_ref[:, cols], preferred_element_type=F32).astype(BF16))
    o = jnp.concatenate(outs, axis=-1)
    return h + jnp.dot(o, wo_ref[...], preferred_element_type=F32)


_ROUTER_ROWS = SUBLANE + N_EXPERTS


def _first_argmax(v, row):
    m = jnp.max(v, axis=0, keepdims=True)
    return m, jnp.min(jnp.where(v == m, row, SUBLANE), axis=0, keepdims=True)


def _router_tile(h, g, w_ref, b_ref):
    xn = _rms(h, g).astype(BF16)
    lg = lax.dot_general(w_ref[...], xn, (((1,), (1,)), ((), ())), preferred_element_type=F32) + b_ref[...]
    tm = lg.shape[1]
    row = lax.broadcasted_iota(jnp.int32, (SUBLANE, tm), 0)
    gl = jnp.where(row < N_GROUPS, lg[0:SUBLANE], -jnp.inf)
    ge = jnp.exp(gl - jnp.max(gl, axis=0, keepdims=True))
    gp_all = ge / jnp.sum(ge, axis=0, keepdims=True)
    gp, gi = _first_argmax(gp_all, row)
    el = lg[SUBLANE:2 * SUBLANE]
    for g in range(1, N_GROUPS):
        el = jnp.where(gi == g, lg[SUBLANE * (g + 1):SUBLANE * (g + 2)], el)
    ee = jnp.exp(el - jnp.max(el, axis=0, keepdims=True))
    ep = ee / jnp.sum(ee, axis=0, keepdims=True)
    v1, i1 = _first_argmax(ep, row)
    v2, i2 = _first_argmax(jnp.where(row == i1, -1.0, ep), row)
    den = v1 + v2
    ids = jnp.where(row == 0, gi * EXPERTS_PER_GROUP + i1, jnp.where(row == 1, gi * EXPERTS_PER_GROUP + i2, 0))
    gates = jnp.where(row == 0, gp * v1 / den, jnp.where(row == 1, gp * v2 / den, 0.0))
    return ids, gates


assert D_MODEL == SUBLANE * LANE


def _token_tile(r):
    return pl.ds(pl.multiple_of(r * SUBLANE, SUBLANE), SUBLANE)


def _rows_to_tiles(ref, x):
    for j in range(SUBLANE):
        ref[pl.ds(j, x.shape[0], stride=SUBLANE), :] = x[:, j * LANE:(j + 1) * LANE]


def _tiles_to_rows(ref, n):
    return jnp.concatenate([ref[pl.ds(j, n, stride=SUBLANE), :] for j in range(SUBLANE)], axis=1)


def _post_kernel(h_ref, a_ref, b_ref, c_ref, wa_ref, wb_ref, wc_ref, gca_ref, wq_ref, k_ref, v_ref, wo_ref,
                 gmoe_ref, wr_ref, br_ref, o_ref, ot_ref, ids_ref, gates_ref):
    h = h_ref[...]
    h = h + jnp.dot(a_ref[...], wa_ref[...], preferred_element_type=F32)
    h = h + jnp.dot(b_ref[...], wb_ref[...], preferred_element_type=F32)
    h = h + jnp.dot(c_ref[...], wc_ref[...], preferred_element_type=F32)
    h = _xattn_tile(h, gca_ref[...], wq_ref, k_ref, v_ref, wo_ref)
    o_ref[...] = h
    _rows_to_tiles(ot_ref, h)
    ids, gates = _router_tile(h, gmoe_ref[...], wr_ref, br_ref)
    ids_ref[...] = ids
    gates_ref[...] = gates


def _post_mixer(h2, b, s, o_ret, o_ssd, o_lru, w_ret, w_ssd, w_lru, g_ca, wq, kv, wo, g_moe, wg, bg, we, be):
    t, d = h2.shape
    m = kv.shape[0] // b
    ts = ROW_TILE
    nts = s // ts
    zrow = jnp.zeros((SUBLANE - N_GROUPS, d), F32)
    wr = jnp.concatenate([wg.T, zrow, we.T], axis=0).astype(BF16)
    br = jnp.concatenate([bg, jnp.zeros((SUBLANE - N_GROUPS,), F32), be])[:, None]
    row = lambda n: pl.BlockSpec((ts, n), lambda i, j: (i * nts + j, 0))
    const = lambda r, c: pl.BlockSpec((r, c), lambda i, j: (0, 0))
    ka, kb, kc = o_ret.shape[1], o_ssd.shape[1], o_lru.shape[1]
    lanes = pl.BlockSpec((SUBLANE, ts), lambda i, j: (0, i * nts + j))
    return pl.pallas_call(
        _post_kernel,
        grid=(b, nts),
        in_specs=[row(d), row(ka), row(kb), row(kc), const(ka, d), const(kb, d), const(kc, d),
                  const(1, d), const(d, d),
                  pl.BlockSpec((m, d), lambda i, j: (i, 0)),
                  pl.BlockSpec((m, d), lambda i, j: (i, 1)),
                  const(d, d), const(1, d), const(_ROUTER_ROWS, d), const(_ROUTER_ROWS, 1)],
        out_specs=[row(d), pl.BlockSpec((ts * SUBLANE, LANE), lambda i, j: (i * nts + j, 0)), lanes, lanes],
        out_shape=[jax.ShapeDtypeStruct((t, d), F32), jax.ShapeDtypeStruct((t * SUBLANE, LANE), F32),
                   jax.ShapeDtypeStruct((SUBLANE, t), jnp.int32), jax.ShapeDtypeStruct((SUBLANE, t), F32)],
        compiler_params=_cparams(("parallel", "parallel")),
        name="post_mixer",
    )(h2, o_ret, o_ssd, o_lru, w_ret, w_ssd, w_lru, g_ca.reshape(1, d), wq, kv, kv, wo,
      g_moe.reshape(1, d), wr, br)


MOE_BLK = 512


_ASSIGN_BITS = 17


def _dispatch_plan(ids, t):
    n_assign = t * TOP_K
    assert n_assign <= 1 << _ASSIGN_BITS
    e = ids[:TOP_K].reshape(n_assign)
    key =jnp.left_shift(e, _ASSIGN_BITS) | jnp.arange(n_assign, dtype=jnp.int32)
    order = jnp.sort(key) & ((1 << _ASSIGN_BITS) - 1)
    order = jnp.concatenate([order, jnp.zeros((MOE_BLK,), jnp.int32)])
    experts = jnp.arange(N_EXPERTS, dtype=jnp.int32)
    counts = jnp.sum((e[None, :] == experts[:, None]).astype(jnp.int32), axis=1)
    start = jnp.cumsum(counts) - counts
    nb = (counts + MOE_BLK - 1) // MOE_BLK
    blk_end = jnp.cumsum(nb)
    n_blocks = n_assign // MOE_BLK + N_EXPERTS
    j = jnp.arange(n_blocks, dtype=jnp.int32)
    be = jnp.minimum(jnp.sum((blk_end[None, :] <= j[:, None]).astype(jnp.int32), axis=1), N_EXPERTS - 1)
    onehot = (be[:, None] == experts[None, :]).astype(jnp.int32)
    pick = lambda v: jnp.sum(onehot * v[None, :], axis=1)
    r = j - pick(blk_end - nb)
    row_start = jnp.clip(pick(start) + r * MOE_BLK, 0, n_assign)
    n_rows = jnp.clip(pick(counts) - r * MOE_BLK, 0, MOE_BLK)
    tok = jnp.where(order >= t, order - t, order)
    rr = jnp.arange(MOE_BLK, dtype=jnp.int32)[None, :]
    dump = n_assign + (j % 2)[:, None] * MOE_BLK + rr
    dest = jnp.where(rr < n_rows[:, None], order[row_start[:, None] + rr], dump).reshape(n_blocks * MOE_BLK)
    return tok, dest, be, row_start, blk_end[-1:].astype(jnp.int32)


def _expert_kernel(be_ref, rs_ref, nu_ref, tok_ref, dst_ref, h_hbm, g_ref, w1_ref, w3_ref, w2_ref, y_hbm,
                   xbuf, ybuf, wb1, wb3, wb2, gsem, ssem, *, n_tokens):
    n_assign = TOP_K * n_tokens
    blk_rows = MOE_BLK * SUBLANE
    i = pl.program_id(0)
    n_used = nu_ref[0]
    slot = i % 2

    def gather(blk, s):
        base = rs_ref[blk]

        def body(r, carry):
            tok = tok_ref[base + r]
            pltpu.make_async_copy(h_hbm.at[_token_tile(tok), :], xbuf.at[s, _token_tile(r), :], gsem.at[s]).start()
            return carry
        lax.fori_loop(0, MOE_BLK, body, 0, unroll=8)

    def scatter(blk, s):
        base = blk * MOE_BLK

        def body(r, carry):
            dest = dst_ref[base + r]
            pltpu.make_async_copy(ybuf.at[s, _token_tile(r), :], y_hbm.at[_token_tile(dest), :], ssem.at[s]).start()
            return carry
        lax.fori_loop(0, MOE_BLK, body, 0, unroll=8)

    def wait_gather(s):
        pltpu.make_async_copy(h_hbm.at[pl.ds(0, blk_rows), :], xbuf.at[s], gsem.at[s]).wait()

    def wait_scatter(s):
        pltpu.make_async_copy(ybuf.at[s], y_hbm.at[pl.ds(0, blk_rows), :], ssem.at[s]).wait()

    @pl.when(i == 0)
    def _():
        ybuf[0] = jnp.zeros((blk_rows, LANE), F32)
        for s in range(2):
            dump = pl.ds((n_assign + s * MOE_BLK) * SUBLANE, blk_rows)
            fill = pltpu.make_async_copy(ybuf.at[0], y_hbm.at[dump, :], ssem.at[0])
            fill.start()
            fill.wait()
        gather(0, 0)

    @pl.when(i + 1 < n_used)
    def _():
        gather(i + 1, 1 - slot)

    @pl.when(i < n_used)
    def _():
        @pl.when((i == 0) | (be_ref[i] != be_ref[jnp.maximum(i - 1, 0)]))
        def _():
            wb1[...] = w1_ref[...].astype(BF16)
            wb3[...] = w3_ref[...].astype(BF16)
            wb2[...] = w2_ref[...].astype(BF16)

        wait_gather(slot)

        @pl.when(i >= 2)
        def _():
            wait_scatter(slot)

        xn = _rms(_tiles_to_rows(xbuf.at[slot], MOE_BLK), g_ref[...]).astype(BF16)
        h1 = jnp.dot(xn, wb1[...], preferred_element_type=F32)
        h3 = jnp.dot(xn, wb3[...], preferred_element_type=F32)
        hid = (_silu(h1) * h3).astype(BF16)
        _rows_to_tiles(ybuf.at[slot], jnp.dot(hid, wb2[...], preferred_element_type=F32))
        scatter(i, slot)

        @pl.when(i == n_used - 1)
        def _():
            wait_scatter(slot)

            @pl.when(i >= 1)
            def _():
                wait_scatter(1 - slot)


def _experts(h_tiles, g, tok, dest, block_e, row_start, n_used, w1, w3, w2, layer):
    t, d = h_tiles.shape[0] // SUBLANE, D_MODEL
    n_assign = t * TOP_K
    n_blocks = block_e.shape[0]
    de = w1.shape[3]
    wmap = lambda i, be, *_: (layer, be[i], 0, 0)
    grid_spec = pltpu.PrefetchScalarGridSpec(
        num_scalar_prefetch=5,
        grid=(n_blocks,),
        in_specs=[pl.BlockSpec(memory_space=pl.ANY),
                  pl.BlockSpec((1, d), lambda i, *_: (0, 0)),
                  pl.BlockSpec((None, None, d, de), wmap),
                  pl.BlockSpec((None, None, d, de), wmap),
                  pl.BlockSpec((None, None, de, d), wmap)],
        out_specs=pl.BlockSpec(memory_space=pl.ANY),
        scratch_shapes=[pltpu.VMEM((2, MOE_BLK * SUBLANE, LANE), F32), pltpu.VMEM((2, MOE_BLK * SUBLANE, LANE), F32),
                        pltpu.VMEM((d, de), BF16), pltpu.VMEM((d, de), BF16), pltpu.VMEM((de, d), BF16),
                        pltpu.SemaphoreType.DMA((2,)), pltpu.SemaphoreType.DMA((2,))],
    )
    return pl.pallas_call(
        functools.partial(_expert_kernel, n_tokens=t),
        grid_spec=grid_spec,
        out_shape=jax.ShapeDtypeStruct(((n_assign + 2 * MOE_BLK) * SUBLANE, LANE), F32),
        compiler_params=_cparams(("arbitrary",)),
        name="experts",
    )(block_e, row_start, n_used, tok, dest, h_tiles, g.reshape(1, d), w1, w3, w2)


def _moe_combine(h_ref, y0_ref, y1_ref, gt_ref):
    gt = gt_ref[...]
    tm = h_ref.shape[0]
    return h_ref[...] + _tiles_to_rows(y0_ref, tm) * gt[:, 0:1] + _tiles_to_rows(y1_ref, tm) * gt[:, 1:2]


def _combine_kernel(h_ref, y0_ref, y1_ref, gt_ref, g_ref, o_ref):
    o_ref[...] = _rms(_moe_combine(h_ref, y0_ref, y1_ref, gt_ref), g_ref[...])


def _combine_final(h2, y, gates, g_final):
    t, d = h2.shape
    tm = ROW_TILE
    nt = t // tm
    return pl.pallas_call(
        _combine_kernel,
        grid=(nt,),
        in_specs=[pl.BlockSpec((tm, d), lambda i: (i, 0)),
                  pl.BlockSpec((tm * SUBLANE, LANE), lambda i: (i, 0)),
                  pl.BlockSpec((tm * SUBLANE, LANE), lambda i: (nt + i, 0)),
                  pl.BlockSpec((tm, TOP_K), lambda i: (i, 0)),
                  pl.BlockSpec((1, d), lambda i: (0, 0))],
        out_specs=pl.BlockSpec((tm, d), lambda i: (i, 0)),
        out_shape=jax.ShapeDtypeStruct((t, d), F32),
        compiler_params=_cparams(("parallel",)),
        name="combine",
    )(h2, y, y, gates[:TOP_K].T, g_final.reshape(1, d))


def _moe_experts(h_tiles, ids, g, w1, w3, w2, layer):
    t = h_tiles.shape[0] // SUBLANE
    tok, dest, block_e, row_start, n_used = _dispatch_plan(ids, t)
    return _experts(h_tiles, g, tok, dest, block_e, row_start, n_used, w1, w3, w2, layer)


def _ret_cols(w):
    half = HEAD_DIM // 2
    pair = np.concatenate([np.arange(0, half), np.arange(HEAD_DIM, HEAD_DIM + half),
                           np.arange(half, HEAD_DIM), np.arange(HEAD_DIM + half, 2 * HEAD_DIM)])
    qperm = np.concatenate([2 * HEAD_DIM * p + pair for p in range(RET_HEADS // 2)])
    return w[:, np.concatenate([_OFF_Q + qperm, _OFF_K + qperm, np.arange(_OFF_V, _OFF_Z)])]


def kernel(x, mem, g_mix, w_in, w_out, ssd_conv_w, ssd_conv_b, ssd_dt_bias, ssd_a_log, ssd_d, ssd_norm, lru_conv_w, lru_conv_b, lru_lambda, lru_wa, lru_ba, lru_wx, lru_bx, g_ca, g_mem, ca_wq, ca_wkv, ca_wo, g_moe, moe_wg, moe_bg, moe_we, moe_be, moe_w1, moe_w3, moe_w2, g_final):
    b, s, d = x.shape
    depth = g_mix.shape[0]
    h = x.reshape(b * s, d)
    mem2 = mem.reshape(b * mem.shape[1], d)
    moe = None
    for l in range(depth):
        w_ret = _ret_cols(w_in[l]).astype(BF16)
        w_ssd = _ssd_slab_cols(w_in[l]).astype(BF16)
        w_lru = w_in[l][:, _OFF_XR:].astype(BF16)
        if moe is None:
            p_ret, p_ssd, p_lru = _inproj(h, g_mix[l], w_ret, w_ssd, w_lru)
        else:
            h, p_ret, p_ssd, p_lru = _inproj(h, g_mix[l], w_ret, w_ssd, w_lru, moe)
        o_ret = _retention(p_ret, b, s)
        o_ssd = _ssd(p_ssd, b, s, ssd_conv_w[l], ssd_conv_b[l], ssd_dt_bias[l], ssd_a_log[l], ssd_d[l], ssd_norm[l])
        o_lru = _lru(p_lru, b, s, lru_conv_w[l], lru_conv_b[l], lru_lambda[l], lru_wa[l], lru_ba[l], lru_wx[l], lru_bx[l])
        kv = _kv_proj(mem2, g_mem[l], ca_wkv[l].astype(BF16))
        h, h_tiles, ids, gates = _post_mixer(h, b, s, o_ret.reshape(b * s, -1), o_ssd.reshape(b * s, -1),
                                             o_lru.reshape(b * s, -1), *_outproj_weights(w_out[l]),
                                             g_ca[l], ca_wq[l].astype(BF16), kv, ca_wo[l].astype(BF16),
                                             g_moe[l], moe_wg[l], moe_bg[l], moe_we[l], moe_be[l])
        moe = (_moe_experts(h_tiles, ids, g_moe[l], moe_w1, moe_w3, moe_w2, l), gates)
    return _combine_final(h, *moe, g_final).reshape(b, s, d)
```

```python
import functools

import jax
import jax.numpy as jnp
import numpy as np
from jax import lax
from jax.experimental import pallas as pl
from jax.experimental.pallas import tpu as pltpu

D_MODEL = 1024
HEAD_DIM = 64
RET_WIDTH = 384
SSD_WIDTH = 384
LRU_WIDTH = 256
RET_HEADS = 6
SSD_HEADS = 6
SSD_GROUPS = 2
SSD_HPG = 3
SSD_STATE = 128
SSD_CONV_DIM = SSD_WIDTH + 2 * SSD_GROUPS * SSD_STATE
LRU_BLOCKS = 4
LRU_C = 8.0
CONV_WIDTH = 4
ROPE_BASE = 10000.0
CA_HEADS = 4
CA_HEAD_DIM = 256
N_GROUPS = 4
EXPERTS_PER_GROUP = 8
N_EXPERTS = 32
TOP_K = 2
D_EXPERT = 512
EPS = 1e-6

_OFF_Q, _OFF_K, _OFF_V, _OFF_G = 0, 384, 768, 1152
_OFF_Z = 1536
_OFF_XBC = 1920
_OFF_DT = 2816
_OFF_XR = 2822
_OFF_GATE = 3078
D_IN = 3334

LANE = 128
SUBLANE = 8
VMEM_LIMIT = 56 * 1024 * 1024

CHUNK = 128
ROW_TILE = 512
SSD_GW = 256

F32 = jnp.float32
BF16 = jnp.bfloat16


def _cparams(sem):
    return pltpu.CompilerParams(dimension_semantics=sem, vmem_limit_bytes=VMEM_LIMIT)


def _rms(x, g):
    return x * lax.rsqrt(jnp.mean(x * x, axis=-1, keepdims=True) + EPS) * g


def _project(h, g_ref, wr_ref, ws_ref, wl_ref, pr_ref, ps_ref, pq_ref):
    xn = _rms(h, g_ref[...]).astype(BF16)
    pr_ref[...] = jnp.dot(xn, wr_ref[...], preferred_element_type=F32)
    ps_ref[...] = jnp.dot(xn, ws_ref[...], preferred_element_type=F32)
    pq_ref[...] = jnp.dot(xn, wl_ref[...], preferred_element_type=F32)


def _inproj_kernel(x_ref, g_ref, wr_ref, ws_ref, wl_ref, pr_ref, ps_ref, pq_ref):
    _project(x_ref[...], g_ref, wr_ref, ws_ref, wl_ref, pr_ref, ps_ref, pq_ref)


def _combine_inproj_kernel(h_ref, y0_ref, y1_ref, gt_ref, g_ref, wr_ref, ws_ref, wl_ref,
                           ho_ref, pr_ref, ps_ref, pq_ref):
    h = _moe_combine(h_ref, y0_ref, y1_ref, gt_ref)
    ho_ref[...] = h
    _project(h, g_ref, wr_ref, ws_ref, wl_ref, pr_ref, ps_ref, pq_ref)


def _inproj(x2, g, w_ret, w_ssd, w_lru, moe=None):
    t, d = x2.shape
    tm = ROW_TILE
    nt = t // tm
    n_r, n_s, n_l = w_ret.shape[1], w_ssd.shape[1], w_lru.shape[1]
    full = lambda n: pl.BlockSpec((d, n), lambda i: (0, 0))
    row = lambda n: pl.BlockSpec((tm, n), lambda i: (i, 0))
    weights = [pl.BlockSpec((1, d), lambda i: (0, 0)), full(n_r), full(n_s), full(n_l)]
    outs = [row(n_r), row(n_s), row(n_l)]
    shapes = [jax.ShapeDtypeStruct((t, n), F32) for n in (n_r, n_s, n_l)]
    if moe is None:
        return pl.pallas_call(
            _inproj_kernel, grid=(nt,), in_specs=[row(d)] + weights, out_specs=outs, out_shape=shapes,
            compiler_params=_cparams(("parallel",)), name="inproj",
        )(x2, g.reshape(1, d), w_ret, w_ssd, w_lru)
    y, gates = moe
    tiles = lambda off: pl.BlockSpec((tm * SUBLANE, LANE), lambda i: (off + i, 0))
    return pl.pallas_call(
        _combine_inproj_kernel, grid=(nt,),
        in_specs=[row(d), tiles(0), tiles(nt), row(TOP_K)] + weights,
        out_specs=[row(d)] + outs, out_shape=[jax.ShapeDtypeStruct((t, d), F32)] + shapes,
        compiler_params=_cparams(("parallel",)), name="combine_inproj",
    )(x2, y, y, gates[:TOP_K].T, g.reshape(1, d), w_ret, w_ssd, w_lru)


def _ret_kernel(q_ref, k_ref, v_ref, g_ref, cos_ref, sin_ref, d2_ref, dec_ref, gc_ref,
                o_ref, qs_ref, ks_ref, acc_ref):
    s = q_ref.shape[0]
    n = s // CHUNK
    lane = lax.broadcasted_iota(jnp.int32, (1, LANE), 1)
    mq0 = ((lane // 32) % 2 == 0).astype(F32)
    mq1 = 1.0 - mq0
    mv0 = (lane < HEAD_DIM).astype(F32)
    mv1 = 1.0 - mv0
    krow = lax.broadcasted_iota(jnp.int32, (LANE, LANE), 0)
    vcol = lax.broadcasted_iota(jnp.int32, (LANE, LANE), 1)
    bd = (((krow // 32) % 2) == (vcol // HEAD_DIM)).astype(F32)
    zeta_f = dec_ref[0]
    zeta_b = dec_ref[1]
    xi_f = dec_ref[2]
    xi_b = dec_ref[3]
    gc = gc_ref[...]
    d2 = d2_ref[...]

    def rot(x, c):
        rows = pl.ds(c * CHUNK, CHUNK)
        return x * cos_ref[rows, :] + pltpu.roll(x, 64, 1) * sin_ref[rows, :]

    def fwd(c, st):
        rows = pl.ds(pl.multiple_of(c * CHUNK, CHUNK), CHUNK)
        qc = rot(q_ref[rows, :], c)
        kc = rot(k_ref[rows, :], c) * (HEAD_DIM ** -0.5)
        vc = v_ref[rows, :]
        qs_ref[rows, :] = qc
        ks_ref[rows, :] = kc
        q2 = jnp.concatenate([qc * mq0, qc * mq1], axis=0)
        sc = lax.dot_general(q2, kc, (((1,), (1,)), ((), ())), preferred_element_type=F32) * d2
        o2 = jnp.dot(sc, vc, preferred_element_type=F32)
        o = o2[:CHUNK] * mv0 + o2[CHUNK:] * mv1
        o = o + jnp.dot(qc, st, preferred_element_type=F32) * xi_f
        acc_ref[rows, :] = o
        upd = jnp.dot(kc.T, vc * zeta_f, preferred_element_type=F32) * bd
        return st * gc + upd

    lax.fori_loop(0, n, fwd, jnp.zeros((LANE, LANE), F32), unroll=16)

    def bwd(i, st):
        c = n - 1 - i
        rows = pl.ds(pl.multiple_of(c * CHUNK, CHUNK), CHUNK)
        qc = qs_ref[rows, :]
        kc = ks_ref[rows, :]
        vc = v_ref[rows, :]
        o = acc_ref[rows, :] + jnp.dot(qc, st, preferred_element_type=F32) * xi_b
        oo = o * o
        ss0 = jnp.sum(oo * mv0, axis=1, keepdims=True)
        ss1 = jnp.sum(oo * mv1, axis=1, keepdims=True)
        scale = lax.rsqrt(ss0 * (1.0 / HEAD_DIM) + EPS) * mv0 + lax.rsqrt(ss1 * (1.0 / HEAD_DIM) + EPS) * mv1
        gate = g_ref[rows, :]
        gate = gate * jax.nn.sigmoid(gate)
        o_ref[rows, :] = (gate * (o * scale)).astype(o_ref.dtype)
        upd = jnp.dot(kc.T, vc * zeta_b, preferred_element_type=F32) * bd
        return st * gc + upd

    lax.fori_loop(0, n, bwd, jnp.zeros((LANE, LANE), F32), unroll=16)


def _ret_constants(s):
    half = HEAD_DIM // 2
    inv = ROPE_BASE ** (-jnp.arange(half, dtype=F32) / half)
    ang = jnp.arange(s, dtype=F32)[:, None] * inv[None, :]
    cos = jnp.cos(ang)
    sin = jnp.sin(ang)
    cos2 = jnp.concatenate([cos, cos, cos, cos], axis=1)
    sin2 = jnp.concatenate([-sin, -sin, sin, sin], axis=1)
    log_gamma = jnp.log1p(-jnp.exp2(-5.0 - jnp.arange(RET_HEADS, dtype=F32)))
    lg_pair = log_gamma.reshape(RET_HEADS // 2, 2)
    idx = jnp.arange(CHUNK, dtype=F32)
    adiff = jnp.abs(idx[:, None] - idx[None, :])
    d2 = jnp.exp(adiff[None, None] * lg_pair[:, :, None, None])
    d2 = d2.reshape(RET_HEADS // 2, 2 * CHUNK, CHUNK)
    lg_lane = jnp.repeat(lg_pair, HEAD_DIM, axis=1)
    pos = idx[None, :, None]
    lgl = lg_lane[:, None, :]
    dec = jnp.stack([jnp.exp((CHUNK - 1 - pos) * lgl), jnp.exp(pos * lgl),
                     jnp.exp((pos + 1) * lgl), jnp.exp((CHUNK - pos) * lgl)], axis=1)
    gc = jnp.exp(CHUNK * lg_lane)[:, None, :]
    return cos2, sin2, d2, dec, gc


def _retention(p_ret, b, s):
    p3 = p_ret.reshape(b, s, 4 * RET_WIDTH)
    cos2, sin2, d2, dec, gc = _ret_constants(s)
    npair = RET_HEADS // 2
    col = lambda off: pl.BlockSpec((None, s, LANE), lambda i, p: (i, 0, off + p))
    return pl.pallas_call(
        _ret_kernel,
        grid=(b, npair),
        in_specs=[col(0), col(npair), col(2 * npair), col(3 * npair),
                  pl.BlockSpec((s, LANE), lambda i, p: (0, 0)),
                  pl.BlockSpec((s, LANE), lambda i, p: (0, 0)),
                  pl.BlockSpec((None, 2 * CHUNK, CHUNK), lambda i, p: (p, 0, 0)),
                  pl.BlockSpec((None, 4, CHUNK, LANE), lambda i, p: (p, 0, 0, 0)),
                  pl.BlockSpec((None, 1, LANE), lambda i, p: (p, 0, 0))],
        out_specs=pl.BlockSpec((None, s, LANE), lambda i, p: (i, 0, p)),
        out_shape=jax.ShapeDtypeStruct((b, s, RET_WIDTH), BF16),
        scratch_shapes=[pltpu.VMEM((s, LANE), F32), pltpu.VMEM((s, LANE), F32), pltpu.VMEM((s, LANE), F32)],
        compiler_params=_cparams(("parallel", "parallel")),
        name="retention",
    )(p3, p3, p3, p3, cos2, sin2, d2, dec, gc)


_PAD = 8


def _softplus(x):
    return jnp.maximum(x, 0.0) + jnp.log1p(jnp.exp(-jnp.abs(x)))


def _silu(x):
    return x * jax.nn.sigmoid(x)


def _expand_heads(cols, base, lane_head):
    acc = jnp.zeros((cols.shape[0], SSD_GW), F32)
    for r in range(SSD_HPG):
        acc = jnp.where(lane_head == r, cols[:, base + r:base + r + 1], acc)
    return acc


def _dwconv_chunk(xp_ref, cw_ref, cb_ref, c):
    rows = CHUNK + 2 * _PAD
    win = xp_ref[pl.ds(pl.multiple_of(c * CHUNK, CHUNK), rows), :]
    acc = cb_ref[...]
    for t in range(CONV_WIDTH):
        shift = (CONV_WIDTH // 2 - t) % rows
        tap = win if shift == 0 else pltpu.roll(win, shift, 0)
        acc = acc + tap[_PAD:_PAD + CHUNK, :] * cw_ref[t:t + 1, :]
    return acc


def _ssd_kernel(xbc_ref, z_ref, dt_ref, cw_ref, cb_ref, dtb_ref, alog_ref, dsk_ref, nw_ref, o_ref,
                xp_ref, xc_ref, y_ref, pc_ref, dts_ref):
    s = z_ref.shape[0]
    n = s // CHUNK
    w = xbc_ref.shape[1]
    xp_ref[0:_PAD, :] = jnp.zeros((_PAD, w), F32)
    xp_ref[_PAD + s:2 * _PAD + s, :] = jnp.zeros((_PAD, w), F32)

    def copy(c, carry):
        rows = pl.ds(pl.multiple_of(c * CHUNK, CHUNK), CHUNK)
        xp_ref[pl.ds(pl.multiple_of(c * CHUNK + _PAD, _PAD), CHUNK), :] = xbc_ref[rows, :]
        return carry

    lax.fori_loop(0, n, copy, 0)

    lane_head = lax.broadcasted_iota(jnp.int32, (1, SSD_GW), 1) // HEAD_DIM
    ii = lax.broadcasted_iota(jnp.int32, (CHUNK, CHUNK), 0)
    jj = lax.broadcasted_iota(jnp.int32, (CHUNK, CHUNK), 1)
    lower = ii >= jj
    tri = lower.astype(F32)
    a_neg = -jnp.exp(alog_ref[...])
    dt_bias = dtb_ref[...]

    def fwd(c, hf):
        rows = pl.ds(pl.multiple_of(c * CHUNK, CHUNK), CHUNK)
        xc = _silu(_dwconv_chunk(xp_ref, cw_ref, cb_ref, c))
        xc_ref[rows, :] = xc
        xs = xc[:, 0:SSD_GW]
        bm = xc[:, SSD_GW:SSD_GW + SSD_STATE]
        cm = xc[:, SSD_GW + SSD_STATE:]
        dt = _softplus(dt_ref[rows, :] + dt_bias)
        dts_ref[rows, :] = dt
        la = dt * a_neg
        p = jnp.dot(tri, la, precision=lax.Precision.HIGHEST, preferred_element_type=F32)
        pc_ref[rows, :] = p
        e = p - la
        pt = p.T
        et = e.T
        dtt = dt.T
        g = lax.dot_general(cm, bm, (((1,), (1,)), ((), ())), preferred_element_type=F32)
        y = jnp.zeros((CHUNK, SSD_GW), F32)
        for r in range(SSD_HPG):
            arg = jnp.where(lower, p[:, r:r + 1] - pt[r:r + 1, :], et[3 + r:4 + r, :] - e[:, 3 + r:4 + r])
            dtj = jnp.where(lower, dtt[r:r + 1, :], dtt[3 + r:4 + r, :])
            m = g * (jnp.exp(arg) * dtj)
            y = y + jnp.dot(m, jnp.where(lane_head == r, xs, 0.0), preferred_element_type=F32)
        pf = _expand_heads(p, 0, lane_head)
        pf_last = _expand_heads(p[CHUNK - 1:CHUNK, :], 0, lane_head)
        dtf = _expand_heads(dt, 0, lane_head)
        y = y + jnp.dot(cm, hf, preferred_element_type=F32) * jnp.exp(pf)
        y_ref[rows, :] = y
        upd = jnp.dot(bm.T, xs * (jnp.exp(pf_last - pf) * dtf), preferred_element_type=F32)
        return hf * jnp.exp(pf_last) + upd

    lax.fori_loop(0, n, fwd, jnp.zeros((SSD_STATE, SSD_GW), F32), unroll=2)

    dsk = dsk_ref[...]
    nw = nw_ref[...]

    def bwd(i, hb):
        c = n - 1 - i
        rows = pl.ds(pl.multiple_of(c * CHUNK, CHUNK), CHUNK)
        xs = xc_ref[rows, 0:SSD_GW]
        bm = xc_ref[rows, SSD_GW:SSD_GW + SSD_STATE]
        cm = xc_ref[rows, SSD_GW + SSD_STATE:]
        dt = dts_ref[rows, :]
        p = pc_ref[rows, :]
        e = p - dt * a_neg
        eb = _expand_heads(e, 3, lane_head)
        tb = _expand_heads(p[CHUNK - 1:CHUNK, :], 3, lane_head)
        dtb = _expand_heads(dt, 3, lane_head)
        y = y_ref[rows, :] + jnp.dot(cm, hb, preferred_element_type=F32) * jnp.exp(tb - eb)
        y = (y + dsk * xs) * _silu(z_ref[rows, :])
        ms = jnp.sum(y * y, axis=1, keepdims=True) * (1.0 / (SSD_HPG * HEAD_DIM))
        o_ref[rows, :] = (y * lax.rsqrt(ms + EPS) * nw).astype(o_ref.dtype)
        upd = jnp.dot(bm.T, xs * (jnp.exp(eb) * dtb), preferred_element_type=F32)
        return hb * jnp.exp(tb) + upd

    lax.fori_loop(0, n, bwd, jnp.zeros((SSD_STATE, SSD_GW), F32), unroll=2)


_SSD_XBC_W = SSD_GW + 2 * SSD_STATE
_SSD_SLAB = SSD_GROUPS * (_SSD_XBC_W + SSD_GW + LANE)


def _ssd_xbc_cols(a, off):
    gw = SSD_HPG * HEAD_DIM
    zero = jnp.zeros(a.shape[:-1] + (SSD_GW - gw,), a.dtype)
    out = []
    for g in range(SSD_GROUPS):
        b0 = off + SSD_WIDTH + g * SSD_STATE
        c0 = b0 + SSD_GROUPS * SSD_STATE
        out += [a[..., off + g * gw:off + (g + 1) * gw], zero, a[..., b0:b0 + SSD_STATE], a[..., c0:c0 + SSD_STATE]]
    return jnp.concatenate(out, axis=-1)


def _ssd_slab_cols(w):
    gw = SSD_HPG * HEAD_DIM
    zero = w.shape[1]
    pad = lambda k: np.full(k, zero, np.int64)
    xbc, zz, dtc = [], [], []
    for g in range(SSD_GROUPS):
        b0 = _OFF_XBC + SSD_WIDTH + g * SSD_STATE
        c0 = b0 + SSD_GROUPS * SSD_STATE
        xbc += [_OFF_XBC + g * gw + np.arange(gw), pad(SSD_GW - gw), b0 + np.arange(SSD_STATE), c0 + np.arange(SSD_STATE)]
        zz += [_OFF_Z + g * gw + np.arange(gw), pad(SSD_GW - gw)]
        heads = _OFF_DT + g * SSD_HPG + np.arange(SSD_HPG)
        dtc += [heads, heads, pad(LANE - 2 * SSD_HPG)]
    w0 = jnp.concatenate([w, jnp.zeros((w.shape[0], 1), w.dtype)], axis=1)
    return w0[:, np.concatenate(xbc + zz + dtc)]


def _ssd(p_ssd, b, s, conv_w, conv_b, dt_bias, a_log, d_skip, norm_w):
    p3 = p_ssd.reshape(b, s, _SSD_SLAB)
    cw = _ssd_xbc_cols(conv_w, 0)
    cb = _ssd_xbc_cols(conv_b[None, :], 0)
    gw = SSD_HPG * HEAD_DIM
    zpad = jnp.zeros((SSD_GROUPS, LANE - 2 * SSD_HPG), F32)
    grp = lambda v: jnp.concatenate([v[0].reshape(SSD_GROUPS, SSD_HPG), v[1].reshape(SSD_GROUPS, SSD_HPG), zpad], axis=1)
    dtb = grp(dt_bias)[:, None, :]
    alog = grp(a_log)[:, None, :]
    lanes = lambda v: jnp.pad(v.reshape(SSD_GROUPS, gw), ((0, 0), (0, SSD_GW - gw)))[:, None, :]
    dsk = lanes(jnp.repeat(d_skip, HEAD_DIM))
    nw = lanes(norm_w)
    nxb = _SSD_XBC_W // LANE
    grid_spec = dict(
        grid=(b, SSD_GROUPS),
        in_specs=[pl.BlockSpec((None, s, _SSD_XBC_W), lambda i, g: (i, 0, g)),
                  pl.BlockSpec((None, s, SSD_GW), lambda i, g: (i, 0, SSD_GROUPS * _SSD_XBC_W // SSD_GW + g)),
                  pl.BlockSpec((None, s, LANE), lambda i, g: (i, 0, SSD_GROUPS * (nxb + SSD_GW // LANE) + g)),
                  pl.BlockSpec((CONV_WIDTH, _SSD_XBC_W), lambda i, g: (0, g)),
                  pl.BlockSpec((1, _SSD_XBC_W), lambda i, g: (0, g)),
                  pl.BlockSpec((None, 1, LANE), lambda i, g: (g, 0, 0)),
                  pl.BlockSpec((None, 1, LANE), lambda i, g: (g, 0, 0)),
                  pl.BlockSpec((None, 1, SSD_GW), lambda i, g: (g, 0, 0)),
                  pl.BlockSpec((None, 1, SSD_GW), lambda i, g: (g, 0, 0))],
        out_specs=pl.BlockSpec((None, s, SSD_GW), lambda i, g: (i, 0, g)),
    )
    return pl.pallas_call(
        _ssd_kernel,
        out_shape=jax.ShapeDtypeStruct((b, s, SSD_GROUPS * SSD_GW), BF16),
        scratch_shapes=[pltpu.VMEM((s + 2 * _PAD, _SSD_XBC_W), F32), pltpu.VMEM((s, _SSD_XBC_W), F32),
                        pltpu.VMEM((s, SSD_GW), F32), pltpu.VMEM((s, LANE), F32), pltpu.VMEM((s, LANE), F32)],
        compiler_params=_cparams(("parallel", "parallel")),
        name="ssd",
        **grid_spec,
    )(p3, p3, p3, cw, cb, dtb, alog, dsk, nw)


def _scan_chunk(a, u, h, reverse):
    n_groups = CHUNK // SUBLANE
    width = a.shape[1]
    a = a.reshape(n_groups, SUBLANE, width)
    u = u.reshape(n_groups, SUBLANE, width)
    sub = lax.broadcasted_iota(jnp.int32, (1, SUBLANE, 1), 1)
    d = 1
    while d < SUBLANE:
        keep = (sub < SUBLANE - d) if reverse else (sub >= d)
        shift = SUBLANE - d if reverse else d
        a_sh = jnp.where(keep, pltpu.roll(a, shift, 1), 1.0)
        u_sh = jnp.where(keep, pltpu.roll(u, shift, 1), 0.0)
        u = a * u_sh + u
        a = a * a_sh
        d *= 2
    a = a.reshape(CHUNK, width)
    u = u.reshape(CHUNK, width)
    out = [None] * n_groups
    for g in (range(n_groups - 1, -1, -1) if reverse else range(n_groups)):
        rows = slice(g * SUBLANE, (g + 1) * SUBLANE)
        hg = u[rows] + a[rows] * h
        out[g] = hg
        h = hg[0:1] if reverse else hg[SUBLANE - 1:SUBLANE]
    return jnp.concatenate(out, axis=0), h


def _lru_kernel(x_ref, cw_ref, cb_ref, wg_ref, bg_ref, lam_ref, o_ref, xp_ref, hf_ref, ab_ref, ub_ref):
    s = x_ref.shape[0]
    n = s // CHUNK
    w = LRU_WIDTH
    xp_ref[0:_PAD, :] = jnp.zeros((_PAD, w), F32)
    xp_ref[_PAD + s:2 * _PAD + s, :] = jnp.zeros((_PAD, w), F32)

    def copy(c, carry):
        rows = pl.ds(pl.multiple_of(c * CHUNK, CHUNK), CHUNK)
        xp_ref[pl.ds(pl.multiple_of(c * CHUNK + _PAD, _PAD), CHUNK), :] = x_ref[rows, 0:w]
        return carry

    lax.fori_loop(0, n, copy, 0)
    nsp = -LRU_C * _softplus(-lam_ref[...])

    def gates(pre, xc, k):
        r = jax.nn.sigmoid(pre[:, 2 * k * w:(2 * k + 1) * w])
        i = jax.nn.sigmoid(pre[:, (2 * k + 1) * w:(2 * k + 2) * w])
        log_a = r * nsp[k:k + 1, :]
        a = jnp.exp(log_a)
        return a, jnp.sqrt(-jnp.tanh(log_a) * (a * a + 1.0)) * (i * xc)

    def fwd(c, h):
        rows = pl.ds(pl.multiple_of(c * CHUNK, CHUNK), CHUNK)
        xc = _dwconv_chunk(xp_ref, cw_ref, cb_ref, c)
        pre = jnp.dot(xc.astype(BF16), wg_ref[...], preferred_element_type=F32) + bg_ref[...]
        a_b, u_b = gates(pre, xc, 1)
        ab_ref[rows, :] = a_b
        ub_ref[rows, :] = u_b
        a_f, u_f = gates(pre, xc, 0)
        hc, h = _scan_chunk(a_f, u_f, h, False)
        hf_ref[rows, :] = hc
        return h

    lax.fori_loop(0, n, fwd, jnp.zeros((1, w), F32))

    def bwd(i, h):
        c = n - 1 - i
        rows = pl.ds(pl.multiple_of(c * CHUNK, CHUNK), CHUNK)
        hc, h = _scan_chunk(ab_ref[rows, :], ub_ref[rows, :], h, True)
        gate = x_ref[rows, w:2 * w]
        o_ref[rows, :] = ((hf_ref[rows, :] + hc) * jax.nn.gelu(gate)).astype(o_ref.dtype)
        return h

    lax.fori_loop(0, n, bwd, jnp.zeros((1, w), F32))


def _block_diag(wblk):
    eye = jnp.eye(LRU_BLOCKS, dtype=wblk.dtype)
    return jnp.einsum('hij,hk->hikj', wblk, eye).reshape(LRU_WIDTH, LRU_WIDTH)


def _lru(p_lru, b, s, conv_w, conv_b, lam, wa, ba, wx, bx):
    p3 = p_lru.reshape(b, s, 2 * LRU_WIDTH)
    wg = jnp.concatenate([_block_diag(wa[0]), _block_diag(wx[0]), _block_diag(wa[1]), _block_diag(wx[1])], axis=1).astype(BF16)
    bg = jnp.concatenate([ba[0], bx[0], ba[1], bx[1]])[None, :]
    w = LRU_WIDTH
    const = lambda shape: pl.BlockSpec(shape, lambda i: (0,) * len(shape))
    return pl.pallas_call(
        _lru_kernel,
        grid=(b,),
        in_specs=[pl.BlockSpec((None, s, 2 * w), lambda i: (i, 0, 0)),
                  const((CONV_WIDTH, w)), const((1, w)), const((w, 4 * w)), const((1, 4 * w)), const((2, w))],
        out_specs=pl.BlockSpec((None, s, w), lambda i: (i, 0, 0)),
        out_shape=jax.ShapeDtypeStruct((b, s, w), BF16),
        scratch_shapes=[pltpu.VMEM((s + 2 * _PAD, w), F32), pltpu.VMEM((s, w), F32),
                        pltpu.VMEM((s, w), F32), pltpu.VMEM((s, w), F32)],
        compiler_params=_cparams(("parallel",)),
        name="rglru",
    )(p3, conv_w, conv_b[None, :], wg, bg, lam)


def _outproj_weights(w_out):
    gw = SSD_HPG * HEAD_DIM
    zero = jnp.zeros((SSD_GW - gw, w_out.shape[1]), w_out.dtype)
    rows = [p for g in range(SSD_GROUPS) for p in (w_out[RET_WIDTH + g * gw:RET_WIDTH + (g + 1) * gw], zero)]
    w_ssd = jnp.concatenate(rows, axis=0)
    return (w_out[:RET_WIDTH].astype(BF16), w_ssd.astype(BF16), w_out[RET_WIDTH + SSD_WIDTH:].astype(BF16))


def _kv_kernel(m_ref, g_ref, w_ref, o_ref):
    mn = _rms(m_ref[...], g_ref[...]).astype(BF16)
    o_ref[...] = jnp.dot(mn, w_ref[...], preferred_element_type=F32).astype(o_ref.dtype)


def _kv_proj(mem2, g, wkv):
    t, d = mem2.shape
    n = wkv.shape[1]
    tm = 256
    return pl.pallas_call(
        _kv_kernel,
        grid=(t // tm,),
        in_specs=[pl.BlockSpec((tm, d), lambda i: (i, 0)), pl.BlockSpec((1, d), lambda i: (0, 0)),
                  pl.BlockSpec((d, n), lambda i: (0, 0))],
        out_specs=pl.BlockSpec((tm, n), lambda i: (i, 0)),
        out_shape=jax.ShapeDtypeStruct((t, n), BF16),
        compiler_params=_cparams(("parallel",)),
        name="kvproj",
    )(mem2, g.reshape(1, d), wkv)


def _xattn_tile(h, g, wq_ref, k_ref, v_ref, wo_ref):
    xn = _rms(h, g).astype(BF16)
    q = jnp.dot(xn, wq_ref[...], preferred_element_type=F32).astype(BF16)
    outs = []
    for hd in range(CA_HEADS):
        cols = slice(hd * CA_HEAD_DIM, (hd + 1) * CA_HEAD_DIM)
        sc = lax.dot_general(q[:, cols], k_ref[:, cols], (((1,), (1,)), ((), ())), preferred_element_type=F32)
        sc = sc * (CA_HEAD_DIM ** -0.5)
        e = jnp.exp(sc - jnp.max(sc, axis=-1, keepdims=True))
        p = e / jnp.sum(e, axis=-1, keepdims=True)
        outs.append(jnp.dot(p.astype(BF16), v_ref[:, cols], preferred_element_type=F32).astype(BF16))
    o = jnp.concatenate(outs, axis=-1)
    return h + jnp.dot(o, wo_ref[...], preferred_element_type=F32)


_ROUTER_ROWS = SUBLANE + N_EXPERTS


def _first_argmax(v, row):
    m = jnp.max(v, axis=0, keepdims=True)
    return m, jnp.min(jnp.where(v == m, row, SUBLANE), axis=0, keepdims=True)


def _router_tile(h, g, w_ref, b_ref):
    xn = _rms(h, g).astype(BF16)
    lg = lax.dot_general(w_ref[...], xn, (((1,), (1,)), ((), ())), preferred_element_type=F32) + b_ref[...]
    tm = lg.shape[1]
    row = lax.broadcasted_iota(jnp.int32, (SUBLANE, tm), 0)
    gl = jnp.where(row < N_GROUPS, lg[0:SUBLANE], -jnp.inf)
    ge = jnp.exp(gl - jnp.max(gl, axis=0, keepdims=True))
    gp_all = ge / jnp.sum(ge, axis=0, keepdims=True)
    gp, gi = _first_argmax(gp_all, row)
    el = lg[SUBLANE:2 * SUBLANE]
    for g in range(1, N_GROUPS):
        el = jnp.where(gi == g, lg[SUBLANE * (g + 1):SUBLANE * (g + 2)], el)
    ee = jnp.exp(el - jnp.max(el, axis=0, keepdims=True))
    ep = ee / jnp.sum(ee, axis=0, keepdims=True)
    v1, i1 = _first_argmax(ep, row)
    v2, i2 = _first_argmax(jnp.where(row == i1, -1.0, ep), row)
    den = v1 + v2
    ids = jnp.where(row == 0, gi * EXPERTS_PER_GROUP + i1, jnp.where(row == 1, gi * EXPERTS_PER_GROUP + i2, 0))
    gates = jnp.where(row == 0, gp * v1 / den, jnp.where(row == 1, gp * v2 / den, 0.0))
    return ids, gates


assert D_MODEL == SUBLANE * LANE


def _token_tile(r):
    return pl.ds(pl.multiple_of(r * SUBLANE, SUBLANE), SUBLANE)


def _rows_to_tiles(ref, x):
    for j in range(SUBLANE):
        ref[pl.ds(j, x.shape[0], stride=SUBLANE), :] = x[:, j * LANE:(j + 1) * LANE]


def _tiles_to_rows(ref, n):
    return jnp.concatenate([ref[pl.ds(j, n, stride=SUBLANE), :] for j in range(SUBLANE)], axis=1)


def _post_kernel(h_ref, a_ref, b_ref, c_ref, wa_ref, wb_ref, wc_ref, gca_ref, wq_ref, k_ref, v_ref, wo_ref,
                 gmoe_ref, wr_ref, br_ref, o_ref, ot_ref, ids_ref, gates_ref):
    h = h_ref[...]
    h = h + jnp.dot(a_ref[...], wa_ref[...], preferred_element_type=F32)
    h = h + jnp.dot(b_ref[...], wb_ref[...], preferred_element_type=F32)
    h = h + jnp.dot(c_ref[...], wc_ref[...], preferred_element_type=F32)
    h = _xattn_tile(h, gca_ref[...], wq_ref, k_ref, v_ref, wo_ref)
    o_ref[...] = h
    _rows_to_tiles(ot_ref, h)
    ids, gates = _router_tile(h, gmoe_ref[...], wr_ref, br_ref)
    ids_ref[...] = ids
    gates_ref[...] = gates


def _post_mixer(h2, b, s, o_ret, o_ssd, o_lru, w_ret, w_ssd, w_lru, g_ca, wq, kv, wo, g_moe, wg, bg, we, be):
    t, d = h2.shape
    m = kv.shape[0] // b
    ts = ROW_TILE
    nts = s // ts
    zrow = jnp.zeros((SUBLANE - N_GROUPS, d), F32)
    wr = jnp.concatenate([wg.T, zrow, we.T], axis=0).astype(BF16)
    br = jnp.concatenate([bg, jnp.zeros((SUBLANE - N_GROUPS,), F32), be])[:, None]
    row = lambda n: pl.BlockSpec((ts, n), lambda i, j: (i * nts + j, 0))
    const = lambda r, c: pl.BlockSpec((r, c), lambda i, j: (0, 0))
    ka, kb, kc = o_ret.shape[1], o_ssd.shape[1], o_lru.shape[1]
    lanes = pl.BlockSpec((SUBLANE, ts), lambda i, j: (0, i * nts + j))
    return pl.pallas_call(
        _post_kernel,
        grid=(b, nts),
        in_specs=[row(d), row(ka), row(kb), row(kc), const(ka, d), const(kb, d), const(kc, d),
                  const(1, d), const(d, d),
                  pl.BlockSpec((m, d), lambda i, j: (i, 0)),
                  pl.BlockSpec((m, d), lambda i, j: (i, 1)),
                  const(d, d), const(1, d), const(_ROUTER_ROWS, d), const(_ROUTER_ROWS, 1)],
        out_specs=[row(d), pl.BlockSpec((ts * SUBLANE, LANE), lambda i, j: (i * nts + j, 0)), lanes, lanes],
        out_shape=[jax.ShapeDtypeStruct((t, d), F32), jax.ShapeDtypeStruct((t * SUBLANE, LANE), F32),
                   jax.ShapeDtypeStruct((SUBLANE, t), jnp.int32), jax.ShapeDtypeStruct((SUBLANE, t), F32)],
        compiler_params=_cparams(("parallel", "parallel")),
        name="post_mixer",
    )(h2, o_ret, o_ssd, o_lru, w_ret, w_ssd, w_lru, g_ca.reshape(1, d), wq, kv, kv, wo,
      g_moe.reshape(1, d), wr, br)


MOE_BLK = 512
_DMA_GROUP = 8


_ASSIGN_BITS = 17


def _dispatch_plan(ids, t):
    n_assign = t * TOP_K
    assert n_assign <= 1 << _ASSIGN_BITS
    e = ids[:TOP_K].reshape(n_assign)
    key =jnp.left_shift(e, _ASSIGN_BITS) | jnp.arange(n_assign, dtype=jnp.int32)
    order = jnp.sort(key) & ((1 << _ASSIGN_BITS) - 1)
    order = jnp.concatenate([order, jnp.zeros((MOE_BLK,), jnp.int32)])
    experts = jnp.arange(N_EXPERTS, dtype=jnp.int32)
    counts = jnp.sum((e[None, :] == experts[:, None]).astype(jnp.int32), axis=1)
    start = jnp.cumsum(counts) - counts
    nb = (counts + MOE_BLK - 1) // MOE_BLK
    blk_end = jnp.cumsum(nb)
    n_blocks = n_assign // MOE_BLK + N_EXPERTS
    j = jnp.arange(n_blocks, dtype=jnp.int32)
    be = jnp.minimum(jnp.sum((blk_end[None, :] <= j[:, None]).astype(jnp.int32), axis=1), N_EXPERTS - 1)
    onehot = (be[:, None] == experts[None, :]).astype(jnp.int32)
    pick = lambda v: jnp.sum(onehot * v[None, :], axis=1)
    r = j - pick(blk_end - nb)
    row_start = jnp.clip(pick(start) + r * MOE_BLK, 0, n_assign)
    n_rows = jnp.clip(pick(counts) - r * MOE_BLK, 0, MOE_BLK)
    tok = jnp.where(order >= t, order - t, order)
    rr = jnp.arange(MOE_BLK, dtype=jnp.int32)[None, :]
    dump = n_assign + (j % 2)[:, None] * MOE_BLK + rr
    dest = jnp.where(rr < n_rows[:, None], order[row_start[:, None] + rr], dump).reshape(n_blocks * MOE_BLK)
    return tok, dest, be, row_start, blk_end[-1:].astype(jnp.int32)


def _expert_kernel(be_ref, rs_ref, nu_ref, tok_ref, dst_ref, h_hbm, g_ref, w1_ref, w3_ref, w2_ref, y_hbm,
                   xbuf, ybuf, wb1, wb3, wb2, gsem, ssem, *, n_tokens):
    n_assign = TOP_K * n_tokens
    blk_rows = MOE_BLK * SUBLANE
    i = pl.program_id(0)
    n_used = nu_ref[0]
    slot = i % 2

    def gather(blk, s):
        base = rs_ref[blk]

        def body(grp, carry):
            for j in range(_DMA_GROUP):
                r = grp * _DMA_GROUP + j
                tok = tok_ref[base + r]
                pltpu.make_async_copy(h_hbm.at[_token_tile(tok), :], xbuf.at[s, _token_tile(r), :],
                                      gsem.at[s]).start(priority=j % 2)
            return carry
        lax.fori_loop(0, MOE_BLK // _DMA_GROUP, body, 0)

    def scatter(blk, s):
        base = blk * MOE_BLK

        def body(grp, carry):
            for j in range(_DMA_GROUP):
                r = grp * _DMA_GROUP + j
                dest = dst_ref[base + r]
                pltpu.make_async_copy(ybuf.at[s, _token_tile(r), :], y_hbm.at[_token_tile(dest), :],
                                      ssem.at[s]).start(priority=j % 2)
            return carry
        lax.fori_loop(0, MOE_BLK // _DMA_GROUP, body, 0)

    def wait_gather(s):
        pltpu.make_async_copy(h_hbm.at[pl.ds(0, blk_rows), :], xbuf.at[s], gsem.at[s]).wait()

    def wait_scatter(s):
        pltpu.make_async_copy(ybuf.at[s], y_hbm.at[pl.ds(0, blk_rows), :], ssem.at[s]).wait()

    @pl.when(i == 0)
    def _():
        ybuf[0] = jnp.zeros((blk_rows, LANE), F32)
        for s in range(2):
            dump = pl.ds((n_assign + s * MOE_BLK) * SUBLANE, blk_rows)
            fill = pltpu.make_async_copy(ybuf.at[0], y_hbm.at[dump, :], ssem.at[0])
            fill.start()
            fill.wait()
        gather(0, 0)

    @pl.when(i + 1 < n_used)
    def _():
        gather(i + 1, 1 - slot)

    @pl.when(i < n_used)
    def _():
        @pl.when((i == 0) | (be_ref[i] != be_ref[jnp.maximum(i - 1, 0)]))
        def _():
            wb1[...] = w1_ref[...].astype(BF16)
            wb3[...] = w3_ref[...].astype(BF16)
            wb2[...] = w2_ref[...].astype(BF16)

        wait_gather(slot)

        @pl.when(i >= 2)
        def _():
            wait_scatter(slot)

        xn = _rms(_tiles_to_rows(xbuf.at[slot], MOE_BLK), g_ref[...]).astype(BF16)
        h1 = jnp.dot(xn, wb1[...], preferred_element_type=F32)
        h3 = jnp.dot(xn, wb3[...], preferred_element_type=F32)
        hid = (_silu(h1) * h3).astype(BF16)
        _rows_to_tiles(ybuf.at[slot], jnp.dot(hid, wb2[...], preferred_element_type=F32))
        scatter(i, slot)

        @pl.when(i == n_used - 1)
        def _():
            wait_scatter(slot)

            @pl.when(i >= 1)
            def _():
                wait_scatter(1 - slot)


def _experts(h_tiles, g, tok, dest, block_e, row_start, n_used, w1, w3, w2, layer):
    t, d = h_tiles.shape[0] // SUBLANE, D_MODEL
    n_assign = t * TOP_K
    n_blocks = block_e.shape[0]
    de = w1.shape[3]
    wmap = lambda i, be, *_: (layer, be[i], 0, 0)
    grid_spec = pltpu.PrefetchScalarGridSpec(
        num_scalar_prefetch=5,
        grid=(n_blocks,),
        in_specs=[pl.BlockSpec(memory_space=pl.ANY),
                  pl.BlockSpec((1, d), lambda i, *_: (0, 0)),
                  pl.BlockSpec((None, None, d, de), wmap),
                  pl.BlockSpec((None, None, d, de), wmap),
                  pl.BlockSpec((None, None, de, d), wmap)],
        out_specs=pl.BlockSpec(memory_space=pl.ANY),
        scratch_shapes=[pltpu.VMEM((2, MOE_BLK * SUBLANE, LANE), F32), pltpu.VMEM((2, MOE_BLK * SUBLANE, LANE), F32),
                        pltpu.VMEM((d, de), BF16), pltpu.VMEM((d, de), BF16), pltpu.VMEM((de, d), BF16),
                        pltpu.SemaphoreType.DMA((2,)), pltpu.SemaphoreType.DMA((2,))],
    )
    return pl.pallas_call(
        functools.partial(_expert_kernel, n_tokens=t),
        grid_spec=grid_spec,
        out_shape=jax.ShapeDtypeStruct(((n_assign + 2 * MOE_BLK) * SUBLANE, LANE), F32),
        compiler_params=_cparams(("arbitrary",)),
        name="experts",
    )(block_e, row_start, n_used, tok, dest, h_tiles, g.reshape(1, d), w1, w3, w2)


def _moe_combine(h_ref, y0_ref, y1_ref, gt_ref):
    gt = gt_ref[...]
    tm = h_ref.shape[0]
    return h_ref[...] + _tiles_to_rows(y0_ref, tm) * gt[:, 0:1] + _tiles_to_rows(y1_ref, tm) * gt[:, 1:2]


def _combine_kernel(h_ref, y0_ref, y1_ref, gt_ref, g_ref, o_ref):
    o_ref[...] = _rms(_moe_combine(h_ref, y0_ref, y1_ref, gt_ref), g_ref[...])


def _combine_final(h2, y, gates, g_final):
    t, d = h2.shape
    tm = ROW_TILE
    nt = t // tm
    return pl.pallas_call(
        _combine_kernel,
        grid=(nt,),
        in_specs=[pl.BlockSpec((tm, d), lambda i: (i, 0)),
                  pl.BlockSpec((tm * SUBLANE, LANE), lambda i: (i, 0)),
                  pl.BlockSpec((tm * SUBLANE, LANE), lambda i: (nt + i, 0)),
                  pl.BlockSpec((tm, TOP_K), lambda i: (i, 0)),
                  pl.BlockSpec((1, d), lambda i: (0, 0))],
        out_specs=pl.BlockSpec((tm, d), lambda i: (i, 0)),
        out_shape=jax.ShapeDtypeStruct((t, d), F32),
        compiler_params=_cparams(("parallel",)),
        name="combine",
    )(h2, y, y, gates[:TOP_K].T, g_final.reshape(1, d))


def _moe_experts(h_tiles, ids, g, w1, w3, w2, layer):
    t = h_tiles.shape[0] // SUBLANE
    tok, dest, block_e, row_start, n_used = _dispatch_plan(ids, t)
    return _experts(h_tiles, g, tok, dest, block_e, row_start, n_used, w1, w3, w2, layer)


def _ret_cols(w):
    half = HEAD_DIM // 2
    pair = np.concatenate([np.arange(0, half), np.arange(HEAD_DIM, HEAD_DIM + half),
                           np.arange(half, HEAD_DIM), np.arange(HEAD_DIM + half, 2 * HEAD_DIM)])
    qperm = np.concatenate([2 * HEAD_DIM * p + pair for p in range(RET_HEADS // 2)])
    return w[:, np.concatenate([_OFF_Q + qperm, _OFF_K + qperm, np.arange(_OFF_V, _OFF_Z)])]


def kernel(x, mem, g_mix, w_in, w_out, ssd_conv_w, ssd_conv_b, ssd_dt_bias, ssd_a_log, ssd_d, ssd_norm, lru_conv_w, lru_conv_b, lru_lambda, lru_wa, lru_ba, lru_wx, lru_bx, g_ca, g_mem, ca_wq, ca_wkv, ca_wo, g_moe, moe_wg, moe_bg, moe_we, moe_be, moe_w1, moe_w3, moe_w2, g_final):
    b, s, d = x.shape
    depth = g_mix.shape[0]
    h = x.reshape(b * s, d)
    mem2 = mem.reshape(b * mem.shape[1], d)
    moe = None
    for l in range(depth):
        w_ret = _ret_cols(w_in[l]).astype(BF16)
        w_ssd = _ssd_slab_cols(w_in[l]).astype(BF16)
        w_lru = w_in[l][:, _OFF_XR:].astype(BF16)
        if moe is None:
            p_ret, p_ssd, p_lru = _inproj(h, g_mix[l], w_ret, w_ssd, w_lru)
        else:
            h, p_ret, p_ssd, p_lru = _inproj(h, g_mix[l], w_ret, w_ssd, w_lru, moe)
        o_ret = _retention(p_ret, b, s)
        o_ssd = _ssd(p_ssd, b, s, ssd_conv_w[l], ssd_conv_b[l], ssd_dt_bias[l], ssd_a_log[l], ssd_d[l], ssd_norm[l])
        o_lru = _lru(p_lru, b, s, lru_conv_w[l], lru_conv_b[l], lru_lambda[l], lru_wa[l], lru_ba[l], lru_wx[l], lru_bx[l])
        kv = _kv_proj(mem2, g_mem[l], ca_wkv[l].astype(BF16))
        h, h_tiles, ids, gates = _post_mixer(h, b, s, o_ret.reshape(b * s, -1), o_ssd.reshape(b * s, -1),
                                             o_lru.reshape(b * s, -1), *_outproj_weights(w_out[l]),
                                             g_ca[l], ca_wq[l].astype(BF16), kv, ca_wo[l].astype(BF16),
                                             g_moe[l], moe_wg[l], moe_bg[l], moe_we[l], moe_be[l])
        moe = (_moe_experts(h_tiles, ids, g_moe[l], moe_w1, moe_w3, moe_w2, l), gates)
    return _combine_final(h, *moe, g_final).reshape(b, s, d)
```
